```python
import math
import jax, jax.numpy as jnp
from jax import lax
import numpy as np

D_MODEL = 1024
BATCH = 8
SEQ = 2048
DEPTH = 1
DEC_BATCH = 128
DEC_SEQ = 1
PAST_LEN = 16384
PAGE_SIZE = 128

D_SSD = D_MODEL
SSD_HEADDIM = 64
H_SSD = D_SSD // SSD_HEADDIM
SSD_GROUPS = 2
SSD_STATE = 128
CONV_K = 4
CONV_DIM = D_SSD + 2 * SSD_GROUPS * SSD_STATE
SSD_CHUNK = 128
D_RWKV = D_MODEL
HD_RWKV = 64
H_RWKV = D_RWKV // HD_RWKV
W_RANK = 64
A_RANK = 64
G_RANK = 128
RWKV_PROJ = 3 * D_RWKV + W_RANK + A_RANK + G_RANK
RWKV_LN_EPS = 64e-5
D_IN = D_SSD + CONV_DIM + H_SSD + RWKV_PROJ + 2 * D_MODEL
N_MEM = 256
MEM_HEADS = 4
MEM_HD = D_MODEL // MEM_HEADS
N_KEYS = 128
N_EXPERTS = N_KEYS * N_KEYS
PEER_HEADS = 8
PEER_DKEY = 256
PEER_HALF = PEER_DKEY // 2
PEER_TOPK = 16
PEER_BLOCK = 128
NORM_EPS = 1e-6

kernel_name = 'hybrid_ssd_rwkv7_peer_decode_step'


def _rmsnorm(x, w):
    xf = x.astype(jnp.float32)
    y = xf * lax.rsqrt(jnp.mean(xf * xf, axis=-1, keepdims=True) + NORM_EPS)
    return (y * w.astype(jnp.float32)).astype(x.dtype)


def _split(x, sizes):
    offs = []
    acc = 0
    for s in sizes[:-1]:
        acc += s
        offs.append(acc)
    return jnp.split(x, offs, axis=-1)


def _segsum(a):
    cs = jnp.cumsum(a, axis=-1)
    t = a.shape[-1]
    mask = jnp.tril(jnp.ones((t, t), dtype=bool))
    return jnp.where(mask, cs[..., :, None] - cs[..., None, :], -jnp.inf)


def _ssd(xh, dt, a_head, bm, cm, init_state):
    b, L = xh.shape[0], xh.shape[1]
    l = SSD_CHUNK if L % SSD_CHUNK == 0 else L
    c = L // l
    g, e = SSD_GROUPS, H_SSD // SSD_GROUPS
    xc = (xh * dt[..., None]).reshape(b, c, l, g, e, SSD_HEADDIM)
    ad = jnp.moveaxis((dt * a_head).reshape(b, c, l, g, e), (1, 2), (3, 4))
    bc = bm.reshape(b, c, l, g, SSD_STATE)
    cc = cm.reshape(b, c, l, g, SSD_STATE)
    a_cs = jnp.cumsum(ad, axis=-1)
    lmat = jnp.exp(_segsum(ad))
    cb = jnp.einsum('bclgn,bcsgn->bgcls', cc, bc)
    y_diag = jnp.einsum('bgcls,bgecls,bcsgep->bclgep', cb, lmat, xc)
    decay_states = jnp.exp(a_cs[..., -1:] - a_cs)
    states = jnp.einsum('bclgn,bgecl,bclgep->bcgepn', bc, decay_states, xc)
    init = init_state.reshape(b, 1, g, e, SSD_HEADDIM, SSD_STATE)
    states = jnp.concatenate([init, states], axis=1)
    chunk_a = jnp.pad(a_cs[..., -1], ((0, 0), (0, 0), (0, 0), (1, 0)))
    decay_chunk = jnp.exp(_segsum(chunk_a))
    new_states = jnp.einsum('bgezc,bcgepn->bzgepn', decay_chunk, states)
    y_off = jnp.einsum('bclgn,bcgepn,bgecl->bclgep', cc, new_states[:, :-1], jnp.exp(a_cs))
    y = (y_diag + y_off).reshape(b, L, H_SSD, SSD_HEADDIM)
    final = new_states[:, -1].reshape(b, H_SSD, SSD_HEADDIM, SSD_STATE)
    return y, final


def _rwkv7_scan(r, decay, k, v, kk_a, kk_b, s0):
    def step(s, inp):
        r_t, w_t, k_t, v_t, a_t, b_t = inp
        sa = jnp.einsum('bhvk,bhk->bhv', s, a_t)
        s = s * w_t[:, :, None, :] + sa[..., None] * b_t[:, :, None, :] + v_t[..., None] * k_t[:, :, None, :]
        return s, jnp.einsum('bhvk,bhk->bhv', s, r_t)
    xs = tuple(jnp.moveaxis(t, 1, 0) for t in (r, decay, k, v, kk_a, kk_b))
    s_final, ys = lax.scan(step, s0, xs)
    return jnp.moveaxis(ys, 0, 1), s_final


def _gated_mixer(h, conv_prev, ssm_prev, shift_prev, wkv_prev, p):
    f32 = jnp.float32
    b, L, _ = h.shape
    proj = h @ p['w_in']
    z, xbc, dt_raw, rw, gate_raw = _split(proj, (D_SSD, CONV_DIM, H_SSD, RWKV_PROJ, 2 * D_MODEL))

    xbc_all = jnp.concatenate([conv_prev.astype(xbc.dtype), xbc], axis=1)
    conv = p['conv_b'] + sum(xbc_all[:, i:i + L] * p['conv_w'][i] for i in range(CONV_K))
    conv_new = xbc_all[:, L:]
    xs, bm, cm = _split(jax.nn.silu(conv.astype(f32)), (D_SSD, SSD_GROUPS * SSD_STATE, SSD_GROUPS * SSD_STATE))
    xs = xs.reshape(b, L, H_SSD, SSD_HEADDIM)
    dt = jax.nn.softplus(dt_raw.astype(f32) + p['dt_bias'].astype(f32))
    a_head = -jnp.exp(p['a_log'].astype(f32))
    y, ssm_new = _ssd(xs, dt, a_head, bm.reshape(b, L, SSD_GROUPS, SSD_STATE),
                      cm.reshape(b, L, SSD_GROUPS, SSD_STATE), ssm_prev.astype(f32))
    y = (y + p['d_skip'].astype(f32)[:, None] * xs).reshape(b, L, D_SSD)
    y_ssd = _rmsnorm(y * jax.nn.silu(z.astype(f32)), p['ssd_norm_w'])

    rw_prev = jnp.concatenate([shift_prev[:, None].astype(rw.dtype), rw[:, :-1]], axis=1)
    rws = (rw + (rw_prev - rw) * p['rwkv_mu']).astype(f32)
    shift_new = rw[:, -1]
    r, k, v, wl, al, gl = _split(rws, (D_RWKV, D_RWKV, D_RWKV, W_RANK, A_RANK, G_RANK))
    wlog = -jax.nn.softplus(-(p['rwkv_w0'] + jnp.tanh(wl) @ p['rwkv_w2'])) - 0.5
    decay = jnp.exp(-jnp.exp(wlog))
    a = jax.nn.sigmoid(p['rwkv_a0'] + al @ p['rwkv_a2'])
    g = jax.nn.sigmoid(gl) @ p['rwkv_g2']
    heads = lambda t: t.reshape(b, L, H_RWKV, HD_RWKV)
    kk = heads(k * p['rwkv_k_k'])
    kk = kk * lax.rsqrt(jnp.maximum(jnp.sum(kk * kk, axis=-1, keepdims=True), 1e-24))
    kh = heads(k * (1.0 + (a - 1.0) * p['rwkv_k_a']))
    rh, vh, ah = heads(r), heads(v), heads(a)
    yr, wkv_new = _rwkv7_scan(rh, heads(decay), kh, vh, -kk, kk * ah, wkv_prev.astype(f32))
    mean = jnp.mean(yr, axis=-1, keepdims=True)
    var = jnp.mean(jnp.square(yr - mean), axis=-1, keepdims=True)
    yr = ((yr - mean) * lax.rsqrt(var + RWKV_LN_EPS)).reshape(b, L, D_RWKV) * p['rwkv_ln_w'] + p['rwkv_ln_b']
    bonus = jnp.sum(rh * kh * p['rwkv_r_k'], axis=-1, keepdims=True) * vh
    y_rwkv = (yr + bonus.reshape(b, L, D_RWKV)) * g

    gate_ssd, gate_rwkv = jnp.split(jax.nn.sigmoid(gate_raw.astype(f32)), 2, axis=-1)
    merged = gate_ssd * y_ssd + gate_rwkv * y_rwkv
    out = merged.astype(h.dtype) @ p['w_out']
    return (out, conv_new.astype(h.dtype), ssm_new.astype(h.dtype),
            shift_new.astype(h.dtype), wkv_new.astype(h.dtype))


def _memory_kv(mem, mem_norm_w, w_mk, w_mv):
    b, n, _ = mem.shape
    m = _rmsnorm(mem, mem_norm_w)
    return ((m @ w_mk).reshape(b, n, MEM_HEADS, MEM_HD), (m @ w_mv).reshape(b, n, MEM_HEADS, MEM_HD))


def _cross_attn(h, mem_k, mem_v, w_mq, w_mo):
    b, L, _ = h.shape
    q = (h @ w_mq).reshape(b, L, MEM_HEADS, MEM_HD)
    s = jnp.einsum('blhd,bmhd->bhlm', q.astype(jnp.float32), mem_k.astype(jnp.float32)) * (MEM_HD ** -0.5)
    prob = jax.nn.softmax(s, axis=-1).astype(h.dtype)
    o = jnp.einsum('bhlm,bmhd->blhd', prob, mem_v.astype(h.dtype)).reshape(b, L, D_MODEL)
    return o @ w_mo


def _peer(h, w_pq, sub_keys, expert_u, expert_v):
    b, L, d = h.shape
    t = b * L
    pad = (-t) % PEER_BLOCK
    xt = jnp.pad(h.reshape(t, d), ((0, pad), (0, 0))).reshape(-1, PEER_BLOCK, d)

    def one_block(xb):
        q = (xb @ w_pq).reshape(PEER_BLOCK, PEER_HEADS, 2, PEER_HALF).astype(jnp.float32)
        s = jnp.einsum('thcd,hckd->thck', q, sub_keys.astype(jnp.float32))
        s1, i1 = lax.top_k(s[:, :, 0], PEER_TOPK)
        s2, i2 = lax.top_k(s[:, :, 1], PEER_TOPK)
        cand_s = (s1[..., :, None] + s2[..., None, :]).reshape(PEER_BLOCK, PEER_HEADS, PEER_TOPK * PEER_TOPK)
        cand_i = (i1[..., :, None] * N_KEYS + i2[..., None, :]).reshape(PEER_BLOCK, PEER_HEADS, PEER_TOPK * PEER_TOPK)
        top_s, pos = lax.top_k(cand_s, PEER_TOPK)
        idx = jnp.take_along_axis(cand_i, pos, axis=-1)
        gate = jax.nn.softmax(top_s, axis=-1)
        act = jax.nn.gelu(jnp.einsum('td,thkd->thk', xb, expert_u[idx]).astype(jnp.float32), approximate=False)
        coef = (gate * act).astype(xb.dtype)
        return jnp.einsum('thk,thkd->td', coef, expert_v[idx])

    out = lax.map(one_block, xt)
    return out.reshape(-1, d)[:t].reshape(b, L, d)


def _block(x, mem_k, mem_v, conv_prev, ssm_prev, shift_prev, wkv_prev, p):
    mix, conv_new, ssm_new, shift_new, wkv_new = _gated_mixer(
        _rmsnorm(x, p['norm_mix_w']), conv_prev, ssm_prev, shift_prev, wkv_prev, p)
    x = x + mix
    x = x + _cross_attn(_rmsnorm(x, p['norm_mem_w']), mem_k, mem_v, p['w_mq'], p['w_mo'])
    x = x + _peer(_rmsnorm(x, p['norm_ffn_w']), p['w_pq'], p['sub_keys'], p['expert_u'], p['expert_v'])
    return x, conv_new, ssm_new, shift_new, wkv_new


def setup_inputs(seed: int = 0) -> dict:
    key = jax.random.key(seed)
    ks = iter(jax.random.split(key, 64))
    f32 = jnp.float32
    L = DEPTH

    def nrm(shape, scale):
        return jax.random.normal(next(ks), shape, f32) * scale

    def unif(shape, lo, hi):
        return jax.random.uniform(next(ks), shape, f32, lo, hi)

    def gain(shape):
        return 1.0 + nrm(shape, 0.02)

    dt0 = jnp.exp(unif((L, H_SSD), math.log(1e-3), math.log(1e-1)))
    return {
        'x_prompt': nrm((BATCH, SEQ, D_MODEL), 1.0),
        'x_sample': nrm((DEC_BATCH, DEC_SEQ, D_MODEL), 1.0),
        'mem_prompt': nrm((BATCH, N_MEM, D_MODEL), 1.0),
        'state_ssm': nrm((L, DEC_BATCH, H_SSD, SSD_HEADDIM, SSD_STATE), 0.3),
        'state_conv': nrm((L, DEC_BATCH, CONV_K - 1, CONV_DIM), 1.0),
        'state_wkv': nrm((L, DEC_BATCH, H_RWKV, HD_RWKV, HD_RWKV), 0.3),
        'state_shift': nrm((L, DEC_BATCH, RWKV_PROJ), 1.0),
        'cache_mem_k': nrm((L, DEC_BATCH, N_MEM, MEM_HEADS, MEM_HD), 1.0),
        'cache_mem_v': nrm((L, DEC_BATCH, N_MEM, MEM_HEADS, MEM_HD), 1.0),
        'norm_mix_w': gain((L, D_MODEL)),
        'w_in': nrm((L, D_MODEL, D_IN), D_MODEL ** -0.5),
        'conv_w': nrm((L, CONV_K, CONV_DIM), CONV_K ** -0.5),
        'conv_b': nrm((L, CONV_DIM), 0.02),
        'dt_bias': dt0 + jnp.log(-jnp.expm1(-dt0)),
        'a_log': jnp.log(unif((L, H_SSD), 1.0, 16.0)),
        'd_skip': gain((L, H_SSD)),
        'ssd_norm_w': gain((L, D_SSD)),
        'rwkv_mu': unif((L, RWKV_PROJ), 0.0, 1.0),
        'rwkv_w0': unif((L, D_RWKV), -4.0, 0.0),
        'rwkv_w2': nrm((L, W_RANK, D_RWKV), 0.1),
        'rwkv_a0': nrm((L, D_RWKV), 0.1),
        'rwkv_a2': nrm((L, A_RANK, D_RWKV), 0.1),
        'rwkv_g2': nrm((L, G_RANK, D_RWKV), G_RANK ** -0.5),
        'rwkv_k_k': 0.85 + nrm((L, D_RWKV), 0.02),
        'rwkv_k_a': gain((L, D_RWKV)),
        'rwkv_r_k': nrm((L, H_RWKV, HD_RWKV), 0.1),
        'rwkv_ln_w': gain((L, D_RWKV)),
        'rwkv_ln_b': nrm((L, D_RWKV), 0.02),
        'w_out': nrm((L, D_MODEL, D_MODEL), D_MODEL ** -0.5),
        'norm_mem_w': gain((L, D_MODEL)),
        'mem_norm_w': gain((L, D_MODEL)),
        'w_mk': nrm((L, D_MODEL, D_MODEL), D_MODEL ** -0.5),
        'w_mv': nrm((L, D_MODEL, D_MODEL), D_MODEL ** -0.5),
        'w_mq': nrm((L, D_MODEL, D_MODEL), D_MODEL ** -0.5),
        'w_mo': nrm((L, D_MODEL, D_MODEL), D_MODEL ** -0.5),
        'norm_ffn_w': gain((L, D_MODEL)),
        'w_pq': nrm((L, D_MODEL, PEER_HEADS * PEER_DKEY), D_MODEL ** -0.5),
        'sub_keys': nrm((L, PEER_HEADS, 2, N_KEYS, PEER_HALF), PEER_HALF ** -0.5),
        'expert_u': nrm((L, N_EXPERTS, D_MODEL), D_MODEL ** -0.5),
        'expert_v': nrm((L, N_EXPERTS, D_MODEL), 0.25),
        'norm_final_w': gain((D_MODEL,)),
    }


def reference(x_prompt, x_sample, mem_prompt, state_ssm, state_conv, state_wkv, state_shift,
              cache_mem_k, cache_mem_v, norm_mix_w, w_in, conv_w, conv_b, dt_bias, a_log, d_skip,
              ssd_norm_w, rwkv_mu, rwkv_w0, rwkv_w2, rwkv_a0, rwkv_a2, rwkv_g2, rwkv_k_k, rwkv_k_a,
              rwkv_r_k, rwkv_ln_w, rwkv_ln_b, w_out, norm_mem_w, mem_norm_w, w_mk, w_mv, w_mq, w_mo,
              norm_ffn_w, w_pq, sub_keys, expert_u, expert_v, norm_final_w):
    bp = x_prompt.shape[0]
    xp, xs = x_prompt, x_sample
    ssm_p, conv_p, wkv_p, shift_p, mk_p, mv_p = [], [], [], [], [], []
    ssm_s, conv_s, wkv_s, shift_s = [], [], [], []
    for l in range(DEPTH):
        p = {
            'norm_mix_w': norm_mix_w[l], 'w_in': w_in[l], 'conv_w': conv_w[l], 'conv_b': conv_b[l],
            'dt_bias': dt_bias[l], 'a_log': a_log[l], 'd_skip': d_skip[l], 'ssd_norm_w': ssd_norm_w[l],
            'rwkv_mu': rwkv_mu[l], 'rwkv_w0': rwkv_w0[l], 'rwkv_w2': rwkv_w2[l], 'rwkv_a0': rwkv_a0[l],
            'rwkv_a2': rwkv_a2[l], 'rwkv_g2': rwkv_g2[l], 'rwkv_k_k': rwkv_k_k[l], 'rwkv_k_a': rwkv_k_a[l],
            'rwkv_r_k': rwkv_r_k[l], 'rwkv_ln_w': rwkv_ln_w[l], 'rwkv_ln_b': rwkv_ln_b[l], 'w_out': w_out[l],
            'norm_mem_w': norm_mem_w[l], 'w_mq': w_mq[l], 'w_mo': w_mo[l], 'norm_ffn_w': norm_ffn_w[l],
            'w_pq': w_pq[l], 'sub_keys': sub_keys[l], 'expert_u': expert_u[l], 'expert_v': expert_v[l],
        }
        mem_k, mem_v = _memory_kv(mem_prompt, mem_norm_w[l], w_mk[l], w_mv[l])
        xp, c_new, s_new, sh_new, w_new = _block(
            xp, mem_k, mem_v,
            jnp.zeros((bp, CONV_K - 1, CONV_DIM), xp.dtype),
            jnp.zeros((bp, H_SSD, SSD_HEADDIM, SSD_STATE), jnp.float32),
            jnp.zeros((bp, RWKV_PROJ), xp.dtype),
            jnp.zeros((bp, H_RWKV, HD_RWKV, HD_RWKV), jnp.float32), p)
        ssm_p.append(s_new)
        conv_p.append(c_new)
        wkv_p.append(w_new)
        shift_p.append(sh_new)
        mk_p.append(mem_k)
        mv_p.append(mem_v)
        xs, c_new, s_new, sh_new, w_new = _block(
            xs, cache_mem_k[l], cache_mem_v[l], state_conv[l], state_ssm[l], state_shift[l], state_wkv[l], p)
        ssm_s.append(s_new)
        conv_s.append(c_new)
        wkv_s.append(w_new)
        shift_s.append(sh_new)
    y_prompt = _rmsnorm(xp, norm_final_w)
    y_sample = _rmsnorm(xs, norm_final_w)
    ssm_prompt = jnp.stack(ssm_p)
    conv_prompt = jnp.stack(conv_p)
    wkv_prompt = jnp.stack(wkv_p)
    shift_prompt = jnp.stack(shift_p)
    mem_k_prompt = jnp.stack(mk_p)
    mem_v_prompt = jnp.stack(mv_p)
    ssm_sample = jnp.stack(ssm_s)
    conv_sample = jnp.stack(conv_s)
    wkv_sample = jnp.stack(wkv_s)
    shift_sample = jnp.stack(shift_s)
    return (y_prompt, y_sample, ssm_prompt, conv_prompt, wkv_prompt, shift_prompt, mem_k_prompt, mem_v_prompt,
            ssm_sample, conv_sample, wkv_sample, shift_sample)
```

```python
import functools

import jax
import jax.numpy as jnp
from jax import lax
from jax.experimental import pallas as pl
from jax.experimental.pallas import tpu as pltpu

F32 = jnp.float32
BF16 = jnp.bfloat16
HI = lax.Precision.HIGHEST

D_MODEL = 1024
H_SSD = 16
SSD_HEADDIM = 64
SSD_STATE = 128
CONV_K = 4
CONV_DIM = 1536
H_RWKV = 16
HD_RWKV = 64
RWKV_PROJ = 3328
RWKV_LN_EPS = 64e-5
N_MEM = 256
MEM_HEADS = 4
MEM_HD = 256
N_KEYS = 128
PEER_HEADS = 8
PEER_TOPK = 16
NORM_EPS = 1e-6

LANES = 128
SUBLANES = 8
VMEM_LIMIT = 56 * 1024 * 1024

D_IN_PAD = 8064
COL_Z, COL_XS, COL_R, COL_K, COL_V, COL_GS, COL_GR = 0, 1, 2, 3, 4, 5, 6
COL_BC = 14
COL_LR = 30
COL_DT = 62

SSD_CHUNK = 128
RWKV_CHUNK = 64
G_PITCH = 136


def _cparams(sem):
    return pltpu.CompilerParams(dimension_semantics=sem, vmem_limit_bytes=VMEM_LIMIT)


def _dot(a, b, prec=None):
    return jnp.dot(a, b, preferred_element_type=F32, precision=prec)


def _dot_nt(a, b, prec=None):
    return lax.dot_general(a, b, (((1,), (1,)), ((), ())), preferred_element_type=F32, precision=prec)


def _dot_tn(a, b, prec=None):
    return lax.dot_general(a, b, (((0,), (0,)), ((), ())), preferred_element_type=F32, precision=prec)


def _split_bf16(x):
    hi = x.astype(BF16)
    lo = (x - hi.astype(F32)).astype(BF16)
    return hi, lo


def _dot_onehot(x, sel_bf16):
    hi, lo = _split_bf16(x)
    return _dot(hi, sel_bf16) + _dot(lo, sel_bf16)


def _softplus(x):
    return jnp.maximum(x, 0.0) + jnp.log1p(jnp.exp(-jnp.abs(x)))


def _sigmoid(x):
    return 1.0 / (1.0 + jnp.exp(-x))


def _rmsnorm(x, w):
    return x * lax.rsqrt(jnp.mean(x * x, axis=-1, keepdims=True) + NORM_EPS) * w


def _iota(shape, dim):
    return lax.broadcasted_iota(jnp.int32, shape, dim)


def _linear_body(*refs, has_norm, has_res, passes):
    it = iter(refs)
    x_ref = next(it)
    nw_ref = next(it) if has_norm else None
    wh_ref = next(it)
    wl_ref = next(it) if passes == 3 else None
    res_ref = next(it) if has_res else None
    o_ref = next(it)
    x = x_ref[...]
    if has_norm:
        x = _rmsnorm(x, nw_ref[...])
    xh = x.astype(BF16)
    acc = _dot(xh, wh_ref[...])
    if passes == 3:
        xl = (x - xh.astype(F32)).astype(BF16)
        acc = acc + _dot(xl, wh_ref[...]) + _dot(xh, wl_ref[...])
    if has_res:
        acc = acc + res_ref[...]
    o_ref[...] = acc


def _linear(x, w, *, norm_w=None, res=None, passes=1, tm=512, tn=1024, name="linear"):
    m, k = x.shape
    n = w.shape[1]
    tm = min(tm, m)
    tn = min(tn, n)
    assert m % tm == 0 and n % tn == 0
    args = [x]
    specs = [pl.BlockSpec((tm, k), lambda i, j: (i, 0))]
    if norm_w is not None:
        args.append(norm_w.reshape(1, k))
        specs.append(pl.BlockSpec((1, k), lambda i, j: (0, 0)))
    if passes == 3:
        wh, wl = _split_bf16(w)
        args += [wh, wl]
        specs += [pl.BlockSpec((k, tn), lambda i, j: (0, j))] * 2
    else:
        args.append(w.astype(BF16))
        specs.append(pl.BlockSpec((k, tn), lambda i, j: (0, j)))
    if res is not None:
        args.append(res)
        specs.append(pl.BlockSpec((tm, tn), lambda i, j: (i, j)))
    body = functools.partial(_linear_body, has_norm=norm_w is not None, has_res=res is not None, passes=passes)
    return pl.pallas_call(
        body,
        grid=(m // tm, n // tn),
        in_specs=specs,
        out_specs=pl.BlockSpec((tm, tn), lambda i, j: (i, j)),
        out_shape=jax.ShapeDtypeStruct((m, n), F32),
        compiler_params=_cparams(("parallel", "arbitrary")),
        name=name,
    )(*args)


def _ssd_body(z_ref, xs_ref, bc_ref, dt_ref, convprev_ref, ssmprev_ref, convw_ref, convb_ref, dtb_ref,
              alog_ref, dskip_ref, normw_ref, expand_ref, y_ref, state_ref, ext_ref, *, l, n_valid):
    c = pl.program_id(1)

    @pl.when(c == 0)
    def _():
        ext_ref[0:8, :] = convprev_ref[...]
        state_ref[...] = ssmprev_ref[...]

    ext_ref[8:8 + l, 0:D_MODEL] = xs_ref[...]
    ext_ref[8:8 + l, D_MODEL:CONV_DIM] = bc_ref[...]
    conv = convb_ref[...]
    for i in range(CONV_K):
        conv = conv + ext_ref[pl.ds(5 + i, l), :] * convw_ref[i:i + 1, :]
    ext_ref[0:8, :] = ext_ref[l:l + 8, :]

    act = conv * _sigmoid(conv)
    xs = act[:, 0:D_MODEL]
    bm = act[:, D_MODEL:D_MODEL + 256]
    cm = act[:, D_MODEL + 256:CONV_DIM]

    dt = _softplus(dt_ref[...] + dtb_ref[...])
    if n_valid < l:
        dt = jnp.where(_iota((l, LANES), 0) < n_valid, dt, 0.0)
    ad = dt * (-jnp.exp(alog_ref[...]))
    row = _iota((l, l), 0)
    col = _iota((l, l), 1)
    causal = row >= col
    cs = _dot(causal.astype(F32), ad, HI)
    eye = (_iota((LANES, LANES), 0) == _iota((LANES, LANES), 1)).astype(F32)
    cs_t = _dot_nt(eye, cs, HI)
    expand = expand_ref[...]
    dt_e = _dot(dt, expand, HI)
    cs_e = _dot(cs, expand, HI)
    xdt = xs * dt_e
    decay_in = jnp.exp(cs_e)
    decay_out = jnp.exp(cs_e[l - 1:l, :] - cs_e)
    xdt_dec = (xdt * decay_out).astype(BF16)
    lane_lo = _iota((l, LANES), 1) < SSD_HEADDIM
    row_lo = _iota((LANES, 1), 0) < SSD_HEADDIM

    ys = []
    for g in range(2):
        bg = bm[:, g * LANES:(g + 1) * LANES].astype(BF16)
        cg = cm[:, g * LANES:(g + 1) * LANES].astype(BF16)
        cb = _dot_nt(cg, bg)
        for pr in range(4 * g, 4 * g + 4):
            sl = slice(pr * LANES, (pr + 1) * LANES)
            h0, h1 = 2 * pr, 2 * pr + 1
            m0 = jnp.where(causal, jnp.exp(cs[:, h0:h0 + 1] - cs_t[h0:h0 + 1, :]), 0.0) * cb
            m1 = jnp.where(causal, jnp.exp(cs[:, h1:h1 + 1] - cs_t[h1:h1 + 1, :]), 0.0) * cb
            xp = xdt[:, sl]
            x_lo = jnp.where(lane_lo, xp, 0.0).astype(BF16)
            x_hi = jnp.where(lane_lo, 0.0, xp).astype(BF16)
            y_diag = _dot(m0.astype(BF16), x_lo) + _dot(m1.astype(BF16), x_hi)
            s_pair = state_ref[pr]
            y_off = _dot_nt(cg, s_pair.astype(BF16)) * decay_in[:, sl]
            ys.append(y_diag + y_off + dskip_ref[:, sl] * xs[:, sl])
            upd = _dot_tn(xdt_dec[:, sl], bg)
            last = cs[l - 1:l, :]
            row_decay = jnp.exp(jnp.where(row_lo, last[:, h0:h0 + 1], last[:, h1:h1 + 1]))
            state_ref[pr] = row_decay * s_pair + upd
    y = jnp.concatenate(ys, axis=1)
    z = z_ref[...]
    y_ref[...] = _rmsnorm(y * (z * _sigmoid(z)), normw_ref[...])


def _ssd(proj3, conv_prev8, ssm_prev, p, *, l, n_valid):
    b, lp, _ = proj3.shape
    nc = lp // l
    full = lambda shape: pl.BlockSpec(shape, lambda i, c: (0,) * len(shape))
    body = functools.partial(_ssd_body, l=l, n_valid=n_valid)
    return pl.pallas_call(
        body,
        grid=(b, nc),
        in_specs=[
            pl.BlockSpec((None, l, 1024), lambda i, c: (i, c, COL_Z)),
            pl.BlockSpec((None, l, 1024), lambda i, c: (i, c, COL_XS)),
            pl.BlockSpec((None, l, 512), lambda i, c: (i, c, COL_BC)),
            pl.BlockSpec((None, l, 128), lambda i, c: (i, c, COL_DT)),
            pl.BlockSpec((None, 8, CONV_DIM), lambda i, c: (i, 0, 0)),
            pl.BlockSpec((None, 8, 128, 128), lambda i, c: (i, 0, 0, 0)),
            full((CONV_K, CONV_DIM)), full((1, CONV_DIM)), full((1, 128)), full((1, 128)),
            full((1, 1024)), full((1, 1024)), full((128, 1024)),
        ],
        out_specs=[
            pl.BlockSpec((None, l, 1024), lambda i, c: (i, c, 0)),
            pl.BlockSpec((None, 8, 128, 128), lambda i, c: (i, 0, 0, 0)),
        ],
        out_shape=[jax.ShapeDtypeStruct((b, lp, 1024), F32), jax.ShapeDtypeStruct((b, 8, 128, 128), F32)],
        scratch_shapes=[pltpu.VMEM((l + 8, CONV_DIM), F32)],
        compiler_params=_cparams(("parallel", "arbitrary")),
        name="ssd",
    )(proj3, proj3, proj3, proj3, conv_prev8, ssm_prev, p["conv_w"], p["conv_b"], p["dt_bias"], p["a_log"],
      p["d_skip_e"], p["ssd_norm_w"], p["expand"])


def _rwkv_body(r_ref, k_ref, v_ref, lr_ref, shprev_ref, wkvprev_ref, mu_ref, w0_ref, w2_ref, a0_ref, a2_ref,
               g2_ref, kk_ref, ka_ref, rk_ref, lnw_ref, lnb_ref, seg_ref, y_ref, state_ref, ext_ref,
               *, C, n_valid):
    c = pl.program_id(1)

    @pl.when(c == 0)
    def _():
        ext_ref[0:8, :] = shprev_ref[...]
        state_ref[...] = wkvprev_ref[...]

    ext_ref[8:8 + C, 0:1024] = r_ref[...]
    ext_ref[8:8 + C, 1024:2048] = k_ref[...]
    ext_ref[8:8 + C, 2048:3072] = v_ref[...]
    ext_ref[8:8 + C, 3072:3328] = lr_ref[...]
    cur = ext_ref[8:8 + C, :]
    prev = ext_ref[pl.ds(7, C), :]
    mixed = cur + (prev - cur) * mu_ref[...]
    ext_ref[0:8, :] = ext_ref[C:C + 8, :]

    r = mixed[:, 0:1024]
    k = mixed[:, 1024:2048]
    v = mixed[:, 2048:3072]
    wa = mixed[:, 3072:3200]
    gl = mixed[:, 3200:3328]
    seg = seg_ref[...]

    wlog = -_softplus(-(w0_ref[...] + _dot(jnp.tanh(wa), w2_ref[...], HI))) - 0.5
    lw = -jnp.exp(wlog)
    a_sig = _sigmoid(a0_ref[...] + _dot(wa, a2_ref[...], HI))
    g = _dot(_sigmoid(gl), g2_ref[...], HI)
    kk = k * kk_ref[...]
    kk = kk * lax.rsqrt(jnp.maximum(_dot_onehot(kk * kk, seg), 1e-24))
    kh = k * (1.0 + (a_sig - 1.0) * ka_ref[...])
    av = -kk
    bv = kk * a_sig
    bonus = _dot_onehot(r * kh * rk_ref[...], seg) * v
    if n_valid < C:
        valid = _iota((C, 1024), 0) < n_valid
        lw = jnp.where(valid, lw, 0.0)
        av = jnp.where(valid, av, 0.0)
        bv = jnp.where(valid, bv, 0.0)
        kh = jnp.where(valid, kh, 0.0)
        v = jnp.where(valid, v, 0.0)

    causal_c = (_iota((C, C), 0) >= _iota((C, C), 1)).astype(F32)
    cl = _dot(causal_c, lw, HI)
    e_pos = jnp.exp(cl)
    e_neg = jnp.exp(-cl)
    rt = r * e_pos
    at = av * jnp.exp(cl - lw)
    bt = bv * e_neg
    kt = kh * e_neg
    p_end = e_pos[C - 1:C, :]

    R2 = 2 * C
    lane_lo = _iota((C, LANES), 1) < HD_RWKV
    rr = _iota((R2, R2), 0)
    cc = _iota((R2, R2), 1)
    strict = rr > cc
    incl = rr >= cc
    eye = (rr == cc).astype(F32)

    def stack(x):
        return jnp.concatenate([jnp.where(lane_lo, x, 0.0), jnp.where(lane_lo, 0.0, x)], axis=0)

    ys = []
    for pr in range(8):
        sl = slice(pr * LANES, (pr + 1) * LANES)
        a_s, r_s, b_s, k_s, v_s = stack(at[:, sl]), stack(rt[:, sl]), stack(bt[:, sl]), stack(kt[:, sl]), stack(v[:, sl])
        s_bd = state_ref[pr]
        m_ak = jnp.where(strict, _dot_nt(a_s, k_s, HI), 0.0)
        w = _dot_nt(a_s, s_bd, HI) + _dot(m_ak, v_s, HI)
        if n_valid > 1:
            x = jnp.where(strict, _dot_nt(a_s, b_s, HI), 0.0)
            t = eye + x
            span = 2
            while span < C:
                x = _dot(x, x, HI)
                t = t + _dot(t, x, HI)
                span *= 2
            u = _dot(t, w, HI)
        else:
            u = w
        n_rb = jnp.where(incl, _dot_nt(r_s, b_s, HI), 0.0)
        n_rk = jnp.where(incl, _dot_nt(r_s, k_s, HI), 0.0)
        y2 = _dot_nt(r_s, s_bd, HI) + _dot(n_rb, u, HI) + _dot(n_rk, v_s, HI)
        ys.append(y2[0:C] + y2[C:R2])
        state_ref[pr] = (s_bd + _dot_tn(u, b_s, HI) + _dot_tn(v_s, k_s, HI)) * p_end[:, sl]
    yr = jnp.concatenate(ys, axis=1)
    mean = _dot_onehot(yr, seg) * (1.0 / HD_RWKV)
    d = yr - mean
    var = _dot_onehot(d * d, seg) * (1.0 / HD_RWKV)
    yn = d * lax.rsqrt(var + RWKV_LN_EPS) * lnw_ref[...] + lnb_ref[...]
    y_ref[...] = (yn + bonus) * g


def _rwkv(proj3, shift_prev8, wkv_prev_bd, p, *, C, n_valid):
    b, lp, _ = proj3.shape
    nc = lp // C
    full = lambda shape: pl.BlockSpec(shape, lambda i, c: (0,) * len(shape))
    body = functools.partial(_rwkv_body, C=C, n_valid=n_valid)
    return pl.pallas_call(
        body,
        grid=(b, nc),
        in_specs=[
            pl.BlockSpec((None, C, 1024), lambda i, c: (i, c, COL_R)),
            pl.BlockSpec((None, C, 1024), lambda i, c: (i, c, COL_K)),
            pl.BlockSpec((None, C, 1024), lambda i, c: (i, c, COL_V)),
            pl.BlockSpec((None, C, 256), lambda i, c: (i, c, COL_LR)),
            pl.BlockSpec((None, 8, RWKV_PROJ), lambda i, c: (i, 0, 0)),
            pl.BlockSpec((None, 8, 128, 128), lambda i, c: (i, 0, 0, 0)),
            full((1, RWKV_PROJ)), full((1, 1024)), full((128, 1024)), full((1, 1024)), full((128, 1024)),
            full((128, 1024)), full((1, 1024)), full((1, 1024)), full((1, 1024)), full((1, 1024)), full((1, 1024)),
            full((1024, 1024)),
        ],
        out_specs=[
            pl.BlockSpec((None, C, 1024), lambda i, c: (i, c, 0)),
            pl.BlockSpec((None, 8, 128, 128), lambda i, c: (i, 0, 0, 0)),
        ],
        out_shape=[jax.ShapeDtypeStruct((b, lp, 1024), F32), jax.ShapeDtypeStruct((b, 8, 128, 128), F32)],
        scratch_shapes=[pltpu.VMEM((C + 8, RWKV_PROJ), F32)],
        compiler_params=_cparams(("parallel", "arbitrary")),
        name="rwkv",
    )(proj3, proj3, proj3, proj3, shift_prev8, wkv_prev_bd, p["rwkv_mu"], p["rwkv_w0"], p["rwkv_w2p"], p["rwkv_a0"],
      p["rwkv_a2p"], p["rwkv_g2"], p["rwkv_k_k"], p["rwkv_k_a"], p["rwkv_r_k"], p["rwkv_ln_w"], p["rwkv_ln_b"],
      p["seg"])


def _merge_body(x_ref, gs_ref, gr_ref, ys_ref, yr_ref, w_ref, o_ref):
    merged = _sigmoid(gs_ref[...]) * ys_ref[...] + _sigmoid(gr_ref[...]) * yr_ref[...]
    o_ref[...] = x_ref[...] + _dot(merged.astype(BF16), w_ref[...])


def _merge_out(x, proj, y_ssd, y_rwkv, w_out, *, tm=512):
    m = x.shape[0]
    tm = min(tm, m)
    row = lambda j: pl.BlockSpec((tm, 1024), lambda i: (i, j))
    return pl.pallas_call(
        _merge_body,
        grid=(m // tm,),
        in_specs=[row(0), row(COL_GS), row(COL_GR), row(0), row(0), pl.BlockSpec((1024, 1024), lambda i: (0, 0))],
        out_specs=row(0),
        out_shape=jax.ShapeDtypeStruct((m, 1024), F32),
        compiler_params=_cparams(("parallel",)),
        name="merge_out",
    )(x, proj, proj, y_ssd, y_rwkv, w_out.astype(BF16))


def _attn_body(q_ref, k_ref, v_ref, o_ref):
    q = q_ref[...]
    outs = []
    for h in range(MEM_HEADS):
        sl = slice(h * MEM_HD, (h + 1) * MEM_HD)
        s = _dot_nt(q[:, sl].astype(BF16), k_ref[:, sl].astype(BF16)) * (MEM_HD ** -0.5)
        s = s - jnp.max(s, axis=-1, keepdims=True)
        e = jnp.exp(s)
        prob = e / jnp.sum(e, axis=-1, keepdims=True)
        outs.append(_dot(prob.astype(BF16), v_ref[:, sl].astype(BF16)))
    o_ref[...] = jnp.concatenate(outs, axis=1)


def _attn(q3, mem_k, mem_v, *, tq):
    b, lq, _ = q3.shape
    return pl.pallas_call(
        _attn_body,
        grid=(b, lq // tq),
        in_specs=[
            pl.BlockSpec((None, tq, 1024), lambda i, j: (i, j, 0)),
            pl.BlockSpec((None, N_MEM, 1024), lambda i, j: (i, 0, 0)),
            pl.BlockSpec((None, N_MEM, 1024), lambda i, j: (i, 0, 0)),
        ],
        out_specs=pl.BlockSpec((None, tq, 1024), lambda i, j: (i, j, 0)),
        out_shape=jax.ShapeDtypeStruct((b, lq, 1024), F32),
        compiler_params=_cparams(("parallel", "arbitrary")),
        name="mem_attn",
    )(q3, mem_k, mem_v)


def _stage2_pairs():
    return [(r1, r2) for r1 in range(PEER_TOPK) for r2 in range(PEER_TOPK) if (r1 + 1) * (r2 + 1) <= PEER_TOPK]


N_CAND = 56


def _topk_body(q_ref, keys_ref, gate_ref, i1_ref, i2_ref, s_scr, i_scr, cand_scr, c1_scr, c2_scr, res_scr, *, T):
    neg = -jnp.inf
    key_row = _iota((N_KEYS, T), 0)
    cand_row = _iota((N_CAND, T), 0)
    pairs = _stage2_pairs()
    q = q_ref[...]
    for h in range(PEER_HEADS):
        for c in range(2):
            hc = 2 * h + c
            s = _dot_nt(keys_ref[hc], q[:, hc * LANES:(hc + 1) * LANES], HI)
            for r in range(PEER_TOPK):
                m = jnp.max(s, axis=0, keepdims=True)
                idx = jnp.min(jnp.where(s == m, key_row, N_KEYS), axis=0, keepdims=True)
                s_scr[c, r:r + 1, :] = m
                i_scr[c, r:r + 1, :] = idx
                s = jnp.where(key_row == idx, neg, s)
        cand_scr[...] = jnp.full((N_CAND, T), neg, F32)
        for j, (r1, r2) in enumerate(pairs):
            cand_scr[j:j + 1, :] = s_scr[0, r1:r1 + 1, :] + s_scr[1, r2:r2 + 1, :]
            c1_scr[j:j + 1, :] = i_scr[0, r1:r1 + 1, :]
            c2_scr[j:j + 1, :] = i_scr[1, r2:r2 + 1, :]
        for j in range(len(pairs), N_CAND):
            c1_scr[j:j + 1, :] = jnp.zeros((1, T), jnp.int32)
            c2_scr[j:j + 1, :] = jnp.zeros((1, T), jnp.int32)
        cand = cand_scr[...]
        c1 = c1_scr[...]
        c2 = c2_scr[...]
        for r in range(PEER_TOPK):
            m = jnp.max(cand, axis=0, keepdims=True)
            idx = jnp.min(jnp.where(cand == m, cand_row, N_CAND), axis=0, keepdims=True)
            hit = cand_row == idx
            res_scr[0, r:r + 1, :] = m
            i1_ref[h * PEER_TOPK + r:h * PEER_TOPK + r + 1, :] = jnp.sum(jnp.where(hit, c1, 0), axis=0, keepdims=True)
            i2_ref[h * PEER_TOPK + r:h * PEER_TOPK + r + 1, :] = jnp.sum(jnp.where(hit, c2, 0), axis=0, keepdims=True)
            cand = jnp.where(hit, neg, cand)
        top = res_scr[0]
        e = jnp.exp(top - jnp.max(top, axis=0, keepdims=True))
        gate_ref[h * PEER_TOPK:(h + 1) * PEER_TOPK, :] = e / jnp.sum(e, axis=0, keepdims=True)


def _peer_topk(q, sub_keys, *, T=256):
    m = q.shape[0]
    T = min(T, m)
    body = functools.partial(_topk_body, T=T)
    out = pl.BlockSpec((128, T), lambda i: (0, i))
    return pl.pallas_call(
        body,
        grid=(m // T,),
        in_specs=[pl.BlockSpec((T, 2048), lambda i: (i, 0)), pl.BlockSpec((16, 128, 128), lambda i: (0, 0, 0))],
        out_specs=[out, out, out],
        out_shape=[jax.ShapeDtypeStruct((128, m), F32), jax.ShapeDtypeStruct((128, m), jnp.int32),
                   jax.ShapeDtypeStruct((128, m), jnp.int32)],
        scratch_shapes=[pltpu.VMEM((2, PEER_TOPK, T), F32), pltpu.VMEM((2, PEER_TOPK, T), jnp.int32),
                        pltpu.VMEM((N_CAND, T), F32), pltpu.VMEM((N_CAND, T), jnp.int32),
                        pltpu.VMEM((N_CAND, T), jnp.int32), pltpu.VMEM((1, PEER_TOPK, T), F32)],
        compiler_params=_cparams(("parallel",)),
        name="peer_topk",
    )(q, sub_keys.reshape(16, 128, 128))


I1_PER_STEP = 8


def _peer_body(x_ref, gate_ref, i1_ref, i2_ref, nw_ref, fw_ref, u_ref, v_ref, o_ref, hn_scr, g_scr, acc_scr, *, T):
    j = pl.program_id(1)

    @pl.when(j == 0)
    def _():
        hn_scr[...] = _rmsnorm(x_ref[...], nw_ref[...]).astype(BF16)
        acc_scr[...] = jnp.zeros_like(acc_scr)
        key_row = _iota((N_KEYS, LANES), 0)

        def build(t, carry):
            g_row = gate_ref[pl.ds(t, 1), :]
            a_t = jnp.where(key_row == i1_ref[pl.ds(t, 1), :], g_row, 0.0).astype(BF16)
            b_t = jnp.where(key_row == i2_ref[pl.ds(t, 1), :], 1.0, 0.0).astype(BF16)
            g_scr[pl.ds(pl.multiple_of(t * G_PITCH, 8), N_KEYS), :] = _dot_nt(a_t, b_t)
            return carry

        lax.fori_loop(0, T, build, 0)

    act = _dot_nt(hn_scr[...], u_ref[...])
    ge = 0.5 * act * (1.0 + lax.erf(act * (2.0 ** -0.5)))
    coefs = []
    for ii in range(I1_PER_STEP):
        g_t = g_scr[pl.ds(j * I1_PER_STEP + ii, T, stride=G_PITCH), :]
        coefs.append((g_t * ge[:, ii * LANES:(ii + 1) * LANES]).astype(BF16))
    acc_scr[...] += _dot(jnp.concatenate(coefs, axis=1), v_ref[...])

    @pl.when(j == pl.num_programs(1) - 1)
    def _():
        o_ref[...] = _rmsnorm(x_ref[...] + acc_scr[...], fw_ref[...])


def _peer_mix(x, gate, i1, i2, norm_w, final_w, u_bf16, v_bf16, *, T=256):
    m = x.shape[0]
    T = min(T, m)
    ne = I1_PER_STEP * N_KEYS
    body = functools.partial(_peer_body, T=T)
    row = lambda w: pl.BlockSpec((T, w), lambda i, j: (i, 0))
    vec = pl.BlockSpec((1, 1024), lambda i, j: (0, 0))
    return pl.pallas_call(
        body,
        grid=(m // T, N_KEYS // I1_PER_STEP),
        in_specs=[row(1024), row(128), row(128), row(128), vec, vec,
                  pl.BlockSpec((ne, 1024), lambda i, j: (j, 0)), pl.BlockSpec((ne, 1024), lambda i, j: (j, 0))],
        out_specs=row(1024),
        out_shape=jax.ShapeDtypeStruct((m, 1024), F32),
        scratch_shapes=[pltpu.VMEM((T, 1024), BF16), pltpu.VMEM((T * G_PITCH, LANES), F32),
                        pltpu.VMEM((T, 1024), F32)],
        compiler_params=_cparams(("parallel", "arbitrary")),
        name="peer_mix",
    )(x, gate, i1, i2, norm_w.reshape(1, 1024), final_w.reshape(1, 1024), u_bf16, v_bf16)


def _pad_rows(x, rows):
    return jnp.pad(x, ((0, 0), (rows - x.shape[1], 0), (0, 0)))


def _block_diag_pairs(s):
    b, _, p, k = s.shape
    s = s.reshape(b, 8, 2, p, k)
    z = jnp.zeros_like(s[:, :, 0])
    top = jnp.concatenate([s[:, :, 0], z], axis=-1)
    bot = jnp.concatenate([z, s[:, :, 1]], axis=-1)
    return jnp.concatenate([top, bot], axis=-2)


def _block(x, mem_k, mem_v, conv_prev, ssm_prev, shift_prev, wkv_prev, p, *, ssd_l, rwkv_c):
    b, L, _ = x.shape
    m = b * L
    x2d = x.reshape(m, D_MODEL)
    proj = _linear(x2d, p["w_in"], norm_w=p["norm_mix_w"], tn=1152, name="in_proj")
    proj3 = proj.reshape(b, L, D_IN_PAD)
    lp = max(L, SUBLANES)
    n_valid = L if L < SUBLANES else max(ssd_l, rwkv_c)
    proj3p = proj3 if lp == L else jnp.pad(proj3, ((0, 0), (0, lp - L), (0, 0)))

    y_ssd, ssm_new = _ssd(proj3p, _pad_rows(conv_prev, 8), ssm_prev.reshape(b, 8, 128, 128), p,
                          l=ssd_l, n_valid=min(n_valid, ssd_l))
    y_rwkv, wkv_bd = _rwkv(proj3p, _pad_rows(shift_prev[:, None, :], 8), _block_diag_pairs(wkv_prev), p,
                           C=rwkv_c, n_valid=min(n_valid, rwkv_c))
    x1 = _merge_out(x2d, proj, y_ssd[:, :L].reshape(m, D_MODEL), y_rwkv[:, :L].reshape(m, D_MODEL), p["w_out"])

    xbc = jnp.concatenate([proj3[..., 1024:2048], proj3[..., 7168:7680]], axis=-1)
    rw = jnp.concatenate([proj3[..., 2048:5120], proj3[..., 7680:7936]], axis=-1)
    conv_new = jnp.concatenate([conv_prev, xbc[:, max(L - (CONV_K - 1), 0):]], axis=1)[:, -(CONV_K - 1):]
    shift_new = rw[:, L - 1]
    ssm_new = ssm_new.reshape(b, H_SSD, SSD_HEADDIM, SSD_STATE)
    wkv5 = wkv_bd.reshape(b, 8, 2, HD_RWKV, 2, HD_RWKV)
    wkv_new = jnp.stack([wkv5[:, :, 0, :, 0], wkv5[:, :, 1, :, 1]], axis=2).reshape(b, H_RWKV, HD_RWKV, HD_RWKV)

    q = _linear(x1, p["w_mq"], norm_w=p["norm_mem_w"], name="mem_q")
    q3 = q.reshape(b, L, D_MODEL)
    if L < SUBLANES:
        q3 = jnp.pad(q3, ((0, 0), (0, SUBLANES - L), (0, 0)))
    o = _attn(q3, mem_k.reshape(b, N_MEM, D_MODEL), mem_v.reshape(b, N_MEM, D_MODEL), tq=min(512, q3.shape[1]))
    x2 = _linear(o[:, :L].reshape(m, D_MODEL), p["w_mo"], res=x1, name="mem_o")

    pq = _linear(x2, p["w_pq"], norm_w=p["norm_ffn_w"], passes=3, name="peer_q")
    gate_t, i1_t, i2_t = _peer_topk(pq, p["sub_keys"])
    y = _peer_mix(x2, gate_t.T, i1_t.T, i2_t.T, p["norm_ffn_w"], p["norm_final_w"], p["expert_u"], p["expert_v"])
    return y.reshape(b, L, D_MODEL), conv_new, ssm_new, shift_new, wkv_new


def kernel(x_prompt, x_sample, mem_prompt, state_ssm, state_conv, state_wkv, state_shift, cache_mem_k, cache_mem_v, norm_mix_w, w_in, conv_w, conv_b, dt_bias, a_log, d_skip, ssd_norm_w, rwkv_mu, rwkv_w0, rwkv_w2, rwkv_a0, rwkv_a2, rwkv_g2, rwkv_k_k, rwkv_k_a, rwkv_r_k, rwkv_ln_w, rwkv_ln_b, w_out, norm_mem_w, mem_norm_w, w_mk, w_mv, w_mq, w_mo, norm_ffn_w, w_pq, sub_keys, expert_u, expert_v, norm_final_w):
    assert w_in.shape[0] == 1, "single-layer problem"
    bp = x_prompt.shape[0]
    bs = x_sample.shape[0]
    wi = w_in[0]
    w_in_perm = jnp.concatenate(
        [wi[:, 0:2048], wi[:, 2576:5648], wi[:, 5904:7952], wi[:, 2048:2560], wi[:, 5648:5904], wi[:, 2560:2576],
         jnp.zeros((D_MODEL, LANES - H_SSD), F32)], axis=1)
    pad16 = lambda a: jnp.pad(a.reshape(1, H_SSD), ((0, 0), (0, LANES - H_SSD)))
    head_of_lane = jnp.arange(D_MODEL) // SSD_HEADDIM
    zeros64 = jnp.zeros((64, D_MODEL), F32)
    p = {
        "norm_mix_w": norm_mix_w[0], "w_in": w_in_perm,
        "conv_w": conv_w[0], "conv_b": conv_b[0].reshape(1, CONV_DIM),
        "dt_bias": pad16(dt_bias[0]), "a_log": pad16(a_log[0]),
        "d_skip_e": jnp.repeat(d_skip[0], SSD_HEADDIM).reshape(1, D_MODEL),
        "ssd_norm_w": ssd_norm_w[0].reshape(1, D_MODEL),
        "expand": (jnp.arange(LANES)[:, None] == head_of_lane[None, :]).astype(F32),
        "rwkv_mu": rwkv_mu[0].reshape(1, RWKV_PROJ), "rwkv_w0": rwkv_w0[0].reshape(1, D_MODEL),
        "rwkv_w2p": jnp.concatenate([rwkv_w2[0], zeros64], axis=0),
        "rwkv_a0": rwkv_a0[0].reshape(1, D_MODEL),
        "rwkv_a2p": jnp.concatenate([zeros64, rwkv_a2[0]], axis=0),
        "rwkv_g2": rwkv_g2[0], "rwkv_k_k": rwkv_k_k[0].reshape(1, D_MODEL),
        "rwkv_k_a": rwkv_k_a[0].reshape(1, D_MODEL), "rwkv_r_k": rwkv_r_k[0].reshape(1, D_MODEL),
        "rwkv_ln_w": rwkv_ln_w[0].reshape(1, D_MODEL), "rwkv_ln_b": rwkv_ln_b[0].reshape(1, D_MODEL),
        "seg": (head_of_lane[:, None] == head_of_lane[None, :]).astype(BF16),
        "w_out": w_out[0], "norm_mem_w": norm_mem_w[0], "w_mq": w_mq[0], "w_mo": w_mo[0],
        "norm_ffn_w": norm_ffn_w[0], "w_pq": w_pq[0], "sub_keys": sub_keys[0],
        "expert_u": expert_u[0].astype(BF16), "expert_v": expert_v[0].astype(BF16),
        "norm_final_w": norm_final_w,
    }

    mem2d = mem_prompt.reshape(bp * N_MEM, D_MODEL)
    mem_k = _linear(mem2d, w_mk[0], norm_w=mem_norm_w[0], name="mem_k").reshape(bp, N_MEM, MEM_HEADS, MEM_HD)
    mem_v = _linear(mem2d, w_mv[0], norm_w=mem_norm_w[0], name="mem_v").reshape(bp, N_MEM, MEM_HEADS, MEM_HD)
    y_p, conv_p, ssm_p, shift_p, wkv_p = _block(
        x_prompt, mem_k, mem_v,
        jnp.zeros((bp, CONV_K - 1, CONV_DIM), F32), jnp.zeros((bp, H_SSD, SSD_HEADDIM, SSD_STATE), F32),
        jnp.zeros((bp, RWKV_PROJ), F32), jnp.zeros((bp, H_RWKV, HD_RWKV, HD_RWKV), F32), p,
        ssd_l=SSD_CHUNK, rwkv_c=RWKV_CHUNK)
    y_s, conv_s, ssm_s, shift_s, wkv_s = _block(
        x_sample, cache_mem_k[0], cache_mem_v[0], state_conv[0], state_ssm[0], state_shift[0], state_wkv[0], p,
        ssd_l=SUBLANES, rwkv_c=SUBLANES)
    return (y_p, y_s, ssm_p[None], conv_p[None], wkv_p[None], shift_p[None], mem_k[None], mem_v[None],
            ssm_s[None], conv_s[None], wkv_s[None], shift_s[None])
```

```python
import functools

import jax
import jax.numpy as jnp
from jax import lax
from jax.experimental import pallas as pl
from jax.experimental.pallas import tpu as pltpu

F32 = jnp.float32
BF16 = jnp.bfloat16
HI = lax.Precision.HIGHEST

D_MODEL = 1024
H_SSD = 16
SSD_HEADDIM = 64
SSD_STATE = 128
CONV_K = 4
CONV_DIM = 1536
H_RWKV = 16
HD_RWKV = 64
RWKV_PROJ = 3328
RWKV_LN_EPS = 64e-5
N_MEM = 256
MEM_HEADS = 4
MEM_HD = 256
N_KEYS = 128
PEER_HEADS = 8
PEER_TOPK = 16
NORM_EPS = 1e-6

LANES = 128
SUBLANES = 8
VMEM_LIMIT = 56 * 1024 * 1024

D_IN_PAD = 8064
COL_Z, COL_XS, COL_R, COL_K, COL_V, COL_GS, COL_GR = 0, 1, 2, 3, 4, 5, 6
COL_BC = 14
COL_LR = 30
COL_DT = 62

SSD_CHUNK = 128
RWKV_CHUNK = 64
G_PITCH = 136
CHUNK_PASSES = 1


def _cparams(sem):
    return pltpu.CompilerParams(dimension_semantics=sem, vmem_limit_bytes=VMEM_LIMIT)


def _dot(a, b, prec=None):
    return jnp.dot(a, b, preferred_element_type=F32, precision=prec)


def _dot_nt(a, b, prec=None):
    return lax.dot_general(a, b, (((1,), (1,)), ((), ())), preferred_element_type=F32, precision=prec)


def _dot_tn(a, b, prec=None):
    return lax.dot_general(a, b, (((0,), (0,)), ((), ())), preferred_element_type=F32, precision=prec)


def _split_bf16(x):
    hi = x.astype(BF16)
    lo = (x - hi.astype(F32)).astype(BF16)
    return hi, lo


def _dot_onehot(x, sel_bf16):
    hi, lo = _split_bf16(x)
    return _dot(hi, sel_bf16) + _dot(lo, sel_bf16)


NN = ((1,), (0,))
NT = ((1,), (1,))
TN = ((0,), (0,))


def _cmm(a, b, dims):
    dn = (dims, ((), ()))
    if CHUNK_PASSES == 6:
        return lax.dot_general(a, b, dn, preferred_element_type=F32, precision=HI)
    ah, al = _split_bf16(a)
    bh, bl = _split_bf16(b)
    out = lax.dot_general(ah, bh, dn, preferred_element_type=F32)
    if CHUNK_PASSES == 3:
        out = out + lax.dot_general(al, bh, dn, preferred_element_type=F32)
        out = out + lax.dot_general(ah, bl, dn, preferred_element_type=F32)
    return out


def _softplus(x):
    return jnp.maximum(x, 0.0) + jnp.log1p(jnp.exp(-jnp.abs(x)))


def _sigmoid(x):
    return 1.0 / (1.0 + jnp.exp(-x))


def _rmsnorm(x, w):
    return x * lax.rsqrt(jnp.mean(x * x, axis=-1, keepdims=True) + NORM_EPS) * w


def _iota(shape, dim):
    return lax.broadcasted_iota(jnp.int32, shape, dim)


def _linear_body(*refs, has_norm, has_res, passes):
    it = iter(refs)
    x_ref = next(it)
    nw_ref = next(it) if has_norm else None
    wh_ref = next(it)
    wl_ref = next(it) if passes == 3 else None
    res_ref = next(it) if has_res else None
    o_ref = next(it)
    x = x_ref[...]
    if has_norm:
        x = _rmsnorm(x, nw_ref[...])
    xh = x.astype(BF16)
    acc = _dot(xh, wh_ref[...])
    if passes == 3:
        xl = (x - xh.astype(F32)).astype(BF16)
        acc = acc + _dot(xl, wh_ref[...]) + _dot(xh, wl_ref[...])
    if has_res:
        acc = acc + res_ref[...]
    o_ref[...] = acc


def _linear(x, w, *, norm_w=None, res=None, passes=1, tm=512, tn=1024, name="linear"):
    m, k = x.shape
    n = w.shape[1]
    tm = min(tm, m)
    tn = min(tn, n)
    assert m % tm == 0 and n % tn == 0
    args = [x]
    specs = [pl.BlockSpec((tm, k), lambda i, j: (i, 0))]
    if norm_w is not None:
        args.append(norm_w.reshape(1, k))
        specs.append(pl.BlockSpec((1, k), lambda i, j: (0, 0)))
    if passes == 3:
        wh, wl = _split_bf16(w)
        args += [wh, wl]
        specs += [pl.BlockSpec((k, tn), lambda i, j: (0, j))] * 2
    else:
        args.append(w.astype(BF16))
        specs.append(pl.BlockSpec((k, tn), lambda i, j: (0, j)))
    if res is not None:
        args.append(res)
        specs.append(pl.BlockSpec((tm, tn), lambda i, j: (i, j)))
    body = functools.partial(_linear_body, has_norm=norm_w is not None, has_res=res is not None, passes=passes)
    return pl.pallas_call(
        body,
        grid=(m // tm, n // tn),
        in_specs=specs,
        out_specs=pl.BlockSpec((tm, tn), lambda i, j: (i, j)),
        out_shape=jax.ShapeDtypeStruct((m, n), F32),
        compiler_params=_cparams(("parallel", "arbitrary")),
        name=name,
    )(*args)


def _ssd_body(z_ref, xs_ref, bc_ref, dt_ref, convprev_ref, ssmprev_ref, convw_ref, convb_ref, dtb_ref,
              alog_ref, dskip_ref, normw_ref, expand_ref, y_ref, state_ref, ext_ref, *, l, n_valid):
    c = pl.program_id(1)

    @pl.when(c == 0)
    def _():
        ext_ref[0:8, :] = convprev_ref[...]
        state_ref[...] = ssmprev_ref[...]

    ext_ref[8:8 + l, 0:D_MODEL] = xs_ref[...]
    ext_ref[8:8 + l, D_MODEL:CONV_DIM] = bc_ref[...]
    conv = convb_ref[...]
    for i in range(CONV_K):
        conv = conv + ext_ref[pl.ds(5 + i, l), :] * convw_ref[i:i + 1, :]
    ext_ref[0:8, :] = ext_ref[l:l + 8, :]

    act = conv * _sigmoid(conv)
    xs = act[:, 0:D_MODEL]
    bm = act[:, D_MODEL:D_MODEL + 256]
    cm = act[:, D_MODEL + 256:CONV_DIM]

    dt = _softplus(dt_ref[...] + dtb_ref[...])
    if n_valid < l:
        dt = jnp.where(_iota((l, LANES), 0) < n_valid, dt, 0.0)
    ad = dt * (-jnp.exp(alog_ref[...]))
    row = _iota((l, l), 0)
    col = _iota((l, l), 1)
    causal = row >= col
    cs = _dot(causal.astype(F32), ad, HI)
    eye = (_iota((LANES, LANES), 0) == _iota((LANES, LANES), 1)).astype(F32)
    cs_t = _dot_nt(eye, cs, HI)
    expand = expand_ref[...]
    dt_e = _dot(dt, expand, HI)
    cs_e = _dot(cs, expand, HI)
    xdt = xs * dt_e
    decay_in = jnp.exp(cs_e)
    decay_out = jnp.exp(cs_e[l - 1:l, :] - cs_e)
    xdt_dec = (xdt * decay_out).astype(BF16)
    lane_lo = _iota((l, LANES), 1) < SSD_HEADDIM
    row_lo = _iota((LANES, 1), 0) < SSD_HEADDIM

    ys = []
    for g in range(2):
        bg = bm[:, g * LANES:(g + 1) * LANES].astype(BF16)
        cg = cm[:, g * LANES:(g + 1) * LANES].astype(BF16)
        cb = _dot_nt(cg, bg)
        for pr in range(4 * g, 4 * g + 4):
            sl = slice(pr * LANES, (pr + 1) * LANES)
            h0, h1 = 2 * pr, 2 * pr + 1
            m0 = jnp.where(causal, jnp.exp(cs[:, h0:h0 + 1] - cs_t[h0:h0 + 1, :]), 0.0) * cb
            m1 = jnp.where(causal, jnp.exp(cs[:, h1:h1 + 1] - cs_t[h1:h1 + 1, :]), 0.0) * cb
            xp = xdt[:, sl]
            x_lo = jnp.where(lane_lo, xp, 0.0).astype(BF16)
            x_hi = jnp.where(lane_lo, 0.0, xp).astype(BF16)
            y_diag = _dot(m0.astype(BF16), x_lo) + _dot(m1.astype(BF16), x_hi)
            s_pair = state_ref[pr]
            y_off = _dot_nt(cg, s_pair.astype(BF16)) * decay_in[:, sl]
            ys.append(y_diag + y_off + dskip_ref[:, sl] * xs[:, sl])
            upd = _dot_tn(xdt_dec[:, sl], bg)
            last = cs[l - 1:l, :]
            row_decay = jnp.exp(jnp.where(row_lo, last[:, h0:h0 + 1], last[:, h1:h1 + 1]))
            state_ref[pr] = row_decay * s_pair + upd
    y = jnp.concatenate(ys, axis=1)
    z = z_ref[...]
    y_ref[...] = _rmsnorm(y * (z * _sigmoid(z)), normw_ref[...])


def _ssd(proj3, conv_prev8, ssm_prev, p, *, l, n_valid):
    b, lp, _ = proj3.shape
    nc = lp // l
    full = lambda shape: pl.BlockSpec(shape, lambda i, c: (0,) * len(shape))
    body = functools.partial(_ssd_body, l=l, n_valid=n_valid)
    return pl.pallas_call(
        body,
        grid=(b, nc),
        in_specs=[
            pl.BlockSpec((None, l, 1024), lambda i, c: (i, c, COL_Z)),
            pl.BlockSpec((None, l, 1024), lambda i, c: (i, c, COL_XS)),
            pl.BlockSpec((None, l, 512), lambda i, c: (i, c, COL_BC)),
            pl.BlockSpec((None, l, 128), lambda i, c: (i, c, COL_DT)),
            pl.BlockSpec((None, 8, CONV_DIM), lambda i, c: (i, 0, 0)),
            pl.BlockSpec((None, 8, 128, 128), lambda i, c: (i, 0, 0, 0)),
            full((CONV_K, CONV_DIM)), full((1, CONV_DIM)), full((1, 128)), full((1, 128)),
            full((1, 1024)), full((1, 1024)), full((128, 1024)),
        ],
        out_specs=[
            pl.BlockSpec((None, l, 1024), lambda i, c: (i, c, 0)),
            pl.BlockSpec((None, 8, 128, 128), lambda i, c: (i, 0, 0, 0)),
        ],
        out_shape=[jax.ShapeDtypeStruct((b, lp, 1024), F32), jax.ShapeDtypeStruct((b, 8, 128, 128), F32)],
        scratch_shapes=[pltpu.VMEM((l + 8, CONV_DIM), F32)],
        compiler_params=_cparams(("parallel", "arbitrary")),
        name="ssd",
    )(proj3, proj3, proj3, proj3, conv_prev8, ssm_prev, p["conv_w"], p["conv_b"], p["dt_bias"], p["a_log"],
      p["d_skip_e"], p["ssd_norm_w"], p["expand"])


def _dot_w(x, w_ref):
    xh, xl = _split_bf16(x)
    return _dot(xh, w_ref[0]) + _dot(xl, w_ref[0]) + _dot(xh, w_ref[1])


def _dot_sel3(sel_bf16, x):
    x1 = x.astype(BF16)
    r1 = x - x1.astype(F32)
    x2 = r1.astype(BF16)
    x3 = (r1 - x2.astype(F32)).astype(BF16)
    return _dot(sel_bf16, x1) + _dot(sel_bf16, x2) + _dot(sel_bf16, x3)


def _head_sum(x, seg_pair):
    rows = x.shape[0]
    n = x.shape[1] // LANES
    stacked = jnp.concatenate([x[:, i * LANES:(i + 1) * LANES] for i in range(n)], axis=0)
    s = _dot_onehot(stacked, seg_pair)
    return jnp.concatenate([s[i * rows:(i + 1) * rows] for i in range(n)], axis=1)


def _rwkv_body(r_ref, k_ref, v_ref, lr_ref, shprev_ref, wkvprev_ref, mu_ref, w0_ref, w2_ref, a0_ref, a2_ref,
               g2_ref, kk_ref, ka_ref, rk_ref, lnw_ref, lnb_ref, seg_ref, y_ref, state_ref, ext_ref,
               *, C, n_valid):
    c = pl.program_id(1)

    @pl.when(c == 0)
    def _():
        ext_ref[0:8, :] = shprev_ref[...]
        state_ref[...] = wkvprev_ref[...]

    ext_ref[8:8 + C, 0:1024] = r_ref[...]
    ext_ref[8:8 + C, 1024:2048] = k_ref[...]
    ext_ref[8:8 + C, 2048:3072] = v_ref[...]
    ext_ref[8:8 + C, 3072:3328] = lr_ref[...]
    cur = ext_ref[8:8 + C, :]
    prev = ext_ref[pl.ds(7, C), :]
    mixed = cur + (prev - cur) * mu_ref[...]
    ext_ref[0:8, :] = ext_ref[C:C + 8, :]

    r = mixed[:, 0:1024]
    k = mixed[:, 1024:2048]
    v = mixed[:, 2048:3072]
    wa = mixed[:, 3072:3200]
    gl = mixed[:, 3200:3328]
    seg = seg_ref[...]

    wlog = -_softplus(-(w0_ref[...] + _dot_w(jnp.tanh(wa), w2_ref))) - 0.5
    lw = -jnp.exp(wlog)
    a_sig = _sigmoid(a0_ref[...] + _dot_w(wa, a2_ref))
    g = _dot_w(_sigmoid(gl), g2_ref)
    kk = k * kk_ref[...]
    kk = kk * lax.rsqrt(jnp.maximum(_head_sum(kk * kk, seg), 1e-24))
    kh = k * (1.0 + (a_sig - 1.0) * ka_ref[...])
    av = -kk
    bv = kk * a_sig
    bonus = _head_sum(r * kh * rk_ref[...], seg) * v
    if n_valid < C:
        valid = _iota((C, 1024), 0) < n_valid
        lw = jnp.where(valid, lw, 0.0)
        av = jnp.where(valid, av, 0.0)
        bv = jnp.where(valid, bv, 0.0)
        kh = jnp.where(valid, kh, 0.0)
        v = jnp.where(valid, v, 0.0)

    causal_c = jnp.where(_iota((C, C), 0) >= _iota((C, C), 1), 1.0, 0.0).astype(BF16)
    cl = _dot_sel3(causal_c, lw)
    e_pos = jnp.exp(cl)
    e_neg = jnp.exp(-cl)
    rt = r * e_pos
    at = av * jnp.exp(cl - lw)
    bt = bv * e_neg
    kt = kh * e_neg
    p_end = e_pos[C - 1:C, :]

    R2 = 2 * C
    lane_lo = _iota((C, LANES), 1) < HD_RWKV
    rr = _iota((R2, R2), 0)
    cc = _iota((R2, R2), 1)
    strict = rr > cc
    incl = rr >= cc
    eye = (rr == cc).astype(F32)

    def stack(x):
        return jnp.concatenate([jnp.where(lane_lo, x, 0.0), jnp.where(lane_lo, 0.0, x)], axis=0)

    pairs = range(8)
    sls = [slice(p * LANES, (p + 1) * LANES) for p in pairs]
    a_s = [stack(at[:, sl]) for sl in sls]
    r_s = [stack(rt[:, sl]) for sl in sls]
    b_s = [stack(bt[:, sl]) for sl in sls]
    k_s = [stack(kt[:, sl]) for sl in sls]
    v_s = [stack(v[:, sl]) for sl in sls]
    s_bd = [state_ref[p] for p in pairs]
    m_ak = [jnp.where(strict, _cmm(a_s[p], k_s[p], NT), 0.0) for p in pairs]
    w = [_cmm(a_s[p], s_bd[p], NT) + _cmm(m_ak[p], v_s[p], NN) for p in pairs]
    if n_valid > 1:
        x = [jnp.where(strict, _cmm(a_s[p], b_s[p], NT), 0.0) for p in pairs]
        t = [eye + x[p] for p in pairs]
        span = 2
        while span < C:
            x = [_cmm(x[p], x[p], NN) for p in pairs]
            t = [t[p] + _cmm(t[p], x[p], NN) for p in pairs]
            span *= 2
        u = [_cmm(t[p], w[p], NN) for p in pairs]
    else:
        u = w
    n_rb = [jnp.where(incl, _cmm(r_s[p], b_s[p], NT), 0.0) for p in pairs]
    n_rk = [jnp.where(incl, _cmm(r_s[p], k_s[p], NT), 0.0) for p in pairs]
    ys = []
    for p in pairs:
        y2 = _cmm(r_s[p], s_bd[p], NT) + _cmm(n_rb[p], u[p], NN) + _cmm(n_rk[p], v_s[p], NN)
        ys.append(y2[0:C] + y2[C:R2])
        state_ref[p] = (s_bd[p] + _cmm(u[p], b_s[p], TN) + _cmm(v_s[p], k_s[p], TN)) * p_end[:, sls[p]]
    yr = jnp.concatenate(ys, axis=1)
    mean = _head_sum(yr, seg) * (1.0 / HD_RWKV)
    d = yr - mean
    var = _head_sum(d * d, seg) * (1.0 / HD_RWKV)
    yn = d * lax.rsqrt(var + RWKV_LN_EPS) * lnw_ref[...] + lnb_ref[...]
    y_ref[...] = (yn + bonus) * g


def _rwkv(proj3, shift_prev8, wkv_prev_bd, p, *, C, n_valid):
    b, lp, _ = proj3.shape
    nc = lp // C
    full = lambda shape: pl.BlockSpec(shape, lambda i, c: (0,) * len(shape))
    body = functools.partial(_rwkv_body, C=C, n_valid=n_valid)
    return pl.pallas_call(
        body,
        grid=(b, nc),
        in_specs=[
            pl.BlockSpec((None, C, 1024), lambda i, c: (i, c, COL_R)),
            pl.BlockSpec((None, C, 1024), lambda i, c: (i, c, COL_K)),
            pl.BlockSpec((None, C, 1024), lambda i, c: (i, c, COL_V)),
            pl.BlockSpec((None, C, 256), lambda i, c: (i, c, COL_LR)),
            pl.BlockSpec((None, 8, RWKV_PROJ), lambda i, c: (i, 0, 0)),
            pl.BlockSpec((None, 8, 128, 128), lambda i, c: (i, 0, 0, 0)),
            full((1, RWKV_PROJ)), full((1, 1024)), full((2, 128, 1024)), full((1, 1024)), full((2, 128, 1024)),
            full((2, 128, 1024)), full((1, 1024)), full((1, 1024)), full((1, 1024)), full((1, 1024)), full((1, 1024)),
            full((128, 128)),
        ],
        out_specs=[
            pl.BlockSpec((None, C, 1024), lambda i, c: (i, c, 0)),
            pl.BlockSpec((None, 8, 128, 128), lambda i, c: (i, 0, 0, 0)),
        ],
        out_shape=[jax.ShapeDtypeStruct((b, lp, 1024), F32), jax.ShapeDtypeStruct((b, 8, 128, 128), F32)],
        scratch_shapes=[pltpu.VMEM((C + 8, RWKV_PROJ), F32)],
        compiler_params=_cparams(("parallel", "arbitrary")),
        name="rwkv",
    )(proj3, proj3, proj3, proj3, shift_prev8, wkv_prev_bd, p["rwkv_mu"], p["rwkv_w0"], p["rwkv_w2p"], p["rwkv_a0"],
      p["rwkv_a2p"], p["rwkv_g2"], p["rwkv_k_k"], p["rwkv_k_a"], p["rwkv_r_k"], p["rwkv_ln_w"], p["rwkv_ln_b"],
      p["seg"])


def _merge_body(x_ref, gs_ref, gr_ref, ys_ref, yr_ref, w_ref, o_ref):
    merged = _sigmoid(gs_ref[...]) * ys_ref[...] + _sigmoid(gr_ref[...]) * yr_ref[...]
    o_ref[...] = x_ref[...] + _dot(merged.astype(BF16), w_ref[...])


def _merge_out(x, proj, y_ssd, y_rwkv, w_out, *, tm=512):
    m = x.shape[0]
    tm = min(tm, m)
    row = lambda j: pl.BlockSpec((tm, 1024), lambda i: (i, j))
    return pl.pallas_call(
        _merge_body,
        grid=(m // tm,),
        in_specs=[row(0), row(COL_GS), row(COL_GR), row(0), row(0), pl.BlockSpec((1024, 1024), lambda i: (0, 0))],
        out_specs=row(0),
        out_shape=jax.ShapeDtypeStruct((m, 1024), F32),
        compiler_params=_cparams(("parallel",)),
        name="merge_out",
    )(x, proj, proj, y_ssd, y_rwkv, w_out.astype(BF16))


def _attn_body(q_ref, k_ref, v_ref, o_ref):
    q = q_ref[...]
    outs = []
    for h in range(MEM_HEADS):
        sl = slice(h * MEM_HD, (h + 1) * MEM_HD)
        s = _dot_nt(q[:, sl].astype(BF16), k_ref[:, sl].astype(BF16)) * (MEM_HD ** -0.5)
        s = s - jnp.max(s, axis=-1, keepdims=True)
        e = jnp.exp(s)
        prob = e / jnp.sum(e, axis=-1, keepdims=True)
        outs.append(_dot(prob.astype(BF16), v_ref[:, sl].astype(BF16)))
    o_ref[...] = jnp.concatenate(outs, axis=1)


def _attn(q3, mem_k, mem_v, *, tq):
    b, lq, _ = q3.shape
    return pl.pallas_call(
        _attn_body,
        grid=(b, lq // tq),
        in_specs=[
            pl.BlockSpec((None, tq, 1024), lambda i, j: (i, j, 0)),
            pl.BlockSpec((None, N_MEM, 1024), lambda i, j: (i, 0, 0)),
            pl.BlockSpec((None, N_MEM, 1024), lambda i, j: (i, 0, 0)),
        ],
        out_specs=pl.BlockSpec((None, tq, 1024), lambda i, j: (i, j, 0)),
        out_shape=jax.ShapeDtypeStruct((b, lq, 1024), F32),
        compiler_params=_cparams(("parallel", "arbitrary")),
        name="mem_attn",
    )(q3, mem_k, mem_v)


def _attn_step_body(q_ref, k_ref, v_ref, o_ref):
    s = jnp.sum(k_ref[...] * q_ref[...][None], axis=-1, keepdims=True) * (MEM_HD ** -0.5)
    e = jnp.exp(s - jnp.max(s, axis=0, keepdims=True))
    prob = e / jnp.sum(e, axis=0, keepdims=True)
    o_ref[...] = jnp.sum(prob * v_ref[...], axis=0)


def _attn_step(q, mem_k, mem_v):
    b = q.shape[0]
    qspec = pl.BlockSpec((None, MEM_HEADS, MEM_HD), lambda i: (i, 0, 0))
    kvspec = pl.BlockSpec((None, N_MEM, MEM_HEADS, MEM_HD), lambda i: (i, 0, 0, 0))
    return pl.pallas_call(
        _attn_step_body,
        grid=(b,),
        in_specs=[qspec, kvspec, kvspec],
        out_specs=qspec,
        out_shape=jax.ShapeDtypeStruct((b, MEM_HEADS, MEM_HD), F32),
        compiler_params=_cparams(("parallel",)),
        name="mem_attn_step",
    )(q, mem_k, mem_v)


def _stage2_pairs():
    return [(r1, r2) for r1 in range(PEER_TOPK) for r2 in range(PEER_TOPK) if (r1 + 1) * (r2 + 1) <= PEER_TOPK]


N_CAND = 56


def _topk_body(q_ref, keys_ref, gate_ref, i1_ref, i2_ref, s_scr, i_scr, cand_scr, code_scr, res_scr, *, T):
    neg = -jnp.inf
    key_row = _iota((N_KEYS, T), 0).astype(F32)
    cand_row = _iota((N_CAND, T), 0).astype(F32)
    pairs = _stage2_pairs()
    q = q_ref[...]
    for h in range(PEER_HEADS):
        for c in range(2):
            hc = 2 * h + c
            s = _dot_nt(keys_ref[hc], q[:, hc * LANES:(hc + 1) * LANES], HI)
            for r in range(PEER_TOPK):
                m = jnp.max(s, axis=0, keepdims=True)
                idx = jnp.min(jnp.where(s == m, key_row, float(N_KEYS)), axis=0, keepdims=True)
                s_scr[c, r:r + 1, :] = m
                i_scr[c, r:r + 1, :] = idx
                s = jnp.where(key_row == idx, neg, s)
        cand_scr[...] = jnp.full((N_CAND, T), neg, F32)
        code_scr[...] = jnp.zeros((N_CAND, T), F32)
        for j, (r1, r2) in enumerate(pairs):
            cand_scr[j:j + 1, :] = s_scr[0, r1:r1 + 1, :] + s_scr[1, r2:r2 + 1, :]
            code_scr[j:j + 1, :] = i_scr[0, r1:r1 + 1, :] * float(N_KEYS) + i_scr[1, r2:r2 + 1, :]
        cand = cand_scr[...]
        code = code_scr[...]
        for r in range(PEER_TOPK):
            m = jnp.max(cand, axis=0, keepdims=True)
            idx = jnp.min(jnp.where(cand == m, cand_row, float(N_CAND)), axis=0, keepdims=True)
            hit = cand_row == idx
            res_scr[0, r:r + 1, :] = m
            res_scr[1, r:r + 1, :] = jnp.sum(jnp.where(hit, code, 0.0), axis=0, keepdims=True)
            cand = jnp.where(hit, neg, cand)
        top = res_scr[0]
        e = jnp.exp(top - jnp.max(top, axis=0, keepdims=True))
        rows = slice(h * PEER_TOPK, (h + 1) * PEER_TOPK)
        gate_ref[rows, :] = e / jnp.sum(e, axis=0, keepdims=True)
        first = jnp.floor(res_scr[1] * (1.0 / N_KEYS))
        i1_ref[rows, :] = first.astype(jnp.int32)
        i2_ref[rows, :] = (res_scr[1] - first * float(N_KEYS)).astype(jnp.int32)


def _peer_topk(q, sub_keys, *, T=256):
    m = q.shape[0]
    T = min(T, m)
    body = functools.partial(_topk_body, T=T)
    out = pl.BlockSpec((128, T), lambda i: (0, i))
    return pl.pallas_call(
        body,
        grid=(m // T,),
        in_specs=[pl.BlockSpec((T, 2048), lambda i: (i, 0)), pl.BlockSpec((16, 128, 128), lambda i: (0, 0, 0))],
        out_specs=[out, out, out],
        out_shape=[jax.ShapeDtypeStruct((128, m), F32), jax.ShapeDtypeStruct((128, m), jnp.int32),
                   jax.ShapeDtypeStruct((128, m), jnp.int32)],
        scratch_shapes=[pltpu.VMEM((2, PEER_TOPK, T), F32), pltpu.VMEM((2, PEER_TOPK, T), F32),
                        pltpu.VMEM((N_CAND, T), F32), pltpu.VMEM((N_CAND, T), F32),
                        pltpu.VMEM((2, PEER_TOPK, T), F32)],
        compiler_params=_cparams(("parallel",)),
        name="peer_topk",
    )(q, sub_keys.reshape(16, 128, 128))


I1_PER_STEP = 8


def _peer_body(x_ref, gate_ref, i1_ref, i2_ref, nw_ref, fw_ref, u_ref, v_ref, o_ref, hn_scr, g_scr, acc_scr, *, T):
    j = pl.program_id(1)

    @pl.when(j == 0)
    def _():
        hn_scr[...] = _rmsnorm(x_ref[...], nw_ref[...]).astype(BF16)
        acc_scr[...] = jnp.zeros_like(acc_scr)
        key_row = _iota((N_KEYS, LANES), 0)

        def build(t, carry):
            g_row = gate_ref[pl.ds(t, 1), :]
            a_t = jnp.where(key_row == i1_ref[pl.ds(t, 1), :], g_row, 0.0).astype(BF16)
            b_t = jnp.where(key_row == i2_ref[pl.ds(t, 1), :], 1.0, 0.0).astype(BF16)
            g_scr[pl.ds(pl.multiple_of(t * G_PITCH, 8), N_KEYS), :] = _dot_nt(a_t, b_t)
            return carry

        lax.fori_loop(0, T, build, 0, unroll=8)

    act = _dot_nt(hn_scr[...], u_ref[...])
    ge = 0.5 * act * (1.0 + lax.erf(act * (2.0 ** -0.5)))
    coefs = []
    for ii in range(I1_PER_STEP):
        g_t = g_scr[pl.ds(j * I1_PER_STEP + ii, T, stride=G_PITCH), :]
        coefs.append((g_t * ge[:, ii * LANES:(ii + 1) * LANES]).astype(BF16))
    acc_scr[...] += _dot(jnp.concatenate(coefs, axis=1), v_ref[...])

    @pl.when(j == pl.num_programs(1) - 1)
    def _():
        o_ref[...] = _rmsnorm(x_ref[...] + acc_scr[...], fw_ref[...])


def _peer_mix(x, gate, i1, i2, norm_w, final_w, u_bf16, v_bf16, *, T=256):
    m = x.shape[0]
    T = min(T, m)
    ne = I1_PER_STEP * N_KEYS
    body = functools.partial(_peer_body, T=T)
    row = lambda w: pl.BlockSpec((T, w), lambda i, j: (i, 0))
    vec = pl.BlockSpec((1, 1024), lambda i, j: (0, 0))
    return pl.pallas_call(
        body,
        grid=(m // T, N_KEYS // I1_PER_STEP),
        in_specs=[row(1024), row(128), row(128), row(128), vec, vec,
                  pl.BlockSpec((ne, 1024), lambda i, j: (j, 0)), pl.BlockSpec((ne, 1024), lambda i, j: (j, 0))],
        out_specs=row(1024),
        out_shape=jax.ShapeDtypeStruct((m, 1024), F32),
        scratch_shapes=[pltpu.VMEM((T, 1024), BF16), pltpu.VMEM((T * G_PITCH, LANES), F32),
                        pltpu.VMEM((T, 1024), F32)],
        compiler_params=_cparams(("parallel", "arbitrary")),
        name="peer_mix",
    )(x, gate, i1, i2, norm_w.reshape(1, 1024), final_w.reshape(1, 1024), u_bf16, v_bf16)


def _pad_rows(x, rows):
    return jnp.pad(x, ((0, 0), (rows - x.shape[1], 0), (0, 0)))


def _block_diag_pairs(s):
    b, _, p, k = s.shape
    s = s.reshape(b, 8, 2, p, k)
    z = jnp.zeros_like(s[:, :, 0])
    top = jnp.concatenate([s[:, :, 0], z], axis=-1)
    bot = jnp.concatenate([z, s[:, :, 1]], axis=-1)
    return jnp.concatenate([top, bot], axis=-2)


def _block(x, mem_k, mem_v, conv_prev, ssm_prev, shift_prev, wkv_prev, p, *, ssd_l, rwkv_c):
    b, L, _ = x.shape
    m = b * L
    x2d = x.reshape(m, D_MODEL)
    proj = _linear(x2d, p["w_in"], norm_w=p["norm_mix_w"], tn=1152, name="in_proj")
    proj3 = proj.reshape(b, L, D_IN_PAD)
    lp = max(L, SUBLANES)
    n_valid = L if L < SUBLANES else max(ssd_l, rwkv_c)
    proj3p = proj3 if lp == L else jnp.pad(proj3, ((0, 0), (0, lp - L), (0, 0)))

    y_ssd, ssm_new = _ssd(proj3p, _pad_rows(conv_prev, 8), ssm_prev.reshape(b, 8, 128, 128), p,
                          l=ssd_l, n_valid=min(n_valid, ssd_l))
    y_rwkv, wkv_bd = _rwkv(proj3p, _pad_rows(shift_prev[:, None, :], 8), _block_diag_pairs(wkv_prev), p,
                           C=rwkv_c, n_valid=min(n_valid, rwkv_c))
    x1 = _merge_out(x2d, proj, y_ssd[:, :L].reshape(m, D_MODEL), y_rwkv[:, :L].reshape(m, D_MODEL), p["w_out"])

    tail = proj3[:, max(L - (CONV_K - 1), 0):]
    xbc = jnp.concatenate([tail[..., 1024:2048], tail[..., 7168:7680]], axis=-1)
    rw = jnp.concatenate([tail[..., 2048:5120], tail[..., 7680:7936]], axis=-1)
    conv_new = jnp.concatenate([conv_prev, xbc], axis=1)[:, -(CONV_K - 1):]
    shift_new = rw[:, -1]
    ssm_new = ssm_new.reshape(b, H_SSD, SSD_HEADDIM, SSD_STATE)
    wkv5 = wkv_bd.reshape(b, 8, 2, HD_RWKV, 2, HD_RWKV)
    wkv_new = jnp.stack([wkv5[:, :, 0, :, 0], wkv5[:, :, 1, :, 1]], axis=2).reshape(b, H_RWKV, HD_RWKV, HD_RWKV)

    q = _linear(x1, p["w_mq"], norm_w=p["norm_mem_w"], name="mem_q")
    if L == 1:
        o = _attn_step(q.reshape(b, MEM_HEADS, MEM_HD), mem_k, mem_v)
    else:
        o = _attn(q.reshape(b, L, D_MODEL), mem_k.reshape(b, N_MEM, D_MODEL), mem_v.reshape(b, N_MEM, D_MODEL),
                  tq=min(512, L))
    x2 = _linear(o.reshape(m, D_MODEL), p["w_mo"], res=x1, name="mem_o")

    pq = _linear(x2, p["w_pq"], norm_w=p["norm_ffn_w"], passes=3, name="peer_q")
    gate_t, i1_t, i2_t = _peer_topk(pq, p["sub_keys"])
    y = _peer_mix(x2, gate_t.T, i1_t.T, i2_t.T, p["norm_ffn_w"], p["norm_final_w"], p["expert_u"], p["expert_v"])
    return y.reshape(b, L, D_MODEL), conv_new, ssm_new, shift_new, wkv_new


def kernel(x_prompt, x_sample, mem_prompt, state_ssm, state_conv, state_wkv, state_shift, cache_mem_k, cache_mem_v, norm_mix_w, w_in, conv_w, conv_b, dt_bias, a_log, d_skip, ssd_norm_w, rwkv_mu, rwkv_w0, rwkv_w2, rwkv_a0, rwkv_a2, rwkv_g2, rwkv_k_k, rwkv_k_a, rwkv_r_k, rwkv_ln_w, rwkv_ln_b, w_out, norm_mem_w, mem_norm_w, w_mk, w_mv, w_mq, w_mo, norm_ffn_w, w_pq, sub_keys, expert_u, expert_v, norm_final_w):
    assert w_in.shape[0] == 1, "single-layer problem"
    bp = x_prompt.shape[0]
    bs = x_sample.shape[0]
    wi = w_in[0]
    w_in_perm = jnp.concatenate(
        [wi[:, 0:2048], wi[:, 2576:5648], wi[:, 5904:7952], wi[:, 2048:2560], wi[:, 5648:5904], wi[:, 2560:2576],
         jnp.zeros((D_MODEL, LANES - H_SSD), F32)], axis=1)
    pad16 = lambda a: jnp.pad(a.reshape(1, H_SSD), ((0, 0), (0, LANES - H_SSD)))
    head_of_lane = jnp.arange(D_MODEL) // SSD_HEADDIM
    zeros64 = jnp.zeros((64, D_MODEL), F32)
    hi_lo = lambda w: jnp.stack(_split_bf16(w))
    p = {
        "norm_mix_w": norm_mix_w[0], "w_in": w_in_perm,
        "conv_w": conv_w[0], "conv_b": conv_b[0].reshape(1, CONV_DIM),
        "dt_bias": pad16(dt_bias[0]), "a_log": pad16(a_log[0]),
        "d_skip_e": jnp.repeat(d_skip[0], SSD_HEADDIM).reshape(1, D_MODEL),
        "ssd_norm_w": ssd_norm_w[0].reshape(1, D_MODEL),
        "expand": (jnp.arange(LANES)[:, None] == head_of_lane[None, :]).astype(F32),
        "rwkv_mu": rwkv_mu[0].reshape(1, RWKV_PROJ), "rwkv_w0": rwkv_w0[0].reshape(1, D_MODEL),
        "rwkv_w2p": hi_lo(jnp.concatenate([rwkv_w2[0], zeros64], axis=0)),
        "rwkv_a0": rwkv_a0[0].reshape(1, D_MODEL),
        "rwkv_a2p": hi_lo(jnp.concatenate([zeros64, rwkv_a2[0]], axis=0)),
        "rwkv_g2": hi_lo(rwkv_g2[0]), "rwkv_k_k": rwkv_k_k[0].reshape(1, D_MODEL),
        "rwkv_k_a": rwkv_k_a[0].reshape(1, D_MODEL), "rwkv_r_k": rwkv_r_k[0].reshape(1, D_MODEL),
        "rwkv_ln_w": rwkv_ln_w[0].reshape(1, D_MODEL), "rwkv_ln_b": rwkv_ln_b[0].reshape(1, D_MODEL),
        "seg": (head_of_lane[:LANES, None] == head_of_lane[None, :LANES]).astype(BF16),
        "w_out": w_out[0], "norm_mem_w": norm_mem_w[0], "w_mq": w_mq[0], "w_mo": w_mo[0],
        "norm_ffn_w": norm_ffn_w[0], "w_pq": w_pq[0], "sub_keys": sub_keys[0],
        "expert_u": expert_u[0].astype(BF16), "expert_v": expert_v[0].astype(BF16),
        "norm_final_w": norm_final_w,
    }

    mem2d = mem_prompt.reshape(bp * N_MEM, D_MODEL)
    mem_k = _linear(mem2d, w_mk[0], norm_w=mem_norm_w[0], name="mem_k").reshape(bp, N_MEM, MEM_HEADS, MEM_HD)
    mem_v = _linear(mem2d, w_mv[0], norm_w=mem_norm_w[0], name="mem_v").reshape(bp, N_MEM, MEM_HEADS, MEM_HD)
    y_p, conv_p, ssm_p, shift_p, wkv_p = _block(
        x_prompt, mem_k, mem_v,
        jnp.zeros((bp, CONV_K - 1, CONV_DIM), F32), jnp.zeros((bp, H_SSD, SSD_HEADDIM, SSD_STATE), F32),
        jnp.zeros((bp, RWKV_PROJ), F32), jnp.zeros((bp, H_RWKV, HD_RWKV, HD_RWKV), F32), p,
        ssd_l=SSD_CHUNK, rwkv_c=RWKV_CHUNK)
    y_s, conv_s, ssm_s, shift_s, wkv_s = _block(
        x_sample, cache_mem_k[0], cache_mem_v[0], state_conv[0], state_ssm[0], state_shift[0], state_wkv[0], p,
        ssd_l=SUBLANES, rwkv_c=SUBLANES)
    return (y_p, y_s, ssm_p[None], conv_p[None], wkv_p[None], shift_p[None], mem_k[None], mem_v[None],
            ssm_s[None], conv_s[None], wkv_s[None], shift_s[None])
```

```python
import functools

import jax
import jax.numpy as jnp
from jax import lax
from jax.experimental import pallas as pl
from jax.experimental.pallas import tpu as pltpu

F32 = jnp.float32
BF16 = jnp.bfloat16
HI = lax.Precision.HIGHEST

D_MODEL = 1024
H_SSD = 16
SSD_HEADDIM = 64
SSD_STATE = 128
CONV_K = 4
CONV_DIM = 1536
H_RWKV = 16
HD_RWKV = 64
RWKV_PROJ = 3328
RWKV_LN_EPS = 64e-5
N_MEM = 256
MEM_HEADS = 4
MEM_HD = 256
N_KEYS = 128
PEER_HEADS = 8
PEER_TOPK = 16
NORM_EPS = 1e-6

LANES = 128
SUBLANES = 8
VMEM_LIMIT = 56 * 1024 * 1024

D_IN_PAD = 8064
COL_Z, COL_XS, COL_R, COL_K, COL_V, COL_GS, COL_GR = 0, 1, 2, 3, 4, 5, 6
COL_BC = 14
COL_LR = 30
COL_DT = 62

SSD_CHUNK = 128
RWKV_CHUNK = 64
CHUNK_PASSES = 1


def _cparams(sem):
    return pltpu.CompilerParams(dimension_semantics=sem, vmem_limit_bytes=VMEM_LIMIT)


def _dot(a, b, prec=None):
    return jnp.dot(a, b, preferred_element_type=F32, precision=prec)


def _dot_nt(a, b, prec=None):
    return lax.dot_general(a, b, (((1,), (1,)), ((), ())), preferred_element_type=F32, precision=prec)


def _dot_tn(a, b, prec=None):
    return lax.dot_general(a, b, (((0,), (0,)), ((), ())), preferred_element_type=F32, precision=prec)


def _split_bf16(x):
    hi = x.astype(BF16)
    lo = (x - hi.astype(F32)).astype(BF16)
    return hi, lo


def _dot_onehot(x, sel_bf16):
    hi, lo = _split_bf16(x)
    return _dot(hi, sel_bf16) + _dot(lo, sel_bf16)


NN = ((1,), (0,))
NT = ((1,), (1,))
TN = ((0,), (0,))


def _cmm(a, b, dims):
    dn = (dims, ((), ()))
    if CHUNK_PASSES == 6:
        return lax.dot_general(a, b, dn, preferred_element_type=F32, precision=HI)
    ah, al = _split_bf16(a)
    bh, bl = _split_bf16(b)
    out = lax.dot_general(ah, bh, dn, preferred_element_type=F32)
    if CHUNK_PASSES == 3:
        out = out + lax.dot_general(al, bh, dn, preferred_element_type=F32)
        out = out + lax.dot_general(ah, bl, dn, preferred_element_type=F32)
    return out


def _softplus(x):
    return jnp.maximum(x, 0.0) + jnp.log1p(jnp.exp(-jnp.abs(x)))


def _sigmoid(x):
    return 1.0 / (1.0 + jnp.exp(-x))


def _rmsnorm(x, w):
    return x * lax.rsqrt(jnp.mean(x * x, axis=-1, keepdims=True) + NORM_EPS) * w


def _iota(shape, dim):
    return lax.broadcasted_iota(jnp.int32, shape, dim)


def _linear_body(*refs, has_norm, has_res, passes):
    it = iter(refs)
    x_ref = next(it)
    nw_ref = next(it) if has_norm else None
    wh_ref = next(it)
    wl_ref = next(it) if passes == 3 else None
    res_ref = next(it) if has_res else None
    o_ref = next(it)
    x = x_ref[...]
    if has_norm:
        x = _rmsnorm(x, nw_ref[...])
    xh = x.astype(BF16)
    acc = _dot(xh, wh_ref[...])
    if passes == 3:
        xl = (x - xh.astype(F32)).astype(BF16)
        acc = acc + _dot(xl, wh_ref[...]) + _dot(xh, wl_ref[...])
    if has_res:
        acc = acc + res_ref[...]
    o_ref[...] = acc


def _linear(x, w, *, norm_w=None, res=None, passes=1, tm=512, tn=1024, name="linear"):
    m, k = x.shape
    n = w.shape[1]
    tm = min(tm, m)
    tn = min(tn, n)
    assert m % tm == 0 and n % tn == 0
    args = [x]
    specs = [pl.BlockSpec((tm, k), lambda i, j: (i, 0))]
    if norm_w is not None:
        args.append(norm_w.reshape(1, k))
        specs.append(pl.BlockSpec((1, k), lambda i, j: (0, 0)))
    if passes == 3:
        wh, wl = _split_bf16(w)
        args += [wh, wl]
        specs += [pl.BlockSpec((k, tn), lambda i, j: (0, j))] * 2
    else:
        args.append(w.astype(BF16))
        specs.append(pl.BlockSpec((k, tn), lambda i, j: (0, j)))
    if res is not None:
        args.append(res)
        specs.append(pl.BlockSpec((tm, tn), lambda i, j: (i, j)))
    body = functools.partial(_linear_body, has_norm=norm_w is not None, has_res=res is not None, passes=passes)
    return pl.pallas_call(
        body,
        grid=(m // tm, n // tn),
        in_specs=specs,
        out_specs=pl.BlockSpec((tm, tn), lambda i, j: (i, j)),
        out_shape=jax.ShapeDtypeStruct((m, n), F32),
        compiler_params=_cparams(("parallel", "arbitrary")),
        name=name,
    )(*args)


def _ssd_body(z_ref, xs_ref, bc_ref, dt_ref, convprev_ref, ssmprev_ref, convw_ref, convb_ref, dtb_ref,
              alog_ref, dskip_ref, normw_ref, expand_ref, y_ref, state_ref, ext_ref, *, l, n_valid):
    c = pl.program_id(1)

    @pl.when(c == 0)
    def _():
        ext_ref[0:8, :] = convprev_ref[...]
        state_ref[...] = ssmprev_ref[...]

    ext_ref[8:8 + l, 0:D_MODEL] = xs_ref[...]
    ext_ref[8:8 + l, D_MODEL:CONV_DIM] = bc_ref[...]
    conv = convb_ref[...]
    for i in range(CONV_K):
        conv = conv + ext_ref[pl.ds(5 + i, l), :] * convw_ref[i:i + 1, :]
    ext_ref[0:8, :] = ext_ref[l:l + 8, :]

    act = conv * _sigmoid(conv)
    xs = act[:, 0:D_MODEL]
    bm = act[:, D_MODEL:D_MODEL + 256]
    cm = act[:, D_MODEL + 256:CONV_DIM]

    dt = _softplus(dt_ref[...] + dtb_ref[...])
    if n_valid < l:
        dt = jnp.where(_iota((l, LANES), 0) < n_valid, dt, 0.0)
    ad = dt * (-jnp.exp(alog_ref[...]))
    row = _iota((l, l), 0)
    col = _iota((l, l), 1)
    causal = row >= col
    cs = _dot_sel3(jnp.where(causal, 1.0, 0.0).astype(BF16), ad)
    eye = jnp.where(_iota((LANES, LANES), 0) == _iota((LANES, LANES), 1), 1.0, 0.0).astype(BF16)
    cs_parts = _split3(cs)
    cs_t = sum(_dot_nt(eye, part) for part in cs_parts)
    expand = expand_ref[...]
    dt_e = sum(_dot(part, expand) for part in _split3(dt))
    cs_e = sum(_dot(part, expand) for part in cs_parts)
    xdt = xs * dt_e
    decay_in = jnp.exp(cs_e)
    decay_out = jnp.exp(cs_e[l - 1:l, :] - cs_e)
    xdt_dec = (xdt * decay_out).astype(BF16)
    lane_lo = _iota((l, LANES), 1) < SSD_HEADDIM
    row_lo = _iota((LANES, 1), 0) < SSD_HEADDIM

    ys = []
    for g in range(2):
        bg = bm[:, g * LANES:(g + 1) * LANES].astype(BF16)
        cg = cm[:, g * LANES:(g + 1) * LANES].astype(BF16)
        cb = _dot_nt(cg, bg)
        for pr in range(4 * g, 4 * g + 4):
            sl = slice(pr * LANES, (pr + 1) * LANES)
            h0, h1 = 2 * pr, 2 * pr + 1
            m0 = jnp.where(causal, jnp.exp(cs[:, h0:h0 + 1] - cs_t[h0:h0 + 1, :]), 0.0) * cb
            m1 = jnp.where(causal, jnp.exp(cs[:, h1:h1 + 1] - cs_t[h1:h1 + 1, :]), 0.0) * cb
            xp = xdt[:, sl]
            x_lo = jnp.where(lane_lo, xp, 0.0).astype(BF16)
            x_hi = jnp.where(lane_lo, 0.0, xp).astype(BF16)
            y_diag = _dot(m0.astype(BF16), x_lo) + _dot(m1.astype(BF16), x_hi)
            s_pair = state_ref[pr]
            y_off = _dot_nt(cg, s_pair.astype(BF16)) * decay_in[:, sl]
            ys.append(y_diag + y_off + dskip_ref[:, sl] * xs[:, sl])
            upd = _dot_tn(xdt_dec[:, sl], bg)
            last = cs[l - 1:l, :]
            row_decay = jnp.exp(jnp.where(row_lo, last[:, h0:h0 + 1], last[:, h1:h1 + 1]))
            state_ref[pr] = row_decay * s_pair + upd
    y = jnp.concatenate(ys, axis=1)
    z = z_ref[...]
    y_ref[...] = _rmsnorm(y * (z * _sigmoid(z)), normw_ref[...])


def _ssd(proj3, conv_prev8, ssm_prev, p, *, l, n_valid):
    b, lp, _ = proj3.shape
    nc = lp // l
    full = lambda shape: pl.BlockSpec(shape, lambda i, c: (0,) * len(shape))
    body = functools.partial(_ssd_body, l=l, n_valid=n_valid)
    return pl.pallas_call(
        body,
        grid=(b, nc),
        in_specs=[
            pl.BlockSpec((None, l, 1024), lambda i, c: (i, c, COL_Z)),
            pl.BlockSpec((None, l, 1024), lambda i, c: (i, c, COL_XS)),
            pl.BlockSpec((None, l, 512), lambda i, c: (i, c, COL_BC)),
            pl.BlockSpec((None, l, 128), lambda i, c: (i, c, COL_DT)),
            pl.BlockSpec((None, 8, CONV_DIM), lambda i, c: (i, 0, 0)),
            pl.BlockSpec((None, 8, 128, 128), lambda i, c: (i, 0, 0, 0)),
            full((CONV_K, CONV_DIM)), full((1, CONV_DIM)), full((1, 128)), full((1, 128)),
            full((1, 1024)), full((1, 1024)), full((128, 1024)),
        ],
        out_specs=[
            pl.BlockSpec((None, l, 1024), lambda i, c: (i, c, 0)),
            pl.BlockSpec((None, 8, 128, 128), lambda i, c: (i, 0, 0, 0)),
        ],
        out_shape=[jax.ShapeDtypeStruct((b, lp, 1024), F32), jax.ShapeDtypeStruct((b, 8, 128, 128), F32)],
        scratch_shapes=[pltpu.VMEM((l + 8, CONV_DIM), F32)],
        compiler_params=_cparams(("parallel", "arbitrary")),
        name="ssd",
    )(proj3, proj3, proj3, proj3, conv_prev8, ssm_prev, p["conv_w"], p["conv_b"], p["dt_bias"], p["a_log"],
      p["d_skip_e"], p["ssd_norm_w"], p["expand"])


def _dot_w(x, w_ref):
    xh, xl = _split_bf16(x)
    return _dot(xh, w_ref[0]) + _dot(xl, w_ref[0]) + _dot(xh, w_ref[1])


def _split3(x):
    x1 = x.astype(BF16)
    r1 = x - x1.astype(F32)
    x2 = r1.astype(BF16)
    x3 = (r1 - x2.astype(F32)).astype(BF16)
    return x1, x2, x3


def _dot_sel3(sel_bf16, x):
    return sum(_dot(sel_bf16, part) for part in _split3(x))


def _head_sum(x, seg_pair):
    rows = x.shape[0]
    n = x.shape[1] // LANES
    stacked = jnp.concatenate([x[:, i * LANES:(i + 1) * LANES] for i in range(n)], axis=0)
    s = _dot_onehot(stacked, seg_pair)
    return jnp.concatenate([s[i * rows:(i + 1) * rows] for i in range(n)], axis=1)


def _rwkv_body(r_ref, k_ref, v_ref, lr_ref, shprev_ref, wkvprev_ref, mu_ref, w0_ref, w2_ref, a0_ref, a2_ref,
               g2_ref, kk_ref, ka_ref, rk_ref, lnw_ref, lnb_ref, seg_ref, y_ref, wkv_ref, ext_ref, state_ref,
               *, C, n_valid):
    c = pl.program_id(1)
    same_head = (_iota((LANES, LANES), 0) < HD_RWKV) == (_iota((LANES, LANES), 1) < HD_RWKV)

    @pl.when(c == 0)
    def _():
        ext_ref[0:8, :] = shprev_ref[...]
        for p in range(8):
            s_c = wkvprev_ref[p]
            state_ref[p] = jnp.where(same_head, jnp.concatenate([s_c, s_c], axis=1), 0.0)

    ext_ref[8:8 + C, 0:1024] = r_ref[...]
    ext_ref[8:8 + C, 1024:2048] = k_ref[...]
    ext_ref[8:8 + C, 2048:3072] = v_ref[...]
    ext_ref[8:8 + C, 3072:3328] = lr_ref[...]
    cur = ext_ref[8:8 + C, :]
    prev = ext_ref[pl.ds(7, C), :]
    mixed = cur + (prev - cur) * mu_ref[...]
    ext_ref[0:8, :] = ext_ref[C:C + 8, :]

    r = mixed[:, 0:1024]
    k = mixed[:, 1024:2048]
    v = mixed[:, 2048:3072]
    wa = mixed[:, 3072:3200]
    gl = mixed[:, 3200:3328]
    seg = seg_ref[...]

    wlog = -_softplus(-(w0_ref[...] + _dot_w(jnp.tanh(wa), w2_ref))) - 0.5
    lw = -jnp.exp(wlog)
    a_sig = _sigmoid(a0_ref[...] + _dot_w(wa, a2_ref))
    g = _dot_w(_sigmoid(gl), g2_ref)
    kk = k * kk_ref[...]
    kk = kk * lax.rsqrt(jnp.maximum(_head_sum(kk * kk, seg), 1e-24))
    kh = k * (1.0 + (a_sig - 1.0) * ka_ref[...])
    av = -kk
    bv = kk * a_sig
    bonus = _head_sum(r * kh * rk_ref[...], seg) * v
    if n_valid < C:
        valid = _iota((C, 1024), 0) < n_valid
        lw = jnp.where(valid, lw, 0.0)
        av = jnp.where(valid, av, 0.0)
        bv = jnp.where(valid, bv, 0.0)
        kh = jnp.where(valid, kh, 0.0)
        v = jnp.where(valid, v, 0.0)

    causal_c = jnp.where(_iota((C, C), 0) >= _iota((C, C), 1), 1.0, 0.0).astype(BF16)
    cl = _dot_sel3(causal_c, lw)
    e_pos = jnp.exp(cl)
    e_neg = jnp.exp(-cl)
    rt = r * e_pos
    at = av * jnp.exp(cl - lw)
    bt = bv * e_neg
    kt = kh * e_neg
    p_end = e_pos[C - 1:C, :]

    R2 = 2 * C
    lane_lo = _iota((C, LANES), 1) < HD_RWKV
    rr = _iota((R2, R2), 0)
    cc = _iota((R2, R2), 1)
    strict = rr > cc
    incl = rr >= cc
    eye = (rr == cc).astype(F32)

    def stack(x):
        return jnp.concatenate([jnp.where(lane_lo, x, 0.0), jnp.where(lane_lo, 0.0, x)], axis=0)

    pairs = range(8)
    sls = [slice(p * LANES, (p + 1) * LANES) for p in pairs]
    a_s = [stack(at[:, sl]) for sl in sls]
    r_s = [stack(rt[:, sl]) for sl in sls]
    b_s = [stack(bt[:, sl]) for sl in sls]
    k_s = [stack(kt[:, sl]) for sl in sls]
    v_s = [stack(v[:, sl]) for sl in sls]
    s_bd = [state_ref[p] for p in pairs]
    wide = R2 == LANES
    if wide:
        ar = [jnp.concatenate([a_s[p], r_s[p]], axis=0) for p in pairs]
        bk = [jnp.concatenate([b_s[p], k_s[p]], axis=0) for p in pairs]
        quad = [_cmm(ar[p], bk[p], NT) for p in pairs]
        ars = [_cmm(ar[p], s_bd[p], NT) for p in pairs]
        m_ab = [quad[p][:R2, :R2] for p in pairs]
        m_ak = [jnp.where(strict, quad[p][:R2, R2:], 0.0) for p in pairs]
        w = [ars[p][:R2] + _cmm(m_ak[p], v_s[p], NN) for p in pairs]
    else:
        m_ab = [_cmm(a_s[p], b_s[p], NT) for p in pairs] if n_valid > 1 else None
        m_ak = [jnp.where(strict, _cmm(a_s[p], k_s[p], NT), 0.0) for p in pairs]
        w = [_cmm(a_s[p], s_bd[p], NT) + _cmm(m_ak[p], v_s[p], NN) for p in pairs]
    if n_valid > 1:
        x = [jnp.where(strict, m_ab[p], 0.0) for p in pairs]
        t = [eye + x[p] for p in pairs]
        span = 2
        while span < C:
            x = [_cmm(x[p], x[p], NN) for p in pairs]
            t = [t[p] + _cmm(t[p], x[p], NN) for p in pairs]
            span *= 2
        u = [_cmm(t[p], w[p], NN) for p in pairs]
    else:
        u = w
    ys = []
    for p in pairs:
        if wide:
            n_r = jnp.where(jnp.concatenate([incl, incl], axis=1), quad[p][R2:, :], 0.0)
            uv = jnp.concatenate([u[p], v_s[p]], axis=0)
            y2 = ars[p][R2:] + _cmm(n_r, uv, NN)
            upd = _cmm(uv, bk[p], TN)
        else:
            n_rb = jnp.where(incl, _cmm(r_s[p], b_s[p], NT), 0.0)
            n_rk = jnp.where(incl, _cmm(r_s[p], k_s[p], NT), 0.0)
            y2 = _cmm(r_s[p], s_bd[p], NT) + _cmm(n_rb, u[p], NN) + _cmm(n_rk, v_s[p], NN)
            upd = _cmm(u[p], b_s[p], TN) + _cmm(v_s[p], k_s[p], TN)
        ys.append(y2[0:C] + y2[C:R2])
        state_ref[p] = (s_bd[p] + upd) * p_end[:, sls[p]]
    yr = jnp.concatenate(ys, axis=1)
    mean = _head_sum(yr, seg) * (1.0 / HD_RWKV)
    d = yr - mean
    var = _head_sum(d * d, seg) * (1.0 / HD_RWKV)
    yn = d * lax.rsqrt(var + RWKV_LN_EPS) * lnw_ref[...] + lnb_ref[...]
    y_ref[...] = (yn + bonus) * g

    @pl.when(c == pl.num_programs(1) - 1)
    def _():
        upper = _iota((LANES, HD_RWKV), 0) < HD_RWKV
        for p in range(8):
            s_p = state_ref[p]
            wkv_ref[p] = jnp.where(upper, s_p[:, 0:HD_RWKV], s_p[:, HD_RWKV:LANES])


def _rwkv(proj3, shift_prev8, wkv_prev, p, *, C, n_valid):
    b, lp, _ = proj3.shape
    nc = lp // C
    full = lambda shape: pl.BlockSpec(shape, lambda i, c: (0,) * len(shape))
    body = functools.partial(_rwkv_body, C=C, n_valid=n_valid)
    return pl.pallas_call(
        body,
        grid=(b, nc),
        in_specs=[
            pl.BlockSpec((None, C, 1024), lambda i, c: (i, c, COL_R)),
            pl.BlockSpec((None, C, 1024), lambda i, c: (i, c, COL_K)),
            pl.BlockSpec((None, C, 1024), lambda i, c: (i, c, COL_V)),
            pl.BlockSpec((None, C, 256), lambda i, c: (i, c, COL_LR)),
            pl.BlockSpec((None, 8, RWKV_PROJ), lambda i, c: (i, 0, 0)),
            pl.BlockSpec((None, 8, 128, HD_RWKV), lambda i, c: (i, 0, 0, 0)),
            full((1, RWKV_PROJ)), full((1, 1024)), full((2, 128, 1024)), full((1, 1024)), full((2, 128, 1024)),
            full((2, 128, 1024)), full((1, 1024)), full((1, 1024)), full((1, 1024)), full((1, 1024)), full((1, 1024)),
            full((128, 128)),
        ],
        out_specs=[
            pl.BlockSpec((None, C, 1024), lambda i, c: (i, c, 0)),
            pl.BlockSpec((None, 8, 128, HD_RWKV), lambda i, c: (i, 0, 0, 0)),
        ],
        out_shape=[jax.ShapeDtypeStruct((b, lp, 1024), F32), jax.ShapeDtypeStruct((b, 8, 128, HD_RWKV), F32)],
        scratch_shapes=[pltpu.VMEM((C + 8, RWKV_PROJ), F32), pltpu.VMEM((8, 128, 128), F32)],
        compiler_params=_cparams(("parallel", "arbitrary")),
        name="rwkv",
    )(proj3, proj3, proj3, proj3, shift_prev8, wkv_prev, p["rwkv_mu"], p["rwkv_w0"], p["rwkv_w2p"], p["rwkv_a0"],
      p["rwkv_a2p"], p["rwkv_g2"], p["rwkv_k_k"], p["rwkv_k_a"], p["rwkv_r_k"], p["rwkv_ln_w"], p["rwkv_ln_b"],
      p["seg"])


def _merge_body(x_ref, gs_ref, gr_ref, ys_ref, yr_ref, w_ref, o_ref):
    merged = _sigmoid(gs_ref[...]) * ys_ref[...] + _sigmoid(gr_ref[...]) * yr_ref[...]
    o_ref[...] = x_ref[...] + _dot(merged.astype(BF16), w_ref[...])


def _merge_out(x, proj, y_ssd, y_rwkv, w_out, *, tm=512):
    m = x.shape[0]
    tm = min(tm, m)
    row = lambda j: pl.BlockSpec((tm, 1024), lambda i: (i, j))
    return pl.pallas_call(
        _merge_body,
        grid=(m // tm,),
        in_specs=[row(0), row(COL_GS), row(COL_GR), row(0), row(0), pl.BlockSpec((1024, 1024), lambda i: (0, 0))],
        out_specs=row(0),
        out_shape=jax.ShapeDtypeStruct((m, 1024), F32),
        compiler_params=_cparams(("parallel",)),
        name="merge_out",
    )(x, proj, proj, y_ssd, y_rwkv, w_out.astype(BF16))


def _attn_body(q_ref, k_ref, v_ref, o_ref):
    q = q_ref[...]
    outs = []
    for h in range(MEM_HEADS):
        sl = slice(h * MEM_HD, (h + 1) * MEM_HD)
        s = _dot_nt(q[:, sl].astype(BF16), k_ref[:, sl].astype(BF16)) * (MEM_HD ** -0.5)
        s = s - jnp.max(s, axis=-1, keepdims=True)
        e = jnp.exp(s)
        prob = e / jnp.sum(e, axis=-1, keepdims=True)
        outs.append(_dot(prob.astype(BF16), v_ref[:, sl].astype(BF16)))
    o_ref[...] = jnp.concatenate(outs, axis=1)


def _attn(q3, mem_k, mem_v, *, tq):
    b, lq, _ = q3.shape
    return pl.pallas_call(
        _attn_body,
        grid=(b, lq // tq),
        in_specs=[
            pl.BlockSpec((None, tq, 1024), lambda i, j: (i, j, 0)),
            pl.BlockSpec((None, N_MEM, 1024), lambda i, j: (i, 0, 0)),
            pl.BlockSpec((None, N_MEM, 1024), lambda i, j: (i, 0, 0)),
        ],
        out_specs=pl.BlockSpec((None, tq, 1024), lambda i, j: (i, j, 0)),
        out_shape=jax.ShapeDtypeStruct((b, lq, 1024), F32),
        compiler_params=_cparams(("parallel", "arbitrary")),
        name="mem_attn",
    )(q3, mem_k, mem_v)


def _attn_step_body(q_ref, k_ref, v_ref, o_ref):
    s = jnp.sum(k_ref[...] * q_ref[...][None], axis=-1, keepdims=True) * (MEM_HD ** -0.5)
    e = jnp.exp(s - jnp.max(s, axis=0, keepdims=True))
    prob = e / jnp.sum(e, axis=0, keepdims=True)
    o_ref[...] = jnp.sum(prob * v_ref[...], axis=0)


def _attn_step(q, mem_k, mem_v):
    b = q.shape[0]
    qspec = pl.BlockSpec((None, MEM_HEADS, MEM_HD), lambda i: (i, 0, 0))
    kvspec = pl.BlockSpec((None, N_MEM, MEM_HEADS, MEM_HD), lambda i: (i, 0, 0, 0))
    return pl.pallas_call(
        _attn_step_body,
        grid=(b,),
        in_specs=[qspec, kvspec, kvspec],
        out_specs=qspec,
        out_shape=jax.ShapeDtypeStruct((b, MEM_HEADS, MEM_HD), F32),
        compiler_params=_cparams(("parallel",)),
        name="mem_attn_step",
    )(q, mem_k, mem_v)


def _stage2_pairs():
    return [(r1, r2) for r1 in range(PEER_TOPK) for r2 in range(PEER_TOPK) if (r1 + 1) * (r2 + 1) <= PEER_TOPK]


N_CAND = 56


def _topk_body(q_ref, keys_ref, gate_ref, i1_ref, i2_ref, s_scr, i_scr, cand_scr, code_scr, res_scr, *, T):
    neg = -jnp.inf
    key_row = _iota((N_KEYS, T), 0).astype(F32)
    cand_row = _iota((N_CAND, T), 0).astype(F32)
    pairs = _stage2_pairs()
    q = q_ref[...]
    for h in range(PEER_HEADS):
        for c in range(2):
            hc = 2 * h + c
            s = _dot_nt(keys_ref[hc], q[:, hc * LANES:(hc + 1) * LANES], HI)
            for r in range(PEER_TOPK):
                m = jnp.max(s, axis=0, keepdims=True)
                idx = jnp.min(jnp.where(s == m, key_row, float(N_KEYS)), axis=0, keepdims=True)
                s_scr[c, r:r + 1, :] = m
                i_scr[c, r:r + 1, :] = idx
                s = jnp.where(key_row == idx, neg, s)
        cand_scr[...] = jnp.full((N_CAND, T), neg, F32)
        code_scr[...] = jnp.zeros((N_CAND, T), F32)
        for j, (r1, r2) in enumerate(pairs):
            cand_scr[j:j + 1, :] = s_scr[0, r1:r1 + 1, :] + s_scr[1, r2:r2 + 1, :]
            code_scr[j:j + 1, :] = i_scr[0, r1:r1 + 1, :] * float(N_KEYS) + i_scr[1, r2:r2 + 1, :]
        cand = cand_scr[...]
        code = code_scr[...]
        for r in range(PEER_TOPK):
            m = jnp.max(cand, axis=0, keepdims=True)
            idx = jnp.min(jnp.where(cand == m, cand_row, float(N_CAND)), axis=0, keepdims=True)
            hit = cand_row == idx
            res_scr[0, r:r + 1, :] = m
            res_scr[1, r:r + 1, :] = jnp.sum(jnp.where(hit, code, 0.0), axis=0, keepdims=True)
            cand = jnp.where(hit, neg, cand)
        top = res_scr[0]
        e = jnp.exp(top - jnp.max(top, axis=0, keepdims=True))
        rows = slice(h * PEER_TOPK, (h + 1) * PEER_TOPK)
        gate_ref[rows, :] = e / jnp.sum(e, axis=0, keepdims=True)
        first = jnp.floor(res_scr[1] * (1.0 / N_KEYS))
        i1_ref[rows, :] = first.astype(jnp.int32)
        i2_ref[rows, :] = (res_scr[1] - first * float(N_KEYS)).astype(jnp.int32)


def _peer_topk(q, sub_keys, *, T=256):
    m = q.shape[0]
    T = min(T, m)
    body = functools.partial(_topk_body, T=T)
    out = pl.BlockSpec((128, T), lambda i: (0, i))
    return pl.pallas_call(
        body,
        grid=(m // T,),
        in_specs=[pl.BlockSpec((T, 2048), lambda i: (i, 0)), pl.BlockSpec((16, 128, 128), lambda i: (0, 0, 0))],
        out_specs=[out, out, out],
        out_shape=[jax.ShapeDtypeStruct((128, m), F32), jax.ShapeDtypeStruct((128, m), jnp.int32),
                   jax.ShapeDtypeStruct((128, m), jnp.int32)],
        scratch_shapes=[pltpu.VMEM((2, PEER_TOPK, T), F32), pltpu.VMEM((2, PEER_TOPK, T), F32),
                        pltpu.VMEM((N_CAND, T), F32), pltpu.VMEM((N_CAND, T), F32),
                        pltpu.VMEM((2, PEER_TOPK, T), F32)],
        compiler_params=_cparams(("parallel",)),
        name="peer_topk",
    )(q, sub_keys.reshape(16, 128, 128))


I1_PER_STEP = 8
G_PITCH = 136


def _peer_body(x_ref, gate_ref, i1_ref, i2_ref, nw_ref, fw_ref, u_ref, v_ref, o_ref, hn_scr, g_scr, coef_scr, *, T):
    j = pl.program_id(1)

    @pl.when(j == 0)
    def _():
        hn_scr[...] = _rmsnorm(x_ref[...], nw_ref[...]).astype(BF16)
        o_ref[...] = jnp.zeros_like(o_ref)
        key_row = _iota((N_KEYS, LANES), 0)

        def build(t, carry):
            g_row = gate_ref[pl.ds(t, 1), :]
            a_t = jnp.where(key_row == i1_ref[pl.ds(t, 1), :], g_row, 0.0).astype(BF16)
            b_t = jnp.where(key_row == i2_ref[pl.ds(t, 1), :], 1.0, 0.0).astype(BF16)
            g_scr[pl.ds(pl.multiple_of(t * G_PITCH, 8), N_KEYS), :] = _dot_nt(a_t, b_t)
            return carry

        lax.fori_loop(0, T, build, 0, unroll=32)

    hn = hn_scr[...]
    for cidx in range(I1_PER_STEP // 2):
        act = _dot_nt(hn, u_ref[cidx * 2 * LANES:(cidx + 1) * 2 * LANES, :])
        ge = 0.5 * act * (1.0 + lax.erf(act * (2.0 ** -0.5)))
        for half in range(2):
            ii = 2 * cidx + half
            g_t = g_scr[pl.ds(j * I1_PER_STEP + ii, T, stride=G_PITCH), :]
            coef_scr[:, ii * LANES:(ii + 1) * LANES] = (g_t * ge[:, half * LANES:(half + 1) * LANES]).astype(BF16)
    o_ref[...] += _dot(coef_scr[...], v_ref[...])

    @pl.when(j == pl.num_programs(1) - 1)
    def _():
        o_ref[...] = _rmsnorm(x_ref[...] + o_ref[...], fw_ref[...])


def _peer_mix(x, gate, i1, i2, norm_w, final_w, u_bf16, v_bf16, *, T=512):
    m = x.shape[0]
    T = min(T, m)
    ne = I1_PER_STEP * N_KEYS
    body = functools.partial(_peer_body, T=T)
    once = pl.Buffered(1)
    row = lambda w: pl.BlockSpec((T, w), lambda i, j: (i, 0), pipeline_mode=once)
    vec = pl.BlockSpec((1, 1024), lambda i, j: (0, 0), pipeline_mode=once)
    return pl.pallas_call(
        body,
        grid=(m // T, N_KEYS // I1_PER_STEP),
        in_specs=[row(1024), row(128), row(128), row(128), vec, vec,
                  pl.BlockSpec((ne, 1024), lambda i, j: (j, 0)), pl.BlockSpec((ne, 1024), lambda i, j: (j, 0))],
        out_specs=pl.BlockSpec((T, 1024), lambda i, j: (i, 0)),
        out_shape=jax.ShapeDtypeStruct((m, 1024), F32),
        scratch_shapes=[pltpu.VMEM((T, 1024), BF16), pltpu.VMEM((T * G_PITCH, LANES), F32),
                        pltpu.VMEM((T, 1024), BF16)],
        compiler_params=_cparams(("parallel", "arbitrary")),
        name="peer_mix",
    )(x, gate, i1, i2, norm_w.reshape(1, 1024), final_w.reshape(1, 1024), u_bf16, v_bf16)


def _pad_rows(x, rows):
    return jnp.pad(x, ((0, 0), (rows - x.shape[1], 0), (0, 0)))


def _block(x, mem_k, mem_v, conv_prev, ssm_prev, shift_prev, wkv_prev, p, *, ssd_l, rwkv_c):
    b, L, _ = x.shape
    m = b * L
    x2d = x.reshape(m, D_MODEL)
    proj = _linear(x2d, p["w_in"], norm_w=p["norm_mix_w"], tn=1152, name="in_proj")
    proj3 = proj.reshape(b, L, D_IN_PAD)
    lp = max(L, SUBLANES)
    n_valid = L if L < SUBLANES else max(ssd_l, rwkv_c)
    proj3p = proj3 if lp == L else jnp.pad(proj3, ((0, 0), (0, lp - L), (0, 0)))

    y_ssd, ssm_new = _ssd(proj3p, _pad_rows(conv_prev, 8), ssm_prev.reshape(b, 8, 128, 128), p,
                          l=ssd_l, n_valid=min(n_valid, ssd_l))
    y_rwkv, wkv_new = _rwkv(proj3p, _pad_rows(shift_prev[:, None, :], 8), wkv_prev.reshape(b, 8, 128, HD_RWKV), p,
                            C=rwkv_c, n_valid=min(n_valid, rwkv_c))
    x1 = _merge_out(x2d, proj, y_ssd[:, :L].reshape(m, D_MODEL), y_rwkv[:, :L].reshape(m, D_MODEL), p["w_out"])

    tail = proj3[:, max(L - (CONV_K - 1), 0):]
    xbc = jnp.concatenate([tail[..., 1024:2048], tail[..., 7168:7680]], axis=-1)
    rw = jnp.concatenate([tail[..., 2048:5120], tail[..., 7680:7936]], axis=-1)
    conv_new = jnp.concatenate([conv_prev, xbc], axis=1)[:, -(CONV_K - 1):]
    shift_new = rw[:, -1]
    ssm_new = ssm_new.reshape(b, H_SSD, SSD_HEADDIM, SSD_STATE)
    wkv_new = wkv_new.reshape(b, H_RWKV, HD_RWKV, HD_RWKV)

    q = _linear(x1, p["w_mq"], norm_w=p["norm_mem_w"], name="mem_q")
    if L == 1:
        o = _attn_step(q.reshape(b, MEM_HEADS, MEM_HD), mem_k, mem_v)
    else:
        o = _attn(q.reshape(b, L, D_MODEL), mem_k.reshape(b, N_MEM, D_MODEL), mem_v.reshape(b, N_MEM, D_MODEL),
                  tq=min(512, L))
    x2 = _linear(o.reshape(m, D_MODEL), p["w_mo"], res=x1, name="mem_o")

    pq = _linear(x2, p["w_pq"], norm_w=p["norm_ffn_w"], name="peer_q")
    gate_t, i1_t, i2_t = _peer_topk(pq, p["sub_keys"])
    y = _peer_mix(x2, gate_t.T, i1_t.T, i2_t.T, p["norm_ffn_w"], p["norm_final_w"], p["expert_u"], p["expert_v"])
    return y.reshape(b, L, D_MODEL), conv_new, ssm_new, shift_new, wkv_new


def kernel(x_prompt, x_sample, mem_prompt, state_ssm, state_conv, state_wkv, state_shift, cache_mem_k, cache_mem_v, norm_mix_w, w_in, conv_w, conv_b, dt_bias, a_log, d_skip, ssd_norm_w, rwkv_mu, rwkv_w0, rwkv_w2, rwkv_a0, rwkv_a2, rwkv_g2, rwkv_k_k, rwkv_k_a, rwkv_r_k, rwkv_ln_w, rwkv_ln_b, w_out, norm_mem_w, mem_norm_w, w_mk, w_mv, w_mq, w_mo, norm_ffn_w, w_pq, sub_keys, expert_u, expert_v, norm_final_w):
    assert w_in.shape[0] == 1, "single-layer problem"
    bp = x_prompt.shape[0]
    bs = x_sample.shape[0]
    wi = w_in[0]
    w_in_perm = jnp.concatenate(
        [wi[:, 0:2048], wi[:, 2576:5648], wi[:, 5904:7952], wi[:, 2048:2560], wi[:, 5648:5904], wi[:, 2560:2576],
         jnp.zeros((D_MODEL, LANES - H_SSD), F32)], axis=1)
    pad16 = lambda a: jnp.pad(a.reshape(1, H_SSD), ((0, 0), (0, LANES - H_SSD)))
    head_of_lane = jnp.arange(D_MODEL) // SSD_HEADDIM
    zeros64 = jnp.zeros((64, D_MODEL), F32)
    hi_lo = lambda w: jnp.stack(_split_bf16(w))
    p = {
        "norm_mix_w": norm_mix_w[0], "w_in": w_in_perm,
        "conv_w": conv_w[0], "conv_b": conv_b[0].reshape(1, CONV_DIM),
        "dt_bias": pad16(dt_bias[0]), "a_log": pad16(a_log[0]),
        "d_skip_e": jnp.repeat(d_skip[0], SSD_HEADDIM).reshape(1, D_MODEL),
        "ssd_norm_w": ssd_norm_w[0].reshape(1, D_MODEL),
        "expand": (jnp.arange(LANES)[:, None] == head_of_lane[None, :]).astype(BF16),
        "rwkv_mu": rwkv_mu[0].reshape(1, RWKV_PROJ), "rwkv_w0": rwkv_w0[0].reshape(1, D_MODEL),
        "rwkv_w2p": hi_lo(jnp.concatenate([rwkv_w2[0], zeros64], axis=0)),
        "rwkv_a0": rwkv_a0[0].reshape(1, D_MODEL),
        "rwkv_a2p": hi_lo(jnp.concatenate([zeros64, rwkv_a2[0]], axis=0)),
        "rwkv_g2": hi_lo(rwkv_g2[0]), "rwkv_k_k": rwkv_k_k[0].reshape(1, D_MODEL),
        "rwkv_k_a": rwkv_k_a[0].reshape(1, D_MODEL), "rwkv_r_k": rwkv_r_k[0].reshape(1, D_MODEL),
        "rwkv_ln_w": rwkv_ln_w[0].reshape(1, D_MODEL), "rwkv_ln_b": rwkv_ln_b[0].reshape(1, D_MODEL),
        "seg": (head_of_lane[:LANES, None] == head_of_lane[None, :LANES]).astype(BF16),
        "w_out": w_out[0], "norm_mem_w": norm_mem_w[0], "w_mq": w_mq[0], "w_mo": w_mo[0],
        "norm_ffn_w": norm_ffn_w[0], "w_pq": w_pq[0], "sub_keys": sub_keys[0],
        "expert_u": expert_u[0].astype(BF16), "expert_v": expert_v[0].astype(BF16),
        "norm_final_w": norm_final_w,
    }

    mem2d = mem_prompt.reshape(bp * N_MEM, D_MODEL)
    mem_k = _linear(mem2d, w_mk[0], norm_w=mem_norm_w[0], name="mem_k").reshape(bp, N_MEM, MEM_HEADS, MEM_HD)
    mem_v = _linear(mem2d, w_mv[0], norm_w=mem_norm_w[0], name="mem_v").reshape(bp, N_MEM, MEM_HEADS, MEM_HD)
    y_p, conv_p, ssm_p, shift_p, wkv_p = _block(
        x_prompt, mem_k, mem_v,
        jnp.zeros((bp, CONV_K - 1, CONV_DIM), F32), jnp.zeros((bp, H_SSD, SSD_HEADDIM, SSD_STATE), F32),
        jnp.zeros((bp, RWKV_PROJ), F32), jnp.zeros((bp, H_RWKV, HD_RWKV, HD_RWKV), F32), p,
        ssd_l=SSD_CHUNK, rwkv_c=RWKV_CHUNK)
    y_s, conv_s, ssm_s, shift_s, wkv_s = _block(
        x_sample, cache_mem_k[0], cache_mem_v[0], state_conv[0], state_ssm[0], state_shift[0], state_wkv[0], p,
        ssd_l=SUBLANES, rwkv_c=SUBLANES)
    return (y_p, y_s, ssm_p[None], conv_p[None], wkv_p[None], shift_p[None], mem_k[None], mem_v[None],
            ssm_s[None], conv_s[None], wkv_s[None], shift_s[None])
```

```python
import functools

import jax
import jax.numpy as jnp
from jax import lax
from jax.experimental import pallas as pl
from jax.experimental.pallas import tpu as pltpu

F32 = jnp.float32
BF16 = jnp.bfloat16
HI = lax.Precision.HIGHEST

D_MODEL = 1024
H_SSD = 16
SSD_HEADDIM = 64
SSD_STATE = 128
CONV_K = 4
CONV_DIM = 1536
H_RWKV = 16
HD_RWKV = 64
RWKV_PROJ = 3328
RWKV_LN_EPS = 64e-5
N_MEM = 256
MEM_HEADS = 4
MEM_HD = 256
N_KEYS = 128
PEER_HEADS = 8
PEER_TOPK = 16
NORM_EPS = 1e-6

LANES = 128
SUBLANES = 8
VMEM_LIMIT = 56 * 1024 * 1024

D_IN_PAD = 8064
COL_Z, COL_XS, COL_R, COL_K, COL_V, COL_GS, COL_GR = 0, 1, 2, 3, 4, 5, 6
COL_BC = 14
COL_LR = 30
COL_DT = 62

SSD_CHUNK = 128
RWKV_CHUNK = 64


def _cparams(sem):
    return pltpu.CompilerParams(dimension_semantics=sem, vmem_limit_bytes=VMEM_LIMIT)


def _dot(a, b, prec=None):
    return jnp.dot(a, b, preferred_element_type=F32, precision=prec)


def _dot_nt(a, b, prec=None):
    return lax.dot_general(a, b, (((1,), (1,)), ((), ())), preferred_element_type=F32, precision=prec)


def _dot_tn(a, b, prec=None):
    return lax.dot_general(a, b, (((0,), (0,)), ((), ())), preferred_element_type=F32, precision=prec)


def _split_bf16(x):
    hi = x.astype(BF16)
    lo = (x - hi.astype(F32)).astype(BF16)
    return hi, lo


def _dot_onehot(x, sel_bf16):
    hi, lo = _split_bf16(x)
    return _dot(hi, sel_bf16) + _dot(lo, sel_bf16)


NN = ((1,), (0,))
NT = ((1,), (1,))
TN = ((0,), (0,))


def _cmm(a, b, dims):
    return lax.dot_general(a.astype(BF16), b.astype(BF16), (dims, ((), ())), preferred_element_type=F32)


def _softplus(x):
    return jnp.maximum(x, 0.0) + jnp.log1p(jnp.exp(-jnp.abs(x)))


def _sigmoid(x):
    return 1.0 / (1.0 + jnp.exp(-x))


def _rmsnorm(x, w):
    return x * lax.rsqrt(jnp.mean(x * x, axis=-1, keepdims=True) + NORM_EPS) * w


def _iota(shape, dim):
    return lax.broadcasted_iota(jnp.int32, shape, dim)


def _linear_body(*refs, has_norm, has_res):
    it = iter(refs)
    x_ref = next(it)
    nw_ref = next(it) if has_norm else None
    w_ref = next(it)
    res_ref = next(it) if has_res else None
    o_ref = next(it)
    x = x_ref[...]
    if has_norm:
        x = _rmsnorm(x, nw_ref[...])
    acc = _dot(x.astype(BF16), w_ref[...])
    if has_res:
        acc = acc + res_ref[...]
    o_ref[...] = acc


def _linear(x, w, *, norm_w=None, res=None, tm=512, tn=1024, name="linear"):
    m, k = x.shape
    n = w.shape[1]
    tm = min(tm, m)
    tn = min(tn, n)
    assert m % tm == 0 and n % tn == 0
    args = [x]
    specs = [pl.BlockSpec((tm, k), lambda i, j: (i, 0))]
    if norm_w is not None:
        args.append(norm_w.reshape(1, k))
        specs.append(pl.BlockSpec((1, k), lambda i, j: (0, 0)))
    args.append(w.astype(BF16))
    specs.append(pl.BlockSpec((k, tn), lambda i, j: (0, j)))
    if res is not None:
        args.append(res)
        specs.append(pl.BlockSpec((tm, tn), lambda i, j: (i, j)))
    body = functools.partial(_linear_body, has_norm=norm_w is not None, has_res=res is not None)
    return pl.pallas_call(
        body,
        grid=(m // tm, n // tn),
        in_specs=specs,
        out_specs=pl.BlockSpec((tm, tn), lambda i, j: (i, j)),
        out_shape=jax.ShapeDtypeStruct((m, n), F32),
        compiler_params=_cparams(("parallel", "arbitrary")),
        name=name,
    )(*args)


def _ssd_body(z_ref, xs_ref, bc_ref, dt_ref, convprev_ref, ssmprev_ref, convw_ref, convb_ref, dtb_ref,
              alog_ref, dskip_ref, normw_ref, expand_ref, y_ref, state_ref, ext_ref, *, l, n_valid):
    c = pl.program_id(1)

    @pl.when(c == 0)
    def _():
        ext_ref[0:8, :] = convprev_ref[...]
        state_ref[...] = ssmprev_ref[...]

    ext_ref[8:8 + l, 0:D_MODEL] = xs_ref[...]
    ext_ref[8:8 + l, D_MODEL:CONV_DIM] = bc_ref[...]
    conv = convb_ref[...]
    for i in range(CONV_K):
        conv = conv + ext_ref[pl.ds(5 + i, l), :] * convw_ref[i:i + 1, :]
    ext_ref[0:8, :] = ext_ref[l:l + 8, :]

    act = conv * _sigmoid(conv)
    xs = act[:, 0:D_MODEL]
    bm = act[:, D_MODEL:D_MODEL + 256]
    cm = act[:, D_MODEL + 256:CONV_DIM]

    dt = _softplus(dt_ref[...] + dtb_ref[...])
    if n_valid < l:
        dt = jnp.where(_iota((l, LANES), 0) < n_valid, dt, 0.0)
    ad = dt * (-jnp.exp(alog_ref[...]))
    row = _iota((l, l), 0)
    col = _iota((l, l), 1)
    causal = row >= col
    cs = _dot_sel3(jnp.where(causal, 1.0, 0.0).astype(BF16), ad)
    eye = jnp.where(_iota((LANES, LANES), 0) == _iota((LANES, LANES), 1), 1.0, 0.0).astype(BF16)
    cs_parts = _split3(cs)
    cs_t = sum(_dot_nt(eye, part) for part in cs_parts)
    expand = expand_ref[...]
    dt_e = sum(_dot(part, expand) for part in _split3(dt))
    cs_e = sum(_dot(part, expand) for part in cs_parts)
    xdt = xs * dt_e
    decay_in = jnp.exp(cs_e)
    decay_out = jnp.exp(cs_e[l - 1:l, :] - cs_e)
    xdt_dec = (xdt * decay_out).astype(BF16)
    lane_lo = _iota((l, LANES), 1) < SSD_HEADDIM
    row_lo = _iota((LANES, 1), 0) < SSD_HEADDIM

    ys = []
    for g in range(2):
        bg = bm[:, g * LANES:(g + 1) * LANES].astype(BF16)
        cg = cm[:, g * LANES:(g + 1) * LANES].astype(BF16)
        cb = _dot_nt(cg, bg)
        for pr in range(4 * g, 4 * g + 4):
            sl = slice(pr * LANES, (pr + 1) * LANES)
            h0, h1 = 2 * pr, 2 * pr + 1
            m0 = jnp.where(causal, jnp.exp(cs[:, h0:h0 + 1] - cs_t[h0:h0 + 1, :]), 0.0) * cb
            m1 = jnp.where(causal, jnp.exp(cs[:, h1:h1 + 1] - cs_t[h1:h1 + 1, :]), 0.0) * cb
            xp = xdt[:, sl]
            x_lo = jnp.where(lane_lo, xp, 0.0).astype(BF16)
            x_hi = jnp.where(lane_lo, 0.0, xp).astype(BF16)
            y_diag = _dot(m0.astype(BF16), x_lo) + _dot(m1.astype(BF16), x_hi)
            s_pair = state_ref[pr]
            y_off = _dot_nt(cg, s_pair.astype(BF16)) * decay_in[:, sl]
            ys.append(y_diag + y_off + dskip_ref[:, sl] * xs[:, sl])
            upd = _dot_tn(xdt_dec[:, sl], bg)
            last = cs[l - 1:l, :]
            row_decay = jnp.exp(jnp.where(row_lo, last[:, h0:h0 + 1], last[:, h1:h1 + 1]))
            state_ref[pr] = row_decay * s_pair + upd
    y = jnp.concatenate(ys, axis=1)
    z = z_ref[...]
    y_ref[...] = _rmsnorm(y * (z * _sigmoid(z)), normw_ref[...])


def _ssd(proj3, conv_prev8, ssm_prev, p, *, l, n_valid):
    b, lp, _ = proj3.shape
    nc = lp // l
    full = lambda shape: pl.BlockSpec(shape, lambda i, c: (0,) * len(shape))
    body = functools.partial(_ssd_body, l=l, n_valid=n_valid)
    return pl.pallas_call(
        body,
        grid=(b, nc),
        in_specs=[
            pl.BlockSpec((None, l, 1024), lambda i, c: (i, c, COL_Z)),
            pl.BlockSpec((None, l, 1024), lambda i, c: (i, c, COL_XS)),
            pl.BlockSpec((None, l, 512), lambda i, c: (i, c, COL_BC)),
            pl.BlockSpec((None, l, 128), lambda i, c: (i, c, COL_DT)),
            pl.BlockSpec((None, 8, CONV_DIM), lambda i, c: (i, 0, 0)),
            pl.BlockSpec((None, 8, 128, 128), lambda i, c: (i, 0, 0, 0)),
            full((CONV_K, CONV_DIM)), full((1, CONV_DIM)), full((1, 128)), full((1, 128)),
            full((1, 1024)), full((1, 1024)), full((128, 1024)),
        ],
        out_specs=[
            pl.BlockSpec((None, l, 1024), lambda i, c: (i, c, 0)),
            pl.BlockSpec((None, 8, 128, 128), lambda i, c: (i, 0, 0, 0)),
        ],
        out_shape=[jax.ShapeDtypeStruct((b, lp, 1024), F32), jax.ShapeDtypeStruct((b, 8, 128, 128), F32)],
        scratch_shapes=[pltpu.VMEM((l + 8, CONV_DIM), F32)],
        compiler_params=_cparams(("parallel", "arbitrary")),
        name="ssd",
    )(proj3, proj3, proj3, proj3, conv_prev8, ssm_prev, p["conv_w"], p["conv_b"], p["dt_bias"], p["a_log"],
      p["d_skip_e"], p["ssd_norm_w"], p["expand"])


def _dot_w(x, w_ref):
    xh, xl = _split_bf16(x)
    return _dot(xh, w_ref[0]) + _dot(xl, w_ref[0]) + _dot(xh, w_ref[1])


def _split3(x):
    x1 = x.astype(BF16)
    r1 = x - x1.astype(F32)
    x2 = r1.astype(BF16)
    x3 = (r1 - x2.astype(F32)).astype(BF16)
    return x1, x2, x3


def _dot_sel3(sel_bf16, x):
    return sum(_dot(sel_bf16, part) for part in _split3(x))


def _head_sum(x, seg_pair):
    rows = x.shape[0]
    n = x.shape[1] // LANES
    stacked = jnp.concatenate([x[:, i * LANES:(i + 1) * LANES] for i in range(n)], axis=0)
    s = _dot_onehot(stacked, seg_pair)
    return jnp.concatenate([s[i * rows:(i + 1) * rows] for i in range(n)], axis=1)


def _rwkv_body(r_ref, k_ref, v_ref, lr_ref, shprev_ref, wkvprev_ref, mu_ref, w0_ref, w2_ref, a0_ref, a2_ref,
               g2_ref, kk_ref, ka_ref, rk_ref, lnw_ref, lnb_ref, seg_ref, y_ref, wkv_ref, ext_ref, state_ref,
               *, C, n_valid):
    c = pl.program_id(1)
    same_head = (_iota((LANES, LANES), 0) < HD_RWKV) == (_iota((LANES, LANES), 1) < HD_RWKV)

    @pl.when(c == 0)
    def _():
        ext_ref[0:8, :] = shprev_ref[...]
        for p in range(8):
            s_c = wkvprev_ref[2 * p:2 * p + 2].reshape(LANES, HD_RWKV)
            state_ref[p] = jnp.where(same_head, jnp.concatenate([s_c, s_c], axis=1), 0.0)

    ext_ref[8:8 + C, 0:1024] = r_ref[...]
    ext_ref[8:8 + C, 1024:2048] = k_ref[...]
    ext_ref[8:8 + C, 2048:3072] = v_ref[...]
    ext_ref[8:8 + C, 3072:3328] = lr_ref[...]
    cur = ext_ref[8:8 + C, :]
    prev = ext_ref[pl.ds(7, C), :]
    mixed = cur + (prev - cur) * mu_ref[...]
    ext_ref[0:8, :] = ext_ref[C:C + 8, :]

    r = mixed[:, 0:1024]
    k = mixed[:, 1024:2048]
    v = mixed[:, 2048:3072]
    wa = mixed[:, 3072:3200]
    gl = mixed[:, 3200:3328]
    seg = seg_ref[...]

    wlog = -_softplus(-(w0_ref[...] + _dot_w(jnp.tanh(wa), w2_ref))) - 0.5
    lw = -jnp.exp(wlog)
    a_sig = _sigmoid(a0_ref[...] + _dot_w(wa, a2_ref))
    g = _dot_w(_sigmoid(gl), g2_ref)
    kk = k * kk_ref[...]
    kk = kk * lax.rsqrt(jnp.maximum(_head_sum(kk * kk, seg), 1e-24))
    kh = k * (1.0 + (a_sig - 1.0) * ka_ref[...])
    av = -kk
    bv = kk * a_sig
    bonus = _head_sum(r * kh * rk_ref[...], seg) * v
    if n_valid < C:
        valid = _iota((C, 1024), 0) < n_valid
        lw = jnp.where(valid, lw, 0.0)
        av = jnp.where(valid, av, 0.0)
        bv = jnp.where(valid, bv, 0.0)
        kh = jnp.where(valid, kh, 0.0)
        v = jnp.where(valid, v, 0.0)

    causal_c = jnp.where(_iota((C, C), 0) >= _iota((C, C), 1), 1.0, 0.0).astype(BF16)
    cl = _dot_sel3(causal_c, lw)
    e_pos = jnp.exp(cl)
    e_neg = jnp.exp(-cl)
    rt = r * e_pos
    at = av * jnp.exp(cl - lw)
    bt = bv * e_neg
    kt = kh * e_neg
    p_end = e_pos[C - 1:C, :]

    R2 = 2 * C
    lane_lo = _iota((C, LANES), 1) < HD_RWKV
    rr = _iota((R2, R2), 0)
    cc = _iota((R2, R2), 1)
    strict = rr > cc
    incl = rr >= cc
    eye = (rr == cc).astype(F32)

    def stack(x):
        return jnp.concatenate([jnp.where(lane_lo, x, 0.0), jnp.where(lane_lo, 0.0, x)], axis=0)

    pairs = range(8)
    sls = [slice(p * LANES, (p + 1) * LANES) for p in pairs]
    a_s = [stack(at[:, sl]) for sl in sls]
    r_s = [stack(rt[:, sl]) for sl in sls]
    b_s = [stack(bt[:, sl]) for sl in sls]
    k_s = [stack(kt[:, sl]) for sl in sls]
    v_s = [stack(v[:, sl]) for sl in sls]
    s_bd = [state_ref[p] for p in pairs]
    wide = R2 == LANES
    if wide:
        ar = [jnp.concatenate([a_s[p], r_s[p]], axis=0) for p in pairs]
        bk = [jnp.concatenate([b_s[p], k_s[p]], axis=0) for p in pairs]
        quad = [_cmm(ar[p], bk[p], NT) for p in pairs]
        ars = [_cmm(ar[p], s_bd[p], NT) for p in pairs]
        m_ab = [quad[p][:R2, :R2] for p in pairs]
        m_ak = [jnp.where(strict, quad[p][:R2, R2:], 0.0) for p in pairs]
        w = [ars[p][:R2] + _cmm(m_ak[p], v_s[p], NN) for p in pairs]
    else:
        m_ab = [_cmm(a_s[p], b_s[p], NT) for p in pairs] if n_valid > 1 else None
        m_ak = [jnp.where(strict, _cmm(a_s[p], k_s[p], NT), 0.0) for p in pairs]
        w = [_cmm(a_s[p], s_bd[p], NT) + _cmm(m_ak[p], v_s[p], NN) for p in pairs]
    if n_valid > 1:
        x = [jnp.where(strict, m_ab[p], 0.0) for p in pairs]
        t = [eye + x[p] for p in pairs]
        span = 2
        while span < C:
            x = [_cmm(x[p], x[p], NN) for p in pairs]
            t = [t[p] + _cmm(t[p], x[p], NN) for p in pairs]
            span *= 2
        u = [_cmm(t[p], w[p], NN) for p in pairs]
    else:
        u = w
    ys = []
    for p in pairs:
        if wide:
            n_r = jnp.where(jnp.concatenate([incl, incl], axis=1), quad[p][R2:, :], 0.0)
            uv = jnp.concatenate([u[p], v_s[p]], axis=0)
            y2 = ars[p][R2:] + _cmm(n_r, uv, NN)
            upd = _cmm(uv, bk[p], TN)
        else:
            n_rb = jnp.where(incl, _cmm(r_s[p], b_s[p], NT), 0.0)
            n_rk = jnp.where(incl, _cmm(r_s[p], k_s[p], NT), 0.0)
            y2 = _cmm(r_s[p], s_bd[p], NT) + _cmm(n_rb, u[p], NN) + _cmm(n_rk, v_s[p], NN)
            upd = _cmm(u[p], b_s[p], TN) + _cmm(v_s[p], k_s[p], TN)
        ys.append(y2[0:C] + y2[C:R2])
        state_ref[p] = (s_bd[p] + upd) * p_end[:, sls[p]]
    yr = jnp.concatenate(ys, axis=1)
    mean = _head_sum(yr, seg) * (1.0 / HD_RWKV)
    d = yr - mean
    var = _head_sum(d * d, seg) * (1.0 / HD_RWKV)
    yn = d * lax.rsqrt(var + RWKV_LN_EPS) * lnw_ref[...] + lnb_ref[...]
    y_ref[...] = (yn + bonus) * g

    @pl.when(c == pl.num_programs(1) - 1)
    def _():
        for p in range(8):
            s_p = state_ref[p]
            wkv_ref[2 * p] = s_p[0:HD_RWKV, 0:HD_RWKV]
            wkv_ref[2 * p + 1] = s_p[HD_RWKV:LANES, HD_RWKV:LANES]


def _rwkv(proj3, shift_prev8, wkv_prev, p, *, C, n_valid):
    b, lp, _ = proj3.shape
    nc = lp // C
    full = lambda shape: pl.BlockSpec(shape, lambda i, c: (0,) * len(shape))
    body = functools.partial(_rwkv_body, C=C, n_valid=n_valid)
    return pl.pallas_call(
        body,
        grid=(b, nc),
        in_specs=[
            pl.BlockSpec((None, C, 1024), lambda i, c: (i, c, COL_R)),
            pl.BlockSpec((None, C, 1024), lambda i, c: (i, c, COL_K)),
            pl.BlockSpec((None, C, 1024), lambda i, c: (i, c, COL_V)),
            pl.BlockSpec((None, C, 256), lambda i, c: (i, c, COL_LR)),
            pl.BlockSpec((None, 8, RWKV_PROJ), lambda i, c: (i, 0, 0)),
            pl.BlockSpec((None, H_RWKV, HD_RWKV, HD_RWKV), lambda i, c: (i, 0, 0, 0)),
            full((1, RWKV_PROJ)), full((1, 1024)), full((2, 128, 1024)), full((1, 1024)), full((2, 128, 1024)),
            full((2, 128, 1024)), full((1, 1024)), full((1, 1024)), full((1, 1024)), full((1, 1024)), full((1, 1024)),
            full((128, 128)),
        ],
        out_specs=[
            pl.BlockSpec((None, C, 1024), lambda i, c: (i, c, 0)),
            pl.BlockSpec((None, H_RWKV, HD_RWKV, HD_RWKV), lambda i, c: (i, 0, 0, 0)),
        ],
        out_shape=[jax.ShapeDtypeStruct((b, lp, 1024), F32), jax.ShapeDtypeStruct((b, H_RWKV, HD_RWKV, HD_RWKV), F32)],
        scratch_shapes=[pltpu.VMEM((C + 8, RWKV_PROJ), F32), pltpu.VMEM((8, 128, 128), F32)],
        compiler_params=_cparams(("parallel", "arbitrary")),
        name="rwkv",
    )(proj3, proj3, proj3, proj3, shift_prev8, wkv_prev, p["rwkv_mu"], p["rwkv_w0"], p["rwkv_w2p"], p["rwkv_a0"],
      p["rwkv_a2p"], p["rwkv_g2"], p["rwkv_k_k"], p["rwkv_k_a"], p["rwkv_r_k"], p["rwkv_ln_w"], p["rwkv_ln_b"],
      p["seg"])


def _merge_body(x_ref, gs_ref, gr_ref, ys_ref, yr_ref, w_ref, o_ref):
    merged = _sigmoid(gs_ref[...]) * ys_ref[...] + _sigmoid(gr_ref[...]) * yr_ref[...]
    o_ref[...] = x_ref[...] + _dot(merged.astype(BF16), w_ref[...])


def _merge_out(x, proj, y_ssd, y_rwkv, w_out, *, tm=512):
    m = x.shape[0]
    tm = min(tm, m)
    row = lambda j: pl.BlockSpec((tm, 1024), lambda i: (i, j))
    return pl.pallas_call(
        _merge_body,
        grid=(m // tm,),
        in_specs=[row(0), row(COL_GS), row(COL_GR), row(0), row(0), pl.BlockSpec((1024, 1024), lambda i: (0, 0))],
        out_specs=row(0),
        out_shape=jax.ShapeDtypeStruct((m, 1024), F32),
        compiler_params=_cparams(("parallel",)),
        name="merge_out",
    )(x, proj, proj, y_ssd, y_rwkv, w_out.astype(BF16))


def _mem_attention(q, k_ref, v_ref):
    outs = []
    for h in range(MEM_HEADS):
        sl = slice(h * MEM_HD, (h + 1) * MEM_HD)
        s = _dot_nt(q[:, sl].astype(BF16), k_ref[:, sl].astype(BF16)) * (MEM_HD ** -0.5)
        s = s - jnp.max(s, axis=-1, keepdims=True)
        e = jnp.exp(s)
        prob = e / jnp.sum(e, axis=-1, keepdims=True)
        outs.append(_dot(prob.astype(BF16), v_ref[:, sl].astype(BF16)))
    return jnp.concatenate(outs, axis=1)


def _tail_body(x_ref, gs_ref, gr_ref, ys_ref, yr_ref, k_ref, v_ref, wout_ref, nmem_ref, wmq_ref, wmo_ref,
               nffn_ref, wpq_ref, x2_ref, pq_ref):
    merged = _sigmoid(gs_ref[...]) * ys_ref[...] + _sigmoid(gr_ref[...]) * yr_ref[...]
    x1 = x_ref[...] + _dot(merged.astype(BF16), wout_ref[...])
    q = _dot(_rmsnorm(x1, nmem_ref[...]).astype(BF16), wmq_ref[...])
    x2 = x1 + _dot(_mem_attention(q, k_ref, v_ref).astype(BF16), wmo_ref[...])
    x2_ref[...] = x2
    pq_ref[...] = _dot(_rmsnorm(x2, nffn_ref[...]).astype(BF16), wpq_ref[...])


def _tail(x3, proj3, y_ssd, y_rwkv, mem_k, mem_v, p, *, tm=256):
    b, L, _ = x3.shape
    once = pl.Buffered(1)
    tile = lambda j: pl.BlockSpec((None, tm, 1024), lambda i, t: (i, t, j))
    mem = pl.BlockSpec((None, N_MEM, 1024), lambda i, t: (i, 0, 0))
    full = lambda r, c: pl.BlockSpec((r, c), lambda i, t: (0, 0), pipeline_mode=once)
    return pl.pallas_call(
        _tail_body,
        grid=(b, L // tm),
        in_specs=[tile(0), tile(COL_GS), tile(COL_GR), tile(0), tile(0), mem, mem,
                  full(1024, 1024), full(1, 1024), full(1024, 1024), full(1024, 1024), full(1, 1024),
                  full(1024, 2048)],
        out_specs=[tile(0), pl.BlockSpec((None, tm, 2048), lambda i, t: (i, t, 0))],
        out_shape=[jax.ShapeDtypeStruct((b, L, 1024), F32), jax.ShapeDtypeStruct((b, L, 2048), F32)],
        compiler_params=_cparams(("parallel", "arbitrary")),
        name="token_tail",
    )(x3, proj3, proj3, y_ssd, y_rwkv, mem_k, mem_v, p["w_out"].astype(BF16), p["norm_mem_w"].reshape(1, 1024),
      p["w_mq"].astype(BF16), p["w_mo"].astype(BF16), p["norm_ffn_w"].reshape(1, 1024), p["w_pq"].astype(BF16))


def _attn_step_body(q_ref, k_ref, v_ref, o_ref):
    s = jnp.sum(k_ref[...] * q_ref[...][None], axis=-1, keepdims=True) * (MEM_HD ** -0.5)
    e = jnp.exp(s - jnp.max(s, axis=0, keepdims=True))
    prob = e / jnp.sum(e, axis=0, keepdims=True)
    o_ref[...] = jnp.sum(prob * v_ref[...], axis=0)


def _attn_step(q, mem_k, mem_v):
    b = q.shape[0]
    qspec = pl.BlockSpec((None, MEM_HEADS, MEM_HD), lambda i: (i, 0, 0))
    kvspec = pl.BlockSpec((None, N_MEM, MEM_HEADS, MEM_HD), lambda i: (i, 0, 0, 0))
    return pl.pallas_call(
        _attn_step_body,
        grid=(b,),
        in_specs=[qspec, kvspec, kvspec],
        out_specs=qspec,
        out_shape=jax.ShapeDtypeStruct((b, MEM_HEADS, MEM_HD), F32),
        compiler_params=_cparams(("parallel",)),
        name="mem_attn_step",
    )(q, mem_k, mem_v)


def _stage2_pairs():
    return [(r1, r2) for r1 in range(PEER_TOPK) for r2 in range(PEER_TOPK) if (r1 + 1) * (r2 + 1) <= PEER_TOPK]


N_CAND = 56


def _topk_body(q_ref, keys_ref, gate_ref, i1_ref, i2_ref, s_scr, i_scr, cand_scr, code_scr, res_scr, *, T):
    neg = -jnp.inf
    key_row = _iota((N_KEYS, T), 0).astype(F32)
    cand_row = _iota((N_CAND, T), 0).astype(F32)
    pairs = _stage2_pairs()
    q = q_ref[...]
    for h in range(PEER_HEADS):
        for c in range(2):
            hc = 2 * h + c
            s = _dot_nt(keys_ref[hc], q[:, hc * LANES:(hc + 1) * LANES], HI)
            for r in range(PEER_TOPK):
                m = jnp.max(s, axis=0, keepdims=True)
                idx = jnp.min(jnp.where(s == m, key_row, float(N_KEYS)), axis=0, keepdims=True)
                s_scr[c, r:r + 1, :] = m
                i_scr[c, r:r + 1, :] = idx
                s = jnp.where(key_row == idx, neg, s)
        cand_scr[...] = jnp.full((N_CAND, T), neg, F32)
        code_scr[...] = jnp.zeros((N_CAND, T), F32)
        for j, (r1, r2) in enumerate(pairs):
            cand_scr[j:j + 1, :] = s_scr[0, r1:r1 + 1, :] + s_scr[1, r2:r2 + 1, :]
            code_scr[j:j + 1, :] = i_scr[0, r1:r1 + 1, :] * float(N_KEYS) + i_scr[1, r2:r2 + 1, :]
        cand = cand_scr[...]
        code = code_scr[...]
        for r in range(PEER_TOPK):
            m = jnp.max(cand, axis=0, keepdims=True)
            idx = jnp.min(jnp.where(cand == m, cand_row, float(N_CAND)), axis=0, keepdims=True)
            hit = cand_row == idx
            res_scr[0, r:r + 1, :] = m
            res_scr[1, r:r + 1, :] = jnp.sum(jnp.where(hit, code, 0.0), axis=0, keepdims=True)
            cand = jnp.where(hit, neg, cand)
        top = res_scr[0]
        e = jnp.exp(top - jnp.max(top, axis=0, keepdims=True))
        rows = slice(h * PEER_TOPK, (h + 1) * PEER_TOPK)
        gate_ref[rows, :] = e / jnp.sum(e, axis=0, keepdims=True)
        first = jnp.floor(res_scr[1] * (1.0 / N_KEYS))
        i1_ref[rows, :] = first.astype(jnp.int32)
        i2_ref[rows, :] = (res_scr[1] - first * float(N_KEYS)).astype(jnp.int32)


def _peer_topk(q, sub_keys, *, T=256):
    m = q.shape[0]
    T = min(T, m)
    body = functools.partial(_topk_body, T=T)
    out = pl.BlockSpec((128, T), lambda i: (0, i))
    return pl.pallas_call(
        body,
        grid=(m // T,),
        in_specs=[pl.BlockSpec((T, 2048), lambda i: (i, 0)), pl.BlockSpec((16, 128, 128), lambda i: (0, 0, 0))],
        out_specs=[out, out, out],
        out_shape=[jax.ShapeDtypeStruct((128, m), F32), jax.ShapeDtypeStruct((128, m), jnp.int32),
                   jax.ShapeDtypeStruct((128, m), jnp.int32)],
        scratch_shapes=[pltpu.VMEM((2, PEER_TOPK, T), F32), pltpu.VMEM((2, PEER_TOPK, T), F32),
                        pltpu.VMEM((N_CAND, T), F32), pltpu.VMEM((N_CAND, T), F32),
                        pltpu.VMEM((2, PEER_TOPK, T), F32)],
        compiler_params=_cparams(("parallel",)),
        name="peer_topk",
    )(q, sub_keys.reshape(16, 128, 128))


I1_PER_STEP = 8
G_PITCH = 136


def _peer_body(x_ref, gate_ref, i1_ref, i2_ref, nw_ref, fw_ref, u_ref, v_ref, o_ref, hn_scr, g_scr, coef_scr, *, T):
    j = pl.program_id(1)

    @pl.when(j == 0)
    def _():
        hn_scr[...] = _rmsnorm(x_ref[...], nw_ref[...]).astype(BF16)
        o_ref[...] = jnp.zeros_like(o_ref)
        key_row = _iota((N_KEYS, LANES), 0)

        def build(t, carry):
            g_row = gate_ref[pl.ds(t, 1), :]
            a_t = jnp.where(key_row == i1_ref[pl.ds(t, 1), :], g_row, 0.0).astype(BF16)
            b_t = jnp.where(key_row == i2_ref[pl.ds(t, 1), :], 1.0, 0.0).astype(BF16)
            g_scr[pl.ds(pl.multiple_of(t * G_PITCH, 8), N_KEYS), :] = _dot_nt(a_t, b_t)
            return carry

        lax.fori_loop(0, T, build, 0, unroll=32)

    hn = hn_scr[...]
    for cidx in range(I1_PER_STEP // 2):
        act = _dot_nt(hn, u_ref[cidx * 2 * LANES:(cidx + 1) * 2 * LANES, :])
        ge = 0.5 * act * (1.0 + lax.erf(act * (2.0 ** -0.5)))
        for half in range(2):
            ii = 2 * cidx + half
            g_t = g_scr[pl.ds(j * I1_PER_STEP + ii, T, stride=G_PITCH), :]
            coef_scr[:, ii * LANES:(ii + 1) * LANES] = (g_t * ge[:, half * LANES:(half + 1) * LANES]).astype(BF16)
    o_ref[...] += _dot(coef_scr[...], v_ref[...])

    @pl.when(j == pl.num_programs(1) - 1)
    def _():
        o_ref[...] = _rmsnorm(x_ref[...] + o_ref[...], fw_ref[...])


def _peer_mix(x, gate, i1, i2, norm_w, final_w, u_bf16, v_bf16, *, T=512):
    m = x.shape[0]
    T = min(T, m)
    ne = I1_PER_STEP * N_KEYS
    body = functools.partial(_peer_body, T=T)
    once = pl.Buffered(1)
    row = lambda w: pl.BlockSpec((T, w), lambda i, j: (i, 0), pipeline_mode=once)
    vec = pl.BlockSpec((1, 1024), lambda i, j: (0, 0), pipeline_mode=once)
    return pl.pallas_call(
        body,
        grid=(m // T, N_KEYS // I1_PER_STEP),
        in_specs=[row(1024), row(128), row(128), row(128), vec, vec,
                  pl.BlockSpec((ne, 1024), lambda i, j: (j, 0)), pl.BlockSpec((ne, 1024), lambda i, j: (j, 0))],
        out_specs=pl.BlockSpec((T, 1024), lambda i, j: (i, 0)),
        out_shape=jax.ShapeDtypeStruct((m, 1024), F32),
        scratch_shapes=[pltpu.VMEM((T, 1024), BF16), pltpu.VMEM((T * G_PITCH, LANES), F32),
                        pltpu.VMEM((T, 1024), BF16)],
        compiler_params=_cparams(("parallel", "arbitrary")),
        name="peer_mix",
    )(x, gate, i1, i2, norm_w.reshape(1, 1024), final_w.reshape(1, 1024), u_bf16, v_bf16)


def _pad_rows(x, rows):
    return jnp.pad(x, ((0, 0), (rows - x.shape[1], 0), (0, 0)))


def _block(x, mem_k, mem_v, conv_prev, ssm_prev, shift_prev, wkv_prev, p, *, ssd_l, rwkv_c):
    b, L, _ = x.shape
    m = b * L
    x2d = x.reshape(m, D_MODEL)
    proj = _linear(x2d, p["w_in"], norm_w=p["norm_mix_w"], tm=1024, tn=1152, name="in_proj")
    proj3 = proj.reshape(b, L, D_IN_PAD)
    lp = max(L, SUBLANES)
    n_valid = L if L < SUBLANES else max(ssd_l, rwkv_c)
    proj3p = proj3 if lp == L else jnp.pad(proj3, ((0, 0), (0, lp - L), (0, 0)))

    y_ssd, ssm_new = _ssd(proj3p, _pad_rows(conv_prev, 8), ssm_prev.reshape(b, 8, 128, 128), p,
                          l=ssd_l, n_valid=min(n_valid, ssd_l))
    y_rwkv, wkv_new = _rwkv(proj3p, _pad_rows(shift_prev[:, None, :], 8), wkv_prev, p,
                            C=rwkv_c, n_valid=min(n_valid, rwkv_c))
    tail = proj3[:, max(L - (CONV_K - 1), 0):]
    xbc = jnp.concatenate([tail[..., 1024:2048], tail[..., 7168:7680]], axis=-1)
    rw = jnp.concatenate([tail[..., 2048:5120], tail[..., 7680:7936]], axis=-1)
    conv_new = jnp.concatenate([conv_prev, xbc], axis=1)[:, -(CONV_K - 1):]
    shift_new = rw[:, -1]
    ssm_new = ssm_new.reshape(b, H_SSD, SSD_HEADDIM, SSD_STATE)

    if L == 1:
        x1 = _merge_out(x2d, proj, y_ssd[:, 0], y_rwkv[:, 0], p["w_out"])
        q = _linear(x1, p["w_mq"], norm_w=p["norm_mem_w"], name="mem_q")
        o = _attn_step(q.reshape(b, MEM_HEADS, MEM_HD), mem_k, mem_v)
        x2 = _linear(o.reshape(m, D_MODEL), p["w_mo"], res=x1, name="mem_o")
        pq = _linear(x2, p["w_pq"], norm_w=p["norm_ffn_w"], name="peer_q")
    else:
        x2, pq = _tail(x, proj3, y_ssd, y_rwkv, mem_k.reshape(b, N_MEM, D_MODEL), mem_v.reshape(b, N_MEM, D_MODEL), p)
        x2, pq = x2.reshape(m, D_MODEL), pq.reshape(m, 2 * D_MODEL)
    gate_t, i1_t, i2_t = _peer_topk(pq, p["sub_keys"])
    y = _peer_mix(x2, gate_t.T, i1_t.T, i2_t.T, p["norm_ffn_w"], p["norm_final_w"], p["expert_u"], p["expert_v"])
    return y.reshape(b, L, D_MODEL), conv_new, ssm_new, shift_new, wkv_new


def kernel(x_prompt, x_sample, mem_prompt, state_ssm, state_conv, state_wkv, state_shift, cache_mem_k, cache_mem_v, norm_mix_w, w_in, conv_w, conv_b, dt_bias, a_log, d_skip, ssd_norm_w, rwkv_mu, rwkv_w0, rwkv_w2, rwkv_a0, rwkv_a2, rwkv_g2, rwkv_k_k, rwkv_k_a, rwkv_r_k, rwkv_ln_w, rwkv_ln_b, w_out, norm_mem_w, mem_norm_w, w_mk, w_mv, w_mq, w_mo, norm_ffn_w, w_pq, sub_keys, expert_u, expert_v, norm_final_w):
    assert w_in.shape[0] == 1, "single-layer problem"
    bp = x_prompt.shape[0]
    bs = x_sample.shape[0]
    wi = w_in[0]
    w_in_perm = jnp.concatenate(
        [wi[:, 0:2048], wi[:, 2576:5648], wi[:, 5904:7952], wi[:, 2048:2560], wi[:, 5648:5904], wi[:, 2560:2576],
         jnp.zeros((D_MODEL, LANES - H_SSD), F32)], axis=1)
    pad16 = lambda a: jnp.pad(a.reshape(1, H_SSD), ((0, 0), (0, LANES - H_SSD)))
    head_of_lane = jnp.arange(D_MODEL) // SSD_HEADDIM
    zeros64 = jnp.zeros((64, D_MODEL), F32)
    hi_lo = lambda w: jnp.stack(_split_bf16(w))
    p = {
        "norm_mix_w": norm_mix_w[0], "w_in": w_in_perm,
        "conv_w": conv_w[0], "conv_b": conv_b[0].reshape(1, CONV_DIM),
        "dt_bias": pad16(dt_bias[0]), "a_log": pad16(a_log[0]),
        "d_skip_e": jnp.repeat(d_skip[0], SSD_HEADDIM).reshape(1, D_MODEL),
        "ssd_norm_w": ssd_norm_w[0].reshape(1, D_MODEL),
        "expand": (jnp.arange(LANES)[:, None] == head_of_lane[None, :]).astype(BF16),
        "rwkv_mu": rwkv_mu[0].reshape(1, RWKV_PROJ), "rwkv_w0": rwkv_w0[0].reshape(1, D_MODEL),
        "rwkv_w2p": hi_lo(jnp.concatenate([rwkv_w2[0], zeros64], axis=0)),
        "rwkv_a0": rwkv_a0[0].reshape(1, D_MODEL),
        "rwkv_a2p": hi_lo(jnp.concatenate([zeros64, rwkv_a2[0]], axis=0)),
        "rwkv_g2": hi_lo(rwkv_g2[0]), "rwkv_k_k": rwkv_k_k[0].reshape(1, D_MODEL),
        "rwkv_k_a": rwkv_k_a[0].reshape(1, D_MODEL), "rwkv_r_k": rwkv_r_k[0].reshape(1, D_MODEL),
        "rwkv_ln_w": rwkv_ln_w[0].reshape(1, D_MODEL), "rwkv_ln_b": rwkv_ln_b[0].reshape(1, D_MODEL),
        "seg": (head_of_lane[:LANES, None] == head_of_lane[None, :LANES]).astype(BF16),
        "w_out": w_out[0], "norm_mem_w": norm_mem_w[0], "w_mq": w_mq[0], "w_mo": w_mo[0],
        "norm_ffn_w": norm_ffn_w[0], "w_pq": w_pq[0], "sub_keys": sub_keys[0],
        "expert_u": expert_u[0].astype(BF16), "expert_v": expert_v[0].astype(BF16),
        "norm_final_w": norm_final_w,
    }

    mem2d = mem_prompt.reshape(bp * N_MEM, D_MODEL)
    mem_k = _linear(mem2d, w_mk[0], norm_w=mem_norm_w[0], name="mem_k").reshape(bp, N_MEM, MEM_HEADS, MEM_HD)
    mem_v = _linear(mem2d, w_mv[0], norm_w=mem_norm_w[0], name="mem_v").reshape(bp, N_MEM, MEM_HEADS, MEM_HD)
    y_p, conv_p, ssm_p, shift_p, wkv_p = _block(
        x_prompt, mem_k, mem_v,
        jnp.zeros((bp, CONV_K - 1, CONV_DIM), F32), jnp.zeros((bp, H_SSD, SSD_HEADDIM, SSD_STATE), F32),
        jnp.zeros((bp, RWKV_PROJ), F32), jnp.zeros((bp, H_RWKV, HD_RWKV, HD_RWKV), F32), p,
        ssd_l=SSD_CHUNK, rwkv_c=RWKV_CHUNK)
    y_s, conv_s, ssm_s, shift_s, wkv_s = _block(
        x_sample, cache_mem_k[0], cache_mem_v[0], state_conv[0], state_ssm[0], state_shift[0], state_wkv[0], p,
        ssd_l=SUBLANES, rwkv_c=SUBLANES)
    return (y_p, y_s, ssm_p[None], conv_p[None], wkv_p[None], shift_p[None], mem_k[None], mem_v[None],
            ssm_s[None], conv_s[None], wkv_s[None], shift_s[None])
```

```python
import functools

import jax
import jax.numpy as jnp
from jax import lax
from jax.experimental import pallas as pl
from jax.experimental.pallas import tpu as pltpu

F32 = jnp.float32
BF16 = jnp.bfloat16
HI = lax.Precision.HIGHEST

D_MODEL = 1024
H_SSD = 16
SSD_HEADDIM = 64
SSD_STATE = 128
CONV_K = 4
CONV_DIM = 1536
H_RWKV = 16
HD_RWKV = 64
RWKV_PROJ = 3328
RWKV_LN_EPS = 64e-5
N_MEM = 256
MEM_HEADS = 4
MEM_HD = 256
N_KEYS = 128
PEER_HEADS = 8
PEER_TOPK = 16
NORM_EPS = 1e-6

LANES = 128
SUBLANES = 8
VMEM_LIMIT = 56 * 1024 * 1024

D_IN_PAD = 8064
COL_Z, COL_XS, COL_R, COL_K, COL_V, COL_GS, COL_GR = 0, 1, 2, 3, 4, 5, 6
COL_BC = 14
COL_LR = 30
COL_DT = 62

SSD_CHUNK = 128
RWKV_CHUNK = 64
SEQS_PER_STEP = 8


def _cparams(sem):
    return pltpu.CompilerParams(dimension_semantics=sem, vmem_limit_bytes=VMEM_LIMIT)


def _dot(a, b, prec=None):
    return jnp.dot(a, b, preferred_element_type=F32, precision=prec)


def _dot_nt(a, b, prec=None):
    return lax.dot_general(a, b, (((1,), (1,)), ((), ())), preferred_element_type=F32, precision=prec)


def _dot_tn(a, b, prec=None):
    return lax.dot_general(a, b, (((0,), (0,)), ((), ())), preferred_element_type=F32, precision=prec)


def _split_bf16(x):
    hi = x.astype(BF16)
    lo = (x - hi.astype(F32)).astype(BF16)
    return hi, lo


def _dot_onehot(x, sel_bf16):
    hi, lo = _split_bf16(x)
    return _dot(hi, sel_bf16) + _dot(lo, sel_bf16)


NN = ((1,), (0,))
NT = ((1,), (1,))
TN = ((0,), (0,))


def _cmm(a, b, dims):
    return lax.dot_general(a.astype(BF16), b.astype(BF16), (dims, ((), ())), preferred_element_type=F32)


def _softplus(x):
    return jnp.maximum(x, 0.0) + jnp.log1p(jnp.exp(-jnp.abs(x)))


def _sigmoid(x):
    return 1.0 / (1.0 + jnp.exp(-x))


def _rmsnorm(x, w):
    return x * lax.rsqrt(jnp.mean(x * x, axis=-1, keepdims=True) + NORM_EPS) * w


def _iota(shape, dim):
    return lax.broadcasted_iota(jnp.int32, shape, dim)


def _per_sequence(nb, per_seq, init, main, fini=None):
    def wrapped(*refs):
        views = [[r.at[s] if flag else r for r, flag in zip(refs, per_seq)] for s in range(nb)]
        c = pl.program_id(1)

        @pl.when(c == 0)
        def _():
            for v in views:
                init(*v)

        live = [main(*v) for v in views]
        while live:
            live = [g for g in live if next(g, True) is None]
        if fini is not None:
            @pl.when(c == pl.num_programs(1) - 1)
            def _():
                for v in views:
                    fini(*v)
    return wrapped


def _linear_body(*refs, has_norm, has_res):
    it = iter(refs)
    x_ref = next(it)
    nw_ref = next(it) if has_norm else None
    w_ref = next(it)
    res_ref = next(it) if has_res else None
    o_ref = next(it)
    x = x_ref[...]
    if has_norm:
        x = _rmsnorm(x, nw_ref[...])
    acc = _dot(x.astype(BF16), w_ref[...])
    if has_res:
        acc = acc + res_ref[...]
    o_ref[...] = acc


def _linear(x, w, *, norm_w=None, res=None, tm=512, tn=1024, name="linear"):
    m, k = x.shape
    n = w.shape[1]
    tm = min(tm, m)
    tn = min(tn, n)
    assert m % tm == 0 and n % tn == 0
    args = [x]
    specs = [pl.BlockSpec((tm, k), lambda i, j: (i, 0))]
    if norm_w is not None:
        args.append(norm_w.reshape(1, k))
        specs.append(pl.BlockSpec((1, k), lambda i, j: (0, 0)))
    args.append(w.astype(BF16))
    specs.append(pl.BlockSpec((k, tn), lambda i, j: (0, j)))
    if res is not None:
        args.append(res)
        specs.append(pl.BlockSpec((tm, tn), lambda i, j: (i, j)))
    body = functools.partial(_linear_body, has_norm=norm_w is not None, has_res=res is not None)
    return pl.pallas_call(
        body,
        grid=(m // tm, n // tn),
        in_specs=specs,
        out_specs=pl.BlockSpec((tm, tn), lambda i, j: (i, j)),
        out_shape=jax.ShapeDtypeStruct((m, n), F32),
        compiler_params=_cparams(("parallel", "arbitrary")),
        name=name,
    )(*args)


def _ssd_init(z_ref, xs_ref, bc_ref, dt_ref, convprev_ref, ssmprev_ref, convw_ref, convb_ref, dtb_ref,
              alog_ref, dskip_ref, normw_ref, expand_ref, y_ref, state_ref, ext_ref):
    ext_ref[0:8, :] = convprev_ref[...]
    state_ref[...] = ssmprev_ref[...]


def _ssd_main(z_ref, xs_ref, bc_ref, dt_ref, convprev_ref, ssmprev_ref, convw_ref, convb_ref, dtb_ref,
              alog_ref, dskip_ref, normw_ref, expand_ref, y_ref, state_ref, ext_ref, *, l, n_valid):
    ext_ref[8:8 + l, 0:D_MODEL] = xs_ref[...]
    ext_ref[8:8 + l, D_MODEL:CONV_DIM] = bc_ref[...]
    conv = convb_ref[...]
    for i in range(CONV_K):
        conv = conv + ext_ref[pl.ds(5 + i, l), :] * convw_ref[i:i + 1, :]
    ext_ref[0:8, :] = ext_ref[l:l + 8, :]

    yield
    act = conv * _sigmoid(conv)
    xs = act[:, 0:D_MODEL]
    bm = act[:, D_MODEL:D_MODEL + 256]
    cm = act[:, D_MODEL + 256:CONV_DIM]

    dt = _softplus(dt_ref[...] + dtb_ref[...])
    if n_valid < l:
        dt = jnp.where(_iota((l, LANES), 0) < n_valid, dt, 0.0)
    ad = dt * (-jnp.exp(alog_ref[...]))
    row = _iota((l, l), 0)
    col = _iota((l, l), 1)
    causal = row >= col
    cs = _dot_sel3(jnp.where(causal, 1.0, 0.0).astype(BF16), ad)
    eye = jnp.where(_iota((LANES, LANES), 0) == _iota((LANES, LANES), 1), 1.0, 0.0).astype(BF16)
    cs_parts = _split3(cs)
    cs_t = sum(_dot_nt(eye, part) for part in cs_parts)
    expand = expand_ref[...]
    dt_e = sum(_dot(part, expand) for part in _split3(dt))
    cs_e = sum(_dot(part, expand) for part in cs_parts)
    yield
    xdt = xs * dt_e
    decay_in = jnp.exp(cs_e)
    decay_out = jnp.exp(cs_e[l - 1:l, :] - cs_e)
    xdt_dec = (xdt * decay_out).astype(BF16)
    lane_lo = _iota((l, LANES), 1) < SSD_HEADDIM
    row_lo = _iota((LANES, 1), 0) < SSD_HEADDIM

    ys = []
    for g in range(2):
        bg = bm[:, g * LANES:(g + 1) * LANES].astype(BF16)
        cg = cm[:, g * LANES:(g + 1) * LANES].astype(BF16)
        cb = _dot_nt(cg, bg)
        for pr in range(4 * g, 4 * g + 4):
            sl = slice(pr * LANES, (pr + 1) * LANES)
            h0, h1 = 2 * pr, 2 * pr + 1
            m0 = jnp.where(causal, jnp.exp(cs[:, h0:h0 + 1] - cs_t[h0:h0 + 1, :]), 0.0) * cb
            m1 = jnp.where(causal, jnp.exp(cs[:, h1:h1 + 1] - cs_t[h1:h1 + 1, :]), 0.0) * cb
            xp = xdt[:, sl]
            x_lo = jnp.where(lane_lo, xp, 0.0).astype(BF16)
            x_hi = jnp.where(lane_lo, 0.0, xp).astype(BF16)
            y_diag = _dot(m0.astype(BF16), x_lo) + _dot(m1.astype(BF16), x_hi)
            s_pair = state_ref[2 * pr:2 * pr + 2].reshape(LANES, SSD_STATE)
            y_off = _dot_nt(cg, s_pair.astype(BF16)) * decay_in[:, sl]
            ys.append(y_diag + y_off + dskip_ref[:, sl] * xs[:, sl])
            upd = _dot_tn(xdt_dec[:, sl], bg)
            last = cs[l - 1:l, :]
            row_decay = jnp.exp(jnp.where(row_lo, last[:, h0:h0 + 1], last[:, h1:h1 + 1]))
            state_ref[2 * pr:2 * pr + 2] = (row_decay * s_pair + upd).reshape(2, SSD_HEADDIM, SSD_STATE)
            yield
    y = jnp.concatenate(ys, axis=1)
    z = z_ref[...]
    y_ref[...] = _rmsnorm(y * (z * _sigmoid(z)), normw_ref[...])


def _ssd(proj3, conv_prev8, ssm_prev, p, *, l, n_valid, nb):
    b, lp, _ = proj3.shape
    nc = lp // l
    full = lambda shape: pl.BlockSpec(shape, lambda i, c: (0,) * len(shape))
    state = pl.BlockSpec((nb, H_SSD, SSD_HEADDIM, SSD_STATE), lambda i, c: (i, 0, 0, 0))
    body = _per_sequence(nb, [True] * 6 + [False] * 7 + [True] * 3, _ssd_init,
                         functools.partial(_ssd_main, l=l, n_valid=n_valid))
    return pl.pallas_call(
        body,
        grid=(b // nb, nc),
        in_specs=[
            pl.BlockSpec((nb, l, 1024), lambda i, c: (i, c, COL_Z)),
            pl.BlockSpec((nb, l, 1024), lambda i, c: (i, c, COL_XS)),
            pl.BlockSpec((nb, l, 512), lambda i, c: (i, c, COL_BC)),
            pl.BlockSpec((nb, l, 128), lambda i, c: (i, c, COL_DT)),
            pl.BlockSpec((nb, 8, CONV_DIM), lambda i, c: (i, 0, 0)),
            state,
            full((CONV_K, CONV_DIM)), full((1, CONV_DIM)), full((1, 128)), full((1, 128)),
            full((1, 1024)), full((1, 1024)), full((128, 1024)),
        ],
        out_specs=[pl.BlockSpec((nb, l, 1024), lambda i, c: (i, c, 0)), state],
        out_shape=[jax.ShapeDtypeStruct((b, lp, 1024), F32),
                   jax.ShapeDtypeStruct((b, H_SSD, SSD_HEADDIM, SSD_STATE), F32)],
        scratch_shapes=[pltpu.VMEM((nb, l + 8, CONV_DIM), F32)],
        compiler_params=_cparams(("parallel", "arbitrary")),
        name="ssd",
    )(proj3, proj3, proj3, proj3, conv_prev8, ssm_prev, p["conv_w"], p["conv_b"], p["dt_bias"], p["a_log"],
      p["d_skip_e"], p["ssd_norm_w"], p["expand"])


def _dot_w(x, w_ref):
    xh, xl = _split_bf16(x)
    return _dot(xh, w_ref[0]) + _dot(xl, w_ref[0]) + _dot(xh, w_ref[1])


def _split3(x):
    x1 = x.astype(BF16)
    r1 = x - x1.astype(F32)
    x2 = r1.astype(BF16)
    x3 = (r1 - x2.astype(F32)).astype(BF16)
    return x1, x2, x3


def _dot_sel3(sel_bf16, x):
    return sum(_dot(sel_bf16, part) for part in _split3(x))


def _head_sum(x, seg_pair):
    rows = x.shape[0]
    n = x.shape[1] // LANES
    stacked = jnp.concatenate([x[:, i * LANES:(i + 1) * LANES] for i in range(n)], axis=0)
    s = _dot_onehot(stacked, seg_pair)
    return jnp.concatenate([s[i * rows:(i + 1) * rows] for i in range(n)], axis=1)


def _rwkv_init(r_ref, k_ref, v_ref, lr_ref, shprev_ref, wkvprev_ref, mu_ref, w0_ref, w2_ref, a0_ref, a2_ref,
               g2_ref, kk_ref, ka_ref, rk_ref, lnw_ref, lnb_ref, seg_ref, y_ref, wkv_ref, ext_ref, state_ref):
    same_head = (_iota((LANES, LANES), 0) < HD_RWKV) == (_iota((LANES, LANES), 1) < HD_RWKV)
    ext_ref[0:8, :] = shprev_ref[...]
    for p in range(8):
        s_c = wkvprev_ref[2 * p:2 * p + 2].reshape(LANES, HD_RWKV)
        state_ref[p] = jnp.where(same_head, jnp.concatenate([s_c, s_c], axis=1), 0.0)


def _rwkv_fini(r_ref, k_ref, v_ref, lr_ref, shprev_ref, wkvprev_ref, mu_ref, w0_ref, w2_ref, a0_ref, a2_ref,
               g2_ref, kk_ref, ka_ref, rk_ref, lnw_ref, lnb_ref, seg_ref, y_ref, wkv_ref, ext_ref, state_ref):
    for p in range(8):
        s_p = state_ref[p]
        wkv_ref[2 * p] = s_p[0:HD_RWKV, 0:HD_RWKV]
        wkv_ref[2 * p + 1] = s_p[HD_RWKV:LANES, HD_RWKV:LANES]


def _rwkv_main(r_ref, k_ref, v_ref, lr_ref, shprev_ref, wkvprev_ref, mu_ref, w0_ref, w2_ref, a0_ref, a2_ref,
               g2_ref, kk_ref, ka_ref, rk_ref, lnw_ref, lnb_ref, seg_ref, y_ref, wkv_ref, ext_ref, state_ref,
               *, C, n_valid):
    ext_ref[8:8 + C, 0:1024] = r_ref[...]
    ext_ref[8:8 + C, 1024:2048] = k_ref[...]
    ext_ref[8:8 + C, 2048:3072] = v_ref[...]
    ext_ref[8:8 + C, 3072:3328] = lr_ref[...]
    cur = ext_ref[8:8 + C, :]
    prev = ext_ref[pl.ds(7, C), :]
    mixed = cur + (prev - cur) * mu_ref[...]
    ext_ref[0:8, :] = ext_ref[C:C + 8, :]

    yield
    r = mixed[:, 0:1024]
    k = mixed[:, 1024:2048]
    v = mixed[:, 2048:3072]
    wa = mixed[:, 3072:3200]
    gl = mixed[:, 3200:3328]
    seg = seg_ref[...]

    wlog = -_softplus(-(w0_ref[...] + _dot_w(jnp.tanh(wa), w2_ref))) - 0.5
    lw = -jnp.exp(wlog)
    a_sig = _sigmoid(a0_ref[...] + _dot_w(wa, a2_ref))
    g = _dot_w(_sigmoid(gl), g2_ref)
    yield
    kk = k * kk_ref[...]
    kk = kk * lax.rsqrt(jnp.maximum(_head_sum(kk * kk, seg), 1e-24))
    kh = k * (1.0 + (a_sig - 1.0) * ka_ref[...])
    av = -kk
    bv = kk * a_sig
    bonus = _head_sum(r * kh * rk_ref[...], seg) * v
    if n_valid < C:
        valid = _iota((C, 1024), 0) < n_valid
        lw = jnp.where(valid, lw, 0.0)
        av = jnp.where(valid, av, 0.0)
        bv = jnp.where(valid, bv, 0.0)
        kh = jnp.where(valid, kh, 0.0)
        v = jnp.where(valid, v, 0.0)

    yield
    causal_c = jnp.where(_iota((C, C), 0) >= _iota((C, C), 1), 1.0, 0.0).astype(BF16)
    cl = _dot_sel3(causal_c, lw)
    e_pos = jnp.exp(cl)
    e_neg = jnp.exp(-cl)
    rt = r * e_pos
    at = av * jnp.exp(cl - lw)
    bt = bv * e_neg
    kt = kh * e_neg
    p_end = e_pos[C - 1:C, :]

    R2 = 2 * C
    lane_lo = _iota((C, LANES), 1) < HD_RWKV
    rr = _iota((R2, R2), 0)
    cc = _iota((R2, R2), 1)
    strict = rr > cc
    incl = rr >= cc
    eye = (rr == cc).astype(F32)

    def stack(x):
        return jnp.concatenate([jnp.where(lane_lo, x, 0.0), jnp.where(lane_lo, 0.0, x)], axis=0)

    pairs = range(8)
    sls = [slice(p * LANES, (p + 1) * LANES) for p in pairs]
    a_s = [stack(at[:, sl]) for sl in sls]
    r_s = [stack(rt[:, sl]) for sl in sls]
    b_s = [stack(bt[:, sl]) for sl in sls]
    k_s = [stack(kt[:, sl]) for sl in sls]
    v_s = [stack(v[:, sl]) for sl in sls]
    yield
    s_bd = [state_ref[p] for p in pairs]
    wide = R2 == LANES
    if wide:
        ar = [jnp.concatenate([a_s[p], r_s[p]], axis=0) for p in pairs]
        bk = [jnp.concatenate([b_s[p], k_s[p]], axis=0) for p in pairs]
        quad = [_cmm(ar[p], bk[p], NT) for p in pairs]
        ars = [_cmm(ar[p], s_bd[p], NT) for p in pairs]
        m_ab = [quad[p][:R2, :R2] for p in pairs]
        m_ak = [jnp.where(strict, quad[p][:R2, R2:], 0.0) for p in pairs]
        w = [ars[p][:R2] + _cmm(m_ak[p], v_s[p], NN) for p in pairs]
    else:
        m_ab = [_cmm(a_s[p], b_s[p], NT) for p in pairs] if n_valid > 1 else None
        m_ak = [jnp.where(strict, _cmm(a_s[p], k_s[p], NT), 0.0) for p in pairs]
        w = [_cmm(a_s[p], s_bd[p], NT) + _cmm(m_ak[p], v_s[p], NN) for p in pairs]
    yield
    if n_valid > 1:
        x = [jnp.where(strict, m_ab[p], 0.0) for p in pairs]
        t = [eye + x[p] for p in pairs]
        span = 2
        while span < C:
            x = [_cmm(x[p], x[p], NN) for p in pairs]
            t = [t[p] + _cmm(t[p], x[p], NN) for p in pairs]
            span *= 2
            yield
        u = [_cmm(t[p], w[p], NN) for p in pairs]
    else:
        u = w
    yield
    ys = []
    for p in pairs:
        if wide:
            n_r = jnp.where(jnp.concatenate([incl, incl], axis=1), quad[p][R2:, :], 0.0)
            uv = jnp.concatenate([u[p], v_s[p]], axis=0)
            y2 = ars[p][R2:] + _cmm(n_r, uv, NN)
            upd = _cmm(uv, bk[p], TN)
        else:
            n_rb = jnp.where(incl, _cmm(r_s[p], b_s[p], NT), 0.0)
            n_rk = jnp.where(incl, _cmm(r_s[p], k_s[p], NT), 0.0)
            y2 = _cmm(r_s[p], s_bd[p], NT) + _cmm(n_rb, u[p], NN) + _cmm(n_rk, v_s[p], NN)
            upd = _cmm(u[p], b_s[p], TN) + _cmm(v_s[p], k_s[p], TN)
        ys.append(y2[0:C] + y2[C:R2])
        state_ref[p] = (s_bd[p] + upd) * p_end[:, sls[p]]
    yield
    yr = jnp.concatenate(ys, axis=1)
    mean = _head_sum(yr, seg) * (1.0 / HD_RWKV)
    d = yr - mean
    var = _head_sum(d * d, seg) * (1.0 / HD_RWKV)
    yn = d * lax.rsqrt(var + RWKV_LN_EPS) * lnw_ref[...] + lnb_ref[...]
    y_ref[...] = (yn + bonus) * g


def _rwkv(proj3, shift_prev8, wkv_prev, p, *, C, n_valid, nb):
    b, lp, _ = proj3.shape
    nc = lp // C
    full = lambda shape: pl.BlockSpec(shape, lambda i, c: (0,) * len(shape))
    state = pl.BlockSpec((nb, H_RWKV, HD_RWKV, HD_RWKV), lambda i, c: (i, 0, 0, 0))
    body = _per_sequence(nb, [True] * 6 + [False] * 12 + [True] * 4, _rwkv_init,
                         functools.partial(_rwkv_main, C=C, n_valid=n_valid), _rwkv_fini)
    return pl.pallas_call(
        body,
        grid=(b // nb, nc),
        in_specs=[
            pl.BlockSpec((nb, C, 1024), lambda i, c: (i, c, COL_R)),
            pl.BlockSpec((nb, C, 1024), lambda i, c: (i, c, COL_K)),
            pl.BlockSpec((nb, C, 1024), lambda i, c: (i, c, COL_V)),
            pl.BlockSpec((nb, C, 256), lambda i, c: (i, c, COL_LR)),
            pl.BlockSpec((nb, 8, RWKV_PROJ), lambda i, c: (i, 0, 0)),
            state,
            full((1, RWKV_PROJ)), full((1, 1024)), full((2, 128, 1024)), full((1, 1024)), full((2, 128, 1024)),
            full((2, 128, 1024)), full((1, 1024)), full((1, 1024)), full((1, 1024)), full((1, 1024)), full((1, 1024)),
            full((128, 128)),
        ],
        out_specs=[pl.BlockSpec((nb, C, 1024), lambda i, c: (i, c, 0)), state],
        out_shape=[jax.ShapeDtypeStruct((b, lp, 1024), F32), jax.ShapeDtypeStruct((b, H_RWKV, HD_RWKV, HD_RWKV), F32)],
        scratch_shapes=[pltpu.VMEM((nb, C + 8, RWKV_PROJ), F32), pltpu.VMEM((nb, 8, 128, 128), F32)],
        compiler_params=_cparams(("parallel", "arbitrary")),
        name="rwkv",
    )(proj3, proj3, proj3, proj3, shift_prev8, wkv_prev, p["rwkv_mu"], p["rwkv_w0"], p["rwkv_w2p"], p["rwkv_a0"],
      p["rwkv_a2p"], p["rwkv_g2"], p["rwkv_k_k"], p["rwkv_k_a"], p["rwkv_r_k"], p["rwkv_ln_w"], p["rwkv_ln_b"],
      p["seg"])


def _merge_body(x_ref, gs_ref, gr_ref, ys_ref, yr_ref, w_ref, o_ref):
    merged = _sigmoid(gs_ref[...]) * ys_ref[...] + _sigmoid(gr_ref[...]) * yr_ref[...]
    o_ref[...] = x_ref[...] + _dot(merged.astype(BF16), w_ref[...])


def _merge_out(x, proj, y_ssd, y_rwkv, w_out, *, tm=512):
    m = x.shape[0]
    tm = min(tm, m)
    row = lambda j: pl.BlockSpec((tm, 1024), lambda i: (i, j))
    return pl.pallas_call(
        _merge_body,
        grid=(m // tm,),
        in_specs=[row(0), row(COL_GS), row(COL_GR), row(0), row(0), pl.BlockSpec((1024, 1024), lambda i: (0, 0))],
        out_specs=row(0),
        out_shape=jax.ShapeDtypeStruct((m, 1024), F32),
        compiler_params=_cparams(("parallel",)),
        name="merge_out",
    )(x, proj, proj, y_ssd, y_rwkv, w_out.astype(BF16))


def _mem_attention(q, k_ref, v_ref):
    outs = []
    for h in range(MEM_HEADS):
        sl = slice(h * MEM_HD, (h + 1) * MEM_HD)
        s = _dot_nt(q[:, sl].astype(BF16), k_ref[:, sl].astype(BF16)) * (MEM_HD ** -0.5)
        s = s - jnp.max(s, axis=-1, keepdims=True)
        e = jnp.exp(s)
        prob = e / jnp.sum(e, axis=-1, keepdims=True)
        outs.append(_dot(prob.astype(BF16), v_ref[:, sl].astype(BF16)))
    return jnp.concatenate(outs, axis=1)


def _tail_body(x_ref, gs_ref, gr_ref, ys_ref, yr_ref, k_ref, v_ref, wout_ref, nmem_ref, wmq_ref, wmo_ref,
               nffn_ref, wpq_ref, x2_ref, pq_ref):
    merged = _sigmoid(gs_ref[...]) * ys_ref[...] + _sigmoid(gr_ref[...]) * yr_ref[...]
    x1 = x_ref[...] + _dot(merged.astype(BF16), wout_ref[...])
    q = _dot(_rmsnorm(x1, nmem_ref[...]).astype(BF16), wmq_ref[...])
    x2 = x1 + _dot(_mem_attention(q, k_ref, v_ref).astype(BF16), wmo_ref[...])
    x2_ref[...] = x2
    pq_ref[...] = _dot(_rmsnorm(x2, nffn_ref[...]).astype(BF16), wpq_ref[...])


def _tail(x3, proj3, y_ssd, y_rwkv, mem_k, mem_v, p, *, tm=256):
    b, L, _ = x3.shape
    once = pl.Buffered(1)
    tile = lambda j: pl.BlockSpec((None, tm, 1024), lambda i, t: (i, t, j))
    mem = pl.BlockSpec((None, N_MEM, 1024), lambda i, t: (i, 0, 0))
    full = lambda r, c: pl.BlockSpec((r, c), lambda i, t: (0, 0), pipeline_mode=once)
    return pl.pallas_call(
        _tail_body,
        grid=(b, L // tm),
        in_specs=[tile(0), tile(COL_GS), tile(COL_GR), tile(0), tile(0), mem, mem,
                  full(1024, 1024), full(1, 1024), full(1024, 1024), full(1024, 1024), full(1, 1024),
                  full(1024, 2048)],
        out_specs=[tile(0), pl.BlockSpec((None, tm, 2048), lambda i, t: (i, t, 0))],
        out_shape=[jax.ShapeDtypeStruct((b, L, 1024), F32), jax.ShapeDtypeStruct((b, L, 2048), F32)],
        compiler_params=_cparams(("parallel", "arbitrary")),
        name="token_tail",
    )(x3, proj3, proj3, y_ssd, y_rwkv, mem_k, mem_v, p["w_out"].astype(BF16), p["norm_mem_w"].reshape(1, 1024),
      p["w_mq"].astype(BF16), p["w_mo"].astype(BF16), p["norm_ffn_w"].reshape(1, 1024), p["w_pq"].astype(BF16))


def _attn_step_body(q_ref, k_ref, v_ref, o_ref):
    s = jnp.sum(k_ref[...] * q_ref[...][:, None], axis=-1, keepdims=True) * (MEM_HD ** -0.5)
    e = jnp.exp(s - jnp.max(s, axis=1, keepdims=True))
    prob = e / jnp.sum(e, axis=1, keepdims=True)
    o_ref[...] = jnp.sum(prob * v_ref[...], axis=1)


def _attn_step(q, mem_k, mem_v, *, nb=2):
    b = q.shape[0]
    qspec = pl.BlockSpec((nb, MEM_HEADS, MEM_HD), lambda i: (i, 0, 0))
    kvspec = pl.BlockSpec((nb, N_MEM, MEM_HEADS, MEM_HD), lambda i: (i, 0, 0, 0))
    return pl.pallas_call(
        _attn_step_body,
        grid=(b // nb,),
        in_specs=[qspec, kvspec, kvspec],
        out_specs=qspec,
        out_shape=jax.ShapeDtypeStruct((b, MEM_HEADS, MEM_HD), F32),
        compiler_params=_cparams(("parallel",)),
        name="mem_attn_step",
    )(q, mem_k, mem_v)


def _stage2_pairs():
    return [(r1, r2) for r1 in range(PEER_TOPK) for r2 in range(PEER_TOPK) if (r1 + 1) * (r2 + 1) <= PEER_TOPK]


N_CAND = 56


def _topk_body(q_ref, keys_ref, gate_ref, i1_ref, i2_ref, s_scr, i_scr, cand_scr, code_scr, res_scr, *, T):
    neg = -jnp.inf
    key_row = _iota((N_KEYS, T), 0).astype(F32)
    cand_row = _iota((N_CAND, T), 0).astype(F32)
    pairs = _stage2_pairs()
    q = q_ref[...]
    for h in range(PEER_HEADS):
        for c in range(2):
            hc = 2 * h + c
            s = _dot_nt(keys_ref[hc], q[:, hc * LANES:(hc + 1) * LANES], HI)
            for r in range(PEER_TOPK):
                m = jnp.max(s, axis=0, keepdims=True)
                idx = jnp.min(jnp.where(s == m, key_row, float(N_KEYS)), axis=0, keepdims=True)
                s_scr[c, r:r + 1, :] = m
                i_scr[c, r:r + 1, :] = idx
                s = jnp.where(key_row == idx, neg, s)
        cand_scr[...] = jnp.full((N_CAND, T), neg, F32)
        code_scr[...] = jnp.zeros((N_CAND, T), F32)
        for j, (r1, r2) in enumerate(pairs):
            cand_scr[j:j + 1, :] = s_scr[0, r1:r1 + 1, :] + s_scr[1, r2:r2 + 1, :]
            code_scr[j:j + 1, :] = i_scr[0, r1:r1 + 1, :] * float(N_KEYS) + i_scr[1, r2:r2 + 1, :]
        cand = cand_scr[...]
        code = code_scr[...]
        for r in range(PEER_TOPK):
            m = jnp.max(cand, axis=0, keepdims=True)
            idx = jnp.min(jnp.where(cand == m, cand_row, float(N_CAND)), axis=0, keepdims=True)
            hit = cand_row == idx
            res_scr[0, r:r + 1, :] = m
            res_scr[1, r:r + 1, :] = jnp.sum(jnp.where(hit, code, 0.0), axis=0, keepdims=True)
            cand = jnp.where(hit, neg, cand)
        top = res_scr[0]
        e = jnp.exp(top - jnp.max(top, axis=0, keepdims=True))
        rows = slice(h * PEER_TOPK, (h + 1) * PEER_TOPK)
        gate_ref[rows, :] = e / jnp.sum(e, axis=0, keepdims=True)
        first = jnp.floor(res_scr[1] * (1.0 / N_KEYS))
        i1_ref[rows, :] = first
        i2_ref[rows, :] = res_scr[1] - first * float(N_KEYS)


def _peer_topk(q, sub_keys, *, T=256):
    m = q.shape[0]
    T = min(T, m)
    body = functools.partial(_topk_body, T=T)
    out = pl.BlockSpec((128, T), lambda i: (0, i))
    return pl.pallas_call(
        body,
        grid=(m // T,),
        in_specs=[pl.BlockSpec((T, 2048), lambda i: (i, 0)), pl.BlockSpec((16, 128, 128), lambda i: (0, 0, 0))],
        out_specs=[out, out, out],
        out_shape=[jax.ShapeDtypeStruct((128, m), F32)] * 3,
        scratch_shapes=[pltpu.VMEM((2, PEER_TOPK, T), F32), pltpu.VMEM((2, PEER_TOPK, T), F32),
                        pltpu.VMEM((N_CAND, T), F32), pltpu.VMEM((N_CAND, T), F32),
                        pltpu.VMEM((2, PEER_TOPK, T), F32)],
        compiler_params=_cparams(("parallel",)),
        name="peer_topk",
    )(q, sub_keys.reshape(16, 128, 128))


I1_PER_STEP = 8
G_PITCH = 136


def _peer_body(x_ref, gate_ref, i1_ref, i2_ref, nw_ref, fw_ref, u_ref, v_ref, o_ref, hn_scr, g_scr, coef_scr, *, T):
    j = pl.program_id(1)

    @pl.when(j == 0)
    def _():
        hn_scr[...] = _rmsnorm(x_ref[...], nw_ref[...]).astype(BF16)
        o_ref[...] = jnp.zeros_like(o_ref)
        shape = (N_KEYS, LANES)
        key_row = _iota(shape, 0).astype(F32).astype(BF16)
        one = jnp.ones(shape, BF16)
        zero = jnp.zeros(shape, BF16)

        def build(t, carry):
            row = lambda ref: jnp.broadcast_to(ref[pl.ds(t, 1), :].astype(BF16), shape)
            a_t = jnp.where(key_row == row(i1_ref), row(gate_ref), zero)
            b_t = jnp.where(key_row == row(i2_ref), one, zero)
            g_scr[pl.ds(pl.multiple_of(t * G_PITCH, 8), N_KEYS), :] = _dot_nt(a_t, b_t)
            return carry

        lax.fori_loop(0, T, build, 0, unroll=32)

    hn = hn_scr[...]
    for cidx in range(I1_PER_STEP // 2):
        act = _dot_nt(hn, u_ref[cidx * 2 * LANES:(cidx + 1) * 2 * LANES, :])
        ge = 0.5 * act * (1.0 + lax.erf(act * (2.0 ** -0.5)))
        for half in range(2):
            ii = 2 * cidx + half
            g_t = g_scr[pl.ds(j * I1_PER_STEP + ii, T, stride=G_PITCH), :]
            coef_scr[:, ii * LANES:(ii + 1) * LANES] = (g_t * ge[:, half * LANES:(half + 1) * LANES]).astype(BF16)
    o_ref[...] += _dot(coef_scr[...], v_ref[...])

    @pl.when(j == pl.num_programs(1) - 1)
    def _():
        o_ref[...] = _rmsnorm(x_ref[...] + o_ref[...], fw_ref[...])


def _peer_mix(x, gate, i1, i2, norm_w, final_w, u_bf16, v_bf16, *, T=512):
    m = x.shape[0]
    T = min(T, m)
    ne = I1_PER_STEP * N_KEYS
    body = functools.partial(_peer_body, T=T)
    once = pl.Buffered(1)
    row = lambda w: pl.BlockSpec((T, w), lambda i, j: (i, 0), pipeline_mode=once)
    vec = pl.BlockSpec((1, 1024), lambda i, j: (0, 0), pipeline_mode=once)
    return pl.pallas_call(
        body,
        grid=(m // T, N_KEYS // I1_PER_STEP),
        in_specs=[row(1024), row(128), row(128), row(128), vec, vec,
                  pl.BlockSpec((ne, 1024), lambda i, j: (j, 0)), pl.BlockSpec((ne, 1024), lambda i, j: (j, 0))],
        out_specs=pl.BlockSpec((T, 1024), lambda i, j: (i, 0)),
        out_shape=jax.ShapeDtypeStruct((m, 1024), F32),
        scratch_shapes=[pltpu.VMEM((T, 1024), BF16), pltpu.VMEM((T * G_PITCH, LANES), F32),
                        pltpu.VMEM((T, 1024), BF16)],
        compiler_params=_cparams(("parallel", "arbitrary")),
        name="peer_mix",
    )(x, gate, i1, i2, norm_w.reshape(1, 1024), final_w.reshape(1, 1024), u_bf16, v_bf16)


def _pad_rows(x, rows):
    return jnp.pad(x, ((0, 0), (rows - x.shape[1], 0), (0, 0)))


def _block(x, mem_k, mem_v, conv_prev, ssm_prev, shift_prev, wkv_prev, p, *, ssd_l, rwkv_c):
    b, L, _ = x.shape
    m = b * L
    x2d = x.reshape(m, D_MODEL)
    proj = _linear(x2d, p["w_in"], norm_w=p["norm_mix_w"], tm=1024, tn=1152, name="in_proj")
    proj3 = proj.reshape(b, L, D_IN_PAD)
    lp = max(L, SUBLANES)
    n_valid = L if L < SUBLANES else max(ssd_l, rwkv_c)
    proj3p = proj3 if lp == L else jnp.pad(proj3, ((0, 0), (0, lp - L), (0, 0)))

    nb = SEQS_PER_STEP if L == 1 else 1
    y_ssd, ssm_new = _ssd(proj3p, _pad_rows(conv_prev, 8), ssm_prev, p,
                          l=ssd_l, n_valid=min(n_valid, ssd_l), nb=nb)
    y_rwkv, wkv_new = _rwkv(proj3p, _pad_rows(shift_prev[:, None, :], 8), wkv_prev, p,
                            C=rwkv_c, n_valid=min(n_valid, rwkv_c), nb=nb)
    tail = proj3[:, max(L - (CONV_K - 1), 0):]
    xbc = jnp.concatenate([tail[..., 1024:2048], tail[..., 7168:7680]], axis=-1)
    rw = jnp.concatenate([tail[..., 2048:5120], tail[..., 7680:7936]], axis=-1)
    conv_new = jnp.concatenate([conv_prev, xbc], axis=1)[:, -(CONV_K - 1):]
    shift_new = rw[:, -1]

    if L == 1:
        x1 = _merge_out(x2d, proj, y_ssd[:, 0], y_rwkv[:, 0], p["w_out"])
        q = _linear(x1, p["w_mq"], norm_w=p["norm_mem_w"], name="mem_q")
        o = _attn_step(q.reshape(b, MEM_HEADS, MEM_HD), mem_k, mem_v)
        x2 = _linear(o.reshape(m, D_MODEL), p["w_mo"], res=x1, name="mem_o")
        pq = _linear(x2, p["w_pq"], norm_w=p["norm_ffn_w"], name="peer_q")
    else:
        x2, pq = _tail(x, proj3, y_ssd, y_rwkv, mem_k.reshape(b, N_MEM, D_MODEL), mem_v.reshape(b, N_MEM, D_MODEL), p)
        x2, pq = x2.reshape(m, D_MODEL), pq.reshape(m, 2 * D_MODEL)
    gate_t, i1_t, i2_t = _peer_topk(pq, p["sub_keys"])
    y = _peer_mix(x2, gate_t.T, i1_t.T, i2_t.T, p["norm_ffn_w"], p["norm_final_w"], p["expert_u"], p["expert_v"])
    return y.reshape(b, L, D_MODEL), conv_new, ssm_new, shift_new, wkv_new


def kernel(x_prompt, x_sample, mem_prompt, state_ssm, state_conv, state_wkv, state_shift, cache_mem_k, cache_mem_v, norm_mix_w, w_in, conv_w, conv_b, dt_bias, a_log, d_skip, ssd_norm_w, rwkv_mu, rwkv_w0, rwkv_w2, rwkv_a0, rwkv_a2, rwkv_g2, rwkv_k_k, rwkv_k_a, rwkv_r_k, rwkv_ln_w, rwkv_ln_b, w_out, norm_mem_w, mem_norm_w, w_mk, w_mv, w_mq, w_mo, norm_ffn_w, w_pq, sub_keys, expert_u, expert_v, norm_final_w):
    assert w_in.shape[0] == 1, "single-layer problem"
    bp = x_prompt.shape[0]
    bs = x_sample.shape[0]
    wi = w_in[0]
    w_in_perm = jnp.concatenate(
        [wi[:, 0:2048], wi[:, 2576:5648], wi[:, 5904:7952], wi[:, 2048:2560], wi[:, 5648:5904], wi[:, 2560:2576],
         jnp.zeros((D_MODEL, LANES - H_SSD), F32)], axis=1)
    pad16 = lambda a: jnp.pad(a.reshape(1, H_SSD), ((0, 0), (0, LANES - H_SSD)))
    head_of_lane = jnp.arange(D_MODEL) // SSD_HEADDIM
    zeros64 = jnp.zeros((64, D_MODEL), F32)
    hi_lo = lambda w: jnp.stack(_split_bf16(w))
    p = {
        "norm_mix_w": norm_mix_w[0], "w_in": w_in_perm,
        "conv_w": conv_w[0], "conv_b": conv_b[0].reshape(1, CONV_DIM),
        "dt_bias": pad16(dt_bias[0]), "a_log": pad16(a_log[0]),
        "d_skip_e": jnp.repeat(d_skip[0], SSD_HEADDIM).reshape(1, D_MODEL),
        "ssd_norm_w": ssd_norm_w[0].reshape(1, D_MODEL),
        "expand": (jnp.arange(LANES)[:, None] == head_of_lane[None, :]).astype(BF16),
        "rwkv_mu": rwkv_mu[0].reshape(1, RWKV_PROJ), "rwkv_w0": rwkv_w0[0].reshape(1, D_MODEL),
        "rwkv_w2p": hi_lo(jnp.concatenate([rwkv_w2[0], zeros64], axis=0)),
        "rwkv_a0": rwkv_a0[0].reshape(1, D_MODEL),
        "rwkv_a2p": hi_lo(jnp.concatenate([zeros64, rwkv_a2[0]], axis=0)),
        "rwkv_g2": hi_lo(rwkv_g2[0]), "rwkv_k_k": rwkv_k_k[0].reshape(1, D_MODEL),
        "rwkv_k_a": rwkv_k_a[0].reshape(1, D_MODEL), "rwkv_r_k": rwkv_r_k[0].reshape(1, D_MODEL),
        "rwkv_ln_w": rwkv_ln_w[0].reshape(1, D_MODEL), "rwkv_ln_b": rwkv_ln_b[0].reshape(1, D_MODEL),
        "seg": (head_of_lane[:LANES, None] == head_of_lane[None, :LANES]).astype(BF16),
        "w_out": w_out[0], "norm_mem_w": norm_mem_w[0], "w_mq": w_mq[0], "w_mo": w_mo[0],
        "norm_ffn_w": norm_ffn_w[0], "w_pq": w_pq[0], "sub_keys": sub_keys[0],
        "expert_u": expert_u[0].astype(BF16), "expert_v": expert_v[0].astype(BF16),
        "norm_final_w": norm_final_w,
    }

    mem2d = mem_prompt.reshape(bp * N_MEM, D_MODEL)
    mem_k = _linear(mem2d, w_mk[0], norm_w=mem_norm_w[0], name="mem_k").reshape(bp, N_MEM, MEM_HEADS, MEM_HD)
    mem_v = _linear(mem2d, w_mv[0], norm_w=mem_norm_w[0], name="mem_v").reshape(bp, N_MEM, MEM_HEADS, MEM_HD)
    y_p, conv_p, ssm_p, shift_p, wkv_p = _block(
        x_prompt, mem_k, mem_v,
        jnp.zeros((bp, CONV_K - 1, CONV_DIM), F32), jnp.zeros((bp, H_SSD, SSD_HEADDIM, SSD_STATE), F32),
        jnp.zeros((bp, RWKV_PROJ), F32), jnp.zeros((bp, H_RWKV, HD_RWKV, HD_RWKV), F32), p,
        ssd_l=SSD_CHUNK, rwkv_c=RWKV_CHUNK)
    y_s, conv_s, ssm_s, shift_s, wkv_s = _block(
        x_sample, cache_mem_k[0], cache_mem_v[0], state_conv[0], state_ssm[0], state_shift[0], state_wkv[0], p,
        ssd_l=SUBLANES, rwkv_c=SUBLANES)
    return (y_p, y_s, ssm_p[None], conv_p[None], wkv_p[None], shift_p[None], mem_k[None], mem_v[None],
            ssm_s[None], conv_s[None], wkv_s[None], shift_s[None])
```

```python
import functools

import jax
import jax.numpy as jnp
from jax import lax
from jax.experimental import pallas as pl
from jax.experimental.pallas import tpu as pltpu

F32 = jnp.float32
BF16 = jnp.bfloat16
HI = lax.Precision.HIGHEST

D_MODEL = 1024
H_SSD = 16
SSD_HEADDIM = 64
SSD_STATE = 128
CONV_K = 4
CONV_DIM = 1536
H_RWKV = 16
HD_RWKV = 64
RWKV_PROJ = 3328
RWKV_LN_EPS = 64e-5
N_MEM = 256
MEM_HEADS = 4
MEM_HD = 256
N_KEYS = 128
PEER_HEADS = 8
PEER_TOPK = 16
NORM_EPS = 1e-6

LANES = 128
SUBLANES = 8
VMEM_LIMIT = 56 * 1024 * 1024

D_IN_PAD = 8064
COL_Z, COL_XS, COL_R, COL_K, COL_V, COL_GS, COL_GR = 0, 1, 2, 3, 4, 5, 6
COL_BC = 14
COL_LR = 30
COL_DT = 62

SSD_CHUNK = 128
RWKV_CHUNK = 64
SEQS_PER_STEP = 8


def _cparams(sem):
    return pltpu.CompilerParams(dimension_semantics=sem, vmem_limit_bytes=VMEM_LIMIT)


def _dot(a, b, prec=None):
    return jnp.dot(a, b, preferred_element_type=F32, precision=prec)


def _dot_nt(a, b, prec=None):
    return lax.dot_general(a, b, (((1,), (1,)), ((), ())), preferred_element_type=F32, precision=prec)


def _dot_tn(a, b, prec=None):
    return lax.dot_general(a, b, (((0,), (0,)), ((), ())), preferred_element_type=F32, precision=prec)


def _split_bf16(x):
    hi = x.astype(BF16)
    lo = (x - hi.astype(F32)).astype(BF16)
    return hi, lo


def _dot_onehot(x, sel_bf16):
    hi, lo = _split_bf16(x)
    return _dot(hi, sel_bf16) + _dot(lo, sel_bf16)


NN = ((1,), (0,))
NT = ((1,), (1,))
TN = ((0,), (0,))


def _cmm(a, b, dims):
    return lax.dot_general(a.astype(BF16), b.astype(BF16), (dims, ((), ())), preferred_element_type=F32)


def _softplus(x):
    return jnp.maximum(x, 0.0) + jnp.log1p(jnp.exp(-jnp.abs(x)))


def _sigmoid(x):
    return 1.0 / (1.0 + jnp.exp(-x))


def _rmsnorm(x, w):
    return x * lax.rsqrt(jnp.mean(x * x, axis=-1, keepdims=True) + NORM_EPS) * w


def _iota(shape, dim):
    return lax.broadcasted_iota(jnp.int32, shape, dim)


def _per_sequence(nb, per_seq, init, main, fini=None):
    def wrapped(*refs):
        views = [[r.at[s] if flag else r for r, flag in zip(refs, per_seq)] for s in range(nb)]
        c = pl.program_id(1)

        @pl.when(c == 0)
        def _():
            for v in views:
                init(*v)

        live = [main(*v) for v in views]
        while live:
            live = [g for g in live if next(g, True) is None]
        if fini is not None:
            @pl.when(c == pl.num_programs(1) - 1)
            def _():
                for v in views:
                    fini(*v)
    return wrapped


def _linear_body(*refs, has_norm, has_res):
    it = iter(refs)
    x_ref = next(it)
    nw_ref = next(it) if has_norm else None
    w_ref = next(it)
    res_ref = next(it) if has_res else None
    o_ref = next(it)
    x = x_ref[...]
    if has_norm:
        x = _rmsnorm(x, nw_ref[...])
    acc = _dot(x.astype(BF16), w_ref[...])
    if has_res:
        acc = acc + res_ref[...]
    o_ref[...] = acc


def _linear(x, w, *, norm_w=None, res=None, tm=512, tn=1024, name="linear"):
    m, k = x.shape
    n = w.shape[1]
    tm = min(tm, m)
    tn = min(tn, n)
    assert m % tm == 0 and n % tn == 0
    args = [x]
    specs = [pl.BlockSpec((tm, k), lambda i, j: (i, 0))]
    if norm_w is not None:
        args.append(norm_w.reshape(1, k))
        specs.append(pl.BlockSpec((1, k), lambda i, j: (0, 0)))
    args.append(w.astype(BF16))
    specs.append(pl.BlockSpec((k, tn), lambda i, j: (0, j)))
    if res is not None:
        args.append(res)
        specs.append(pl.BlockSpec((tm, tn), lambda i, j: (i, j)))
    body = functools.partial(_linear_body, has_norm=norm_w is not None, has_res=res is not None)
    return pl.pallas_call(
        body,
        grid=(m // tm, n // tn),
        in_specs=specs,
        out_specs=pl.BlockSpec((tm, tn), lambda i, j: (i, j)),
        out_shape=jax.ShapeDtypeStruct((m, n), F32),
        compiler_params=_cparams(("parallel", "arbitrary")),
        name=name,
    )(*args)


W_IN_SEGMENTS = ((0, 2048), (2576, 5648), (5904, 7952), (2048, 2560), (5648, 5904), (2560, 2576))


def _permute_body(w_ref, o_ref):
    w = w_ref[...]
    parts = [w[:, a:b] for a, b in W_IN_SEGMENTS]
    parts.append(jnp.zeros((w.shape[0], LANES - H_SSD), F32))
    o_ref[...] = jnp.concatenate(parts, axis=1).astype(BF16)


def _permute_w_in(w):
    k, n = w.shape
    tr = 128
    return pl.pallas_call(
        _permute_body,
        grid=(k // tr,),
        in_specs=[pl.BlockSpec((tr, n), lambda i: (i, 0))],
        out_specs=pl.BlockSpec((tr, D_IN_PAD), lambda i: (i, 0)),
        out_shape=jax.ShapeDtypeStruct((k, D_IN_PAD), BF16),
        compiler_params=_cparams(("parallel",)),
        name="permute_w_in",
    )(w)


def _ssd_init(z_ref, xs_ref, bc_ref, dt_ref, convprev_ref, ssmprev_ref, convw_ref, convb_ref, dtb_ref,
              alog_ref, dskip_ref, normw_ref, expand_ref, y_ref, state_ref, ext_ref):
    ext_ref[0:8, :] = convprev_ref[...]
    state_ref[...] = ssmprev_ref[...]


def _ssd_main(z_ref, xs_ref, bc_ref, dt_ref, convprev_ref, ssmprev_ref, convw_ref, convb_ref, dtb_ref,
              alog_ref, dskip_ref, normw_ref, expand_ref, y_ref, state_ref, ext_ref, *, l, n_valid):
    ext_ref[8:8 + l, 0:D_MODEL] = xs_ref[...]
    ext_ref[8:8 + l, D_MODEL:CONV_DIM] = bc_ref[...]
    conv = convb_ref[...]
    for i in range(CONV_K):
        conv = conv + ext_ref[pl.ds(5 + i, l), :] * convw_ref[i:i + 1, :]
    ext_ref[0:8, :] = ext_ref[l:l + 8, :]

    yield
    act = conv * _sigmoid(conv)
    xs = act[:, 0:D_MODEL]
    bm = act[:, D_MODEL:D_MODEL + 256]
    cm = act[:, D_MODEL + 256:CONV_DIM]

    dt = _softplus(dt_ref[...] + dtb_ref[...])
    if n_valid < l:
        dt = jnp.where(_iota((l, LANES), 0) < n_valid, dt, 0.0)
    ad = dt * (-jnp.exp(alog_ref[...]))
    row = _iota((l, l), 0)
    col = _iota((l, l), 1)
    causal = row >= col
    cs = _dot_sel3(jnp.where(causal, 1.0, 0.0).astype(BF16), ad)
    eye = jnp.where(_iota((LANES, LANES), 0) == _iota((LANES, LANES), 1), 1.0, 0.0).astype(BF16)
    cs_parts = _split3(cs)
    cs_t = sum(_dot_nt(eye, part) for part in cs_parts)
    expand = expand_ref[...]
    dt_e = sum(_dot(part, expand) for part in _split3(dt))
    cs_e = sum(_dot(part, expand) for part in cs_parts)
    yield
    xdt = xs * dt_e
    decay_in = jnp.exp(cs_e)
    decay_out = jnp.exp(cs_e[l - 1:l, :] - cs_e)
    xdt_dec = (xdt * decay_out).astype(BF16)
    lane_lo = _iota((l, LANES), 1) < SSD_HEADDIM
    row_lo = _iota((LANES, 1), 0) < SSD_HEADDIM

    ys = []
    for g in range(2):
        bg = bm[:, g * LANES:(g + 1) * LANES].astype(BF16)
        cg = cm[:, g * LANES:(g + 1) * LANES].astype(BF16)
        cb = _dot_nt(cg, bg)
        for pr in range(4 * g, 4 * g + 4):
            sl = slice(pr * LANES, (pr + 1) * LANES)
            h0, h1 = 2 * pr, 2 * pr + 1
            m0 = jnp.where(causal, jnp.exp(cs[:, h0:h0 + 1] - cs_t[h0:h0 + 1, :]), 0.0) * cb
            m1 = jnp.where(causal, jnp.exp(cs[:, h1:h1 + 1] - cs_t[h1:h1 + 1, :]), 0.0) * cb
            xp = xdt[:, sl]
            x_lo = jnp.where(lane_lo, xp, 0.0).astype(BF16)
            x_hi = jnp.where(lane_lo, 0.0, xp).astype(BF16)
            y_diag = _dot(m0.astype(BF16), x_lo) + _dot(m1.astype(BF16), x_hi)
            s_pair = state_ref[2 * pr:2 * pr + 2].reshape(LANES, SSD_STATE)
            y_off = _dot_nt(cg, s_pair.astype(BF16)) * decay_in[:, sl]
            ys.append(y_diag + y_off + dskip_ref[:, sl] * xs[:, sl])
            upd = _dot_tn(xdt_dec[:, sl], bg)
            last = cs[l - 1:l, :]
            row_decay = jnp.exp(jnp.where(row_lo, last[:, h0:h0 + 1], last[:, h1:h1 + 1]))
            state_ref[2 * pr:2 * pr + 2] = (row_decay * s_pair + upd).reshape(2, SSD_HEADDIM, SSD_STATE)
            yield
    y = jnp.concatenate(ys, axis=1)
    z = z_ref[...]
    y_ref[...] = _rmsnorm(y * (z * _sigmoid(z)), normw_ref[...])


def _ssd(proj3, conv_prev8, ssm_prev, p, *, l, n_valid, nb):
    b, lp, _ = proj3.shape
    nc = lp // l
    full = lambda shape: pl.BlockSpec(shape, lambda i, c: (0,) * len(shape))
    state = pl.BlockSpec((nb, H_SSD, SSD_HEADDIM, SSD_STATE), lambda i, c: (i, 0, 0, 0))
    body = _per_sequence(nb, [True] * 6 + [False] * 7 + [True] * 3, _ssd_init,
                         functools.partial(_ssd_main, l=l, n_valid=n_valid))
    return pl.pallas_call(
        body,
        grid=(b // nb, nc),
        in_specs=[
            pl.BlockSpec((nb, l, 1024), lambda i, c: (i, c, COL_Z)),
            pl.BlockSpec((nb, l, 1024), lambda i, c: (i, c, COL_XS)),
            pl.BlockSpec((nb, l, 512), lambda i, c: (i, c, COL_BC)),
            pl.BlockSpec((nb, l, 128), lambda i, c: (i, c, COL_DT)),
            pl.BlockSpec((nb, 8, CONV_DIM), lambda i, c: (i, 0, 0)),
            state,
            full((CONV_K, CONV_DIM)), full((1, CONV_DIM)), full((1, 128)), full((1, 128)),
            full((1, 1024)), full((1, 1024)), full((128, 1024)),
        ],
        out_specs=[pl.BlockSpec((nb, l, 1024), lambda i, c: (i, c, 0)), state],
        out_shape=[jax.ShapeDtypeStruct((b, lp, 1024), F32),
                   jax.ShapeDtypeStruct((b, H_SSD, SSD_HEADDIM, SSD_STATE), F32)],
        scratch_shapes=[pltpu.VMEM((nb, l + 8, CONV_DIM), F32)],
        compiler_params=_cparams(("parallel", "arbitrary")),
        name="ssd",
    )(proj3, proj3, proj3, proj3, conv_prev8, ssm_prev, p["conv_w"], p["conv_b"], p["dt_bias"], p["a_log"],
      p["d_skip_e"], p["ssd_norm_w"], p["expand"])


def _dot_w(x, w_ref):
    xh, xl = _split_bf16(x)
    return _dot(xh, w_ref[0]) + _dot(xl, w_ref[0]) + _dot(xh, w_ref[1])


def _split3(x):
    x1 = x.astype(BF16)
    r1 = x - x1.astype(F32)
    x2 = r1.astype(BF16)
    x3 = (r1 - x2.astype(F32)).astype(BF16)
    return x1, x2, x3


def _dot_sel3(sel_bf16, x):
    return sum(_dot(sel_bf16, part) for part in _split3(x))


def _head_sum(x, seg_pair):
    rows = x.shape[0]
    n = x.shape[1] // LANES
    stacked = jnp.concatenate([x[:, i * LANES:(i + 1) * LANES] for i in range(n)], axis=0)
    s = _dot_onehot(stacked, seg_pair)
    return jnp.concatenate([s[i * rows:(i + 1) * rows] for i in range(n)], axis=1)


def _rwkv_init(r_ref, k_ref, v_ref, lr_ref, shprev_ref, wkvprev_ref, mu_ref, w0_ref, w2_ref, a0_ref, a2_ref,
               g2_ref, kk_ref, ka_ref, rk_ref, lnw_ref, lnb_ref, seg_ref, y_ref, wkv_ref, ext_ref, state_ref):
    same_head = (_iota((LANES, LANES), 0) < HD_RWKV) == (_iota((LANES, LANES), 1) < HD_RWKV)
    ext_ref[0:8, :] = shprev_ref[...]
    for p in range(8):
        s_c = wkvprev_ref[2 * p:2 * p + 2].reshape(LANES, HD_RWKV)
        state_ref[p] = jnp.where(same_head, jnp.concatenate([s_c, s_c], axis=1), 0.0)


def _rwkv_fini(r_ref, k_ref, v_ref, lr_ref, shprev_ref, wkvprev_ref, mu_ref, w0_ref, w2_ref, a0_ref, a2_ref,
               g2_ref, kk_ref, ka_ref, rk_ref, lnw_ref, lnb_ref, seg_ref, y_ref, wkv_ref, ext_ref, state_ref):
    for p in range(8):
        s_p = state_ref[p]
        wkv_ref[2 * p] = s_p[0:HD_RWKV, 0:HD_RWKV]
        wkv_ref[2 * p + 1] = s_p[HD_RWKV:LANES, HD_RWKV:LANES]


def _rwkv_main(r_ref, k_ref, v_ref, lr_ref, shprev_ref, wkvprev_ref, mu_ref, w0_ref, w2_ref, a0_ref, a2_ref,
               g2_ref, kk_ref, ka_ref, rk_ref, lnw_ref, lnb_ref, seg_ref, y_ref, wkv_ref, ext_ref, state_ref,
               *, C, n_valid):
    ext_ref[8:8 + C, 0:1024] = r_ref[...]
    ext_ref[8:8 + C, 1024:2048] = k_ref[...]
    ext_ref[8:8 + C, 2048:3072] = v_ref[...]
    ext_ref[8:8 + C, 3072:3328] = lr_ref[...]
    cur = ext_ref[8:8 + C, :]
    prev = ext_ref[pl.ds(7, C), :]
    mixed = cur + (prev - cur) * mu_ref[...]
    ext_ref[0:8, :] = ext_ref[C:C + 8, :]

    yield
    r = mixed[:, 0:1024]
    k = mixed[:, 1024:2048]
    v = mixed[:, 2048:3072]
    wa = mixed[:, 3072:3200]
    gl = mixed[:, 3200:3328]
    seg = seg_ref[...]

    wlog = -_softplus(-(w0_ref[...] + _dot_w(jnp.tanh(wa), w2_ref))) - 0.5
    lw = -jnp.exp(wlog)
    a_sig = _sigmoid(a0_ref[...] + _dot_w(wa, a2_ref))
    g = _dot_w(_sigmoid(gl), g2_ref)
    yield
    kk = k * kk_ref[...]
    kk = kk * lax.rsqrt(jnp.maximum(_head_sum(kk * kk, seg), 1e-24))
    kh = k * (1.0 + (a_sig - 1.0) * ka_ref[...])
    av = -kk
    bv = kk * a_sig
    bonus = _head_sum(r * kh * rk_ref[...], seg) * v
    if n_valid < C:
        valid = _iota((C, 1024), 0) < n_valid
        lw = jnp.where(valid, lw, 0.0)
        av = jnp.where(valid, av, 0.0)
        bv = jnp.where(valid, bv, 0.0)
        kh = jnp.where(valid, kh, 0.0)
        v = jnp.where(valid, v, 0.0)

    yield
    causal_c = jnp.where(_iota((C, C), 0) >= _iota((C, C), 1), 1.0, 0.0).astype(BF16)
    cl = _dot_sel3(causal_c, lw)
    e_pos = jnp.exp(cl)
    e_neg = jnp.exp(-cl)
    rt = r * e_pos
    at = av * jnp.exp(cl - lw)
    bt = bv * e_neg
    kt = kh * e_neg
    p_end = e_pos[C - 1:C, :]

    R2 = 2 * C
    lane_lo = _iota((C, LANES), 1) < HD_RWKV
    rr = _iota((R2, R2), 0)
    cc = _iota((R2, R2), 1)
    strict = rr > cc
    incl = rr >= cc
    eye = (rr == cc).astype(F32)

    def stack(x):
        return jnp.concatenate([jnp.where(lane_lo, x, 0.0), jnp.where(lane_lo, 0.0, x)], axis=0)

    pairs = range(8)
    sls = [slice(p * LANES, (p + 1) * LANES) for p in pairs]
    a_s = [stack(at[:, sl]) for sl in sls]
    r_s = [stack(rt[:, sl]) for sl in sls]
    b_s = [stack(bt[:, sl]) for sl in sls]
    k_s = [stack(kt[:, sl]) for sl in sls]
    v_s = [stack(v[:, sl]) for sl in sls]
    yield
    s_bd = [state_ref[p] for p in pairs]
    wide = R2 == LANES
    if wide:
        ar = [jnp.concatenate([a_s[p], r_s[p]], axis=0) for p in pairs]
        bk = [jnp.concatenate([b_s[p], k_s[p]], axis=0) for p in pairs]
        quad = [_cmm(ar[p], bk[p], NT) for p in pairs]
        ars = [_cmm(ar[p], s_bd[p], NT) for p in pairs]
        m_ab = [quad[p][:R2, :R2] for p in pairs]
        m_ak = [jnp.where(strict, quad[p][:R2, R2:], 0.0) for p in pairs]
        w = [ars[p][:R2] + _cmm(m_ak[p], v_s[p], NN) for p in pairs]
    else:
        m_ab = [_cmm(a_s[p], b_s[p], NT) for p in pairs] if n_valid > 1 else None
        m_ak = [jnp.where(strict, _cmm(a_s[p], k_s[p], NT), 0.0) for p in pairs]
        w = [_cmm(a_s[p], s_bd[p], NT) + _cmm(m_ak[p], v_s[p], NN) for p in pairs]
    yield
    if n_valid > 1:
        x = [jnp.where(strict, m_ab[p], 0.0) for p in pairs]
        t = [eye + x[p] for p in pairs]
        span = 2
        while span < C:
            x = [_cmm(x[p], x[p], NN) for p in pairs]
            t = [t[p] + _cmm(t[p], x[p], NN) for p in pairs]
            span *= 2
            yield
        u = [_cmm(t[p], w[p], NN) for p in pairs]
    else:
        u = w
    yield
    ys = []
    for p in pairs:
        if wide:
            n_r = jnp.where(jnp.concatenate([incl, incl], axis=1), quad[p][R2:, :], 0.0)
            uv = jnp.concatenate([u[p], v_s[p]], axis=0)
            y2 = ars[p][R2:] + _cmm(n_r, uv, NN)
            upd = _cmm(uv, bk[p], TN)
        else:
            n_rb = jnp.where(incl, _cmm(r_s[p], b_s[p], NT), 0.0)
            n_rk = jnp.where(incl, _cmm(r_s[p], k_s[p], NT), 0.0)
            y2 = _cmm(r_s[p], s_bd[p], NT) + _cmm(n_rb, u[p], NN) + _cmm(n_rk, v_s[p], NN)
            upd = _cmm(u[p], b_s[p], TN) + _cmm(v_s[p], k_s[p], TN)
        ys.append(y2[0:C] + y2[C:R2])
        state_ref[p] = (s_bd[p] + upd) * p_end[:, sls[p]]
    yield
    yr = jnp.concatenate(ys, axis=1)
    mean = _head_sum(yr, seg) * (1.0 / HD_RWKV)
    d = yr - mean
    var = _head_sum(d * d, seg) * (1.0 / HD_RWKV)
    yn = d * lax.rsqrt(var + RWKV_LN_EPS) * lnw_ref[...] + lnb_ref[...]
    y_ref[...] = (yn + bonus) * g


def _rwkv(proj3, shift_prev8, wkv_prev, p, *, C, n_valid, nb):
    b, lp, _ = proj3.shape
    nc = lp // C
    full = lambda shape: pl.BlockSpec(shape, lambda i, c: (0,) * len(shape))
    state = pl.BlockSpec((nb, H_RWKV, HD_RWKV, HD_RWKV), lambda i, c: (i, 0, 0, 0))
    body = _per_sequence(nb, [True] * 6 + [False] * 12 + [True] * 4, _rwkv_init,
                         functools.partial(_rwkv_main, C=C, n_valid=n_valid), _rwkv_fini)
    return pl.pallas_call(
        body,
        grid=(b // nb, nc),
        in_specs=[
            pl.BlockSpec((nb, C, 1024), lambda i, c: (i, c, COL_R)),
            pl.BlockSpec((nb, C, 1024), lambda i, c: (i, c, COL_K)),
            pl.BlockSpec((nb, C, 1024), lambda i, c: (i, c, COL_V)),
            pl.BlockSpec((nb, C, 256), lambda i, c: (i, c, COL_LR)),
            pl.BlockSpec((nb, 8, RWKV_PROJ), lambda i, c: (i, 0, 0)),
            state,
            full((1, RWKV_PROJ)), full((1, 1024)), full((2, 128, 1024)), full((1, 1024)), full((2, 128, 1024)),
            full((2, 128, 1024)), full((1, 1024)), full((1, 1024)), full((1, 1024)), full((1, 1024)), full((1, 1024)),
            full((128, 128)),
        ],
        out_specs=[pl.BlockSpec((nb, C, 1024), lambda i, c: (i, c, 0)), state],
        out_shape=[jax.ShapeDtypeStruct((b, lp, 1024), F32), jax.ShapeDtypeStruct((b, H_RWKV, HD_RWKV, HD_RWKV), F32)],
        scratch_shapes=[pltpu.VMEM((nb, C + 8, RWKV_PROJ), F32), pltpu.VMEM((nb, 8, 128, 128), F32)],
        compiler_params=_cparams(("parallel", "arbitrary")),
        name="rwkv",
    )(proj3, proj3, proj3, proj3, shift_prev8, wkv_prev, p["rwkv_mu"], p["rwkv_w0"], p["rwkv_w2p"], p["rwkv_a0"],
      p["rwkv_a2p"], p["rwkv_g2"], p["rwkv_k_k"], p["rwkv_k_a"], p["rwkv_r_k"], p["rwkv_ln_w"], p["rwkv_ln_b"],
      p["seg"])


def _merge_body(x_ref, gs_ref, gr_ref, ys_ref, yr_ref, w_ref, o_ref):
    merged = _sigmoid(gs_ref[...]) * ys_ref[...] + _sigmoid(gr_ref[...]) * yr_ref[...]
    o_ref[...] = x_ref[...] + _dot(merged.astype(BF16), w_ref[...])


def _merge_out(x, proj, y_ssd, y_rwkv, w_out, *, tm=512):
    m = x.shape[0]
    tm = min(tm, m)
    row = lambda j: pl.BlockSpec((tm, 1024), lambda i: (i, j))
    return pl.pallas_call(
        _merge_body,
        grid=(m // tm,),
        in_specs=[row(0), row(COL_GS), row(COL_GR), row(0), row(0), pl.BlockSpec((1024, 1024), lambda i: (0, 0))],
        out_specs=row(0),
        out_shape=jax.ShapeDtypeStruct((m, 1024), F32),
        compiler_params=_cparams(("parallel",)),
        name="merge_out",
    )(x, proj, proj, y_ssd, y_rwkv, w_out.astype(BF16))


def _mem_attention(q, k_ref, v_ref):
    outs = []
    for h in range(MEM_HEADS):
        sl = slice(h * MEM_HD, (h + 1) * MEM_HD)
        s = _dot_nt(q[:, sl].astype(BF16), k_ref[:, sl].astype(BF16)) * (MEM_HD ** -0.5)
        s = s - jnp.max(s, axis=-1, keepdims=True)
        e = jnp.exp(s)
        prob = e / jnp.sum(e, axis=-1, keepdims=True)
        outs.append(_dot(prob.astype(BF16), v_ref[:, sl].astype(BF16)))
    return jnp.concatenate(outs, axis=1)


def _tail_body(x_ref, gs_ref, gr_ref, ys_ref, yr_ref, k_ref, v_ref, wout_ref, nmem_ref, wmq_ref, wmo_ref,
               nffn_ref, wpq_ref, x2_ref, pq_ref):
    merged = _sigmoid(gs_ref[...]) * ys_ref[...] + _sigmoid(gr_ref[...]) * yr_ref[...]
    x1 = x_ref[...] + _dot(merged.astype(BF16), wout_ref[...])
    q = _dot(_rmsnorm(x1, nmem_ref[...]).astype(BF16), wmq_ref[...])
    x2 = x1 + _dot(_mem_attention(q, k_ref, v_ref).astype(BF16), wmo_ref[...])
    x2_ref[...] = x2
    pq_ref[...] = _dot(_rmsnorm(x2, nffn_ref[...]).astype(BF16), wpq_ref[...])


def _tail(x3, proj3, y_ssd, y_rwkv, mem_k, mem_v, p, *, tm=512):
    b, L, _ = x3.shape
    once = pl.Buffered(1)
    tile = lambda j: pl.BlockSpec((None, tm, 1024), lambda i, t: (i, t, j))
    mem = pl.BlockSpec((None, N_MEM, 1024), lambda i, t: (i, 0, 0))
    full = lambda r, c: pl.BlockSpec((r, c), lambda i, t: (0, 0), pipeline_mode=once)
    return pl.pallas_call(
        _tail_body,
        grid=(b, L // tm),
        in_specs=[tile(0), tile(COL_GS), tile(COL_GR), tile(0), tile(0), mem, mem,
                  full(1024, 1024), full(1, 1024), full(1024, 1024), full(1024, 1024), full(1, 1024),
                  full(1024, 2048)],
        out_specs=[tile(0), pl.BlockSpec((None, tm, 2048), lambda i, t: (i, t, 0))],
        out_shape=[jax.ShapeDtypeStruct((b, L, 1024), F32), jax.ShapeDtypeStruct((b, L, 2048), F32)],
        compiler_params=_cparams(("parallel", "arbitrary")),
        name="token_tail",
    )(x3, proj3, proj3, y_ssd, y_rwkv, mem_k, mem_v, p["w_out"].astype(BF16), p["norm_mem_w"].reshape(1, 1024),
      p["w_mq"].astype(BF16), p["w_mo"].astype(BF16), p["norm_ffn_w"].reshape(1, 1024), p["w_pq"].astype(BF16))


def _attn_step_body(q_ref, k_ref, v_ref, o_ref):
    s = jnp.sum(k_ref[...] * q_ref[...][:, None], axis=-1, keepdims=True) * (MEM_HD ** -0.5)
    e = jnp.exp(s - jnp.max(s, axis=1, keepdims=True))
    prob = e / jnp.sum(e, axis=1, keepdims=True)
    o_ref[...] = jnp.sum(prob * v_ref[...], axis=1)


def _attn_step(q, mem_k, mem_v, *, nb=2):
    b = q.shape[0]
    qspec = pl.BlockSpec((nb, MEM_HEADS, MEM_HD), lambda i: (i, 0, 0))
    kvspec = pl.BlockSpec((nb, N_MEM, MEM_HEADS, MEM_HD), lambda i: (i, 0, 0, 0))
    return pl.pallas_call(
        _attn_step_body,
        grid=(b // nb,),
        in_specs=[qspec, kvspec, kvspec],
        out_specs=qspec,
        out_shape=jax.ShapeDtypeStruct((b, MEM_HEADS, MEM_HD), F32),
        compiler_params=_cparams(("parallel",)),
        name="mem_attn_step",
    )(q, mem_k, mem_v)


def _stage2_pairs():
    return [(r1, r2) for r1 in range(PEER_TOPK) for r2 in range(PEER_TOPK) if (r1 + 1) * (r2 + 1) <= PEER_TOPK]


N_CAND = 56


def _topk_body(q_ref, keys_ref, gate_ref, i1_ref, i2_ref, s_scr, i_scr, cand_scr, code_scr, res_scr, *, T):
    neg = -jnp.inf
    key_row = _iota((N_KEYS, T), 0).astype(F32)
    cand_row = _iota((N_CAND, T), 0).astype(F32)
    pairs = _stage2_pairs()
    q = q_ref[...]
    for h in range(PEER_HEADS):
        for c in range(2):
            hc = 2 * h + c
            s = _dot_nt(keys_ref[hc], q[:, hc * LANES:(hc + 1) * LANES], HI)
            for r in range(PEER_TOPK):
                m = jnp.max(s, axis=0, keepdims=True)
                idx = jnp.min(jnp.where(s == m, key_row, float(N_KEYS)), axis=0, keepdims=True)
                s_scr[c, r:r + 1, :] = m
                i_scr[c, r:r + 1, :] = idx
                s = jnp.where(key_row == idx, neg, s)
        cand_scr[...] = jnp.full((N_CAND, T), neg, F32)
        code_scr[...] = jnp.zeros((N_CAND, T), F32)
        for j, (r1, r2) in enumerate(pairs):
            cand_scr[j:j + 1, :] = s_scr[0, r1:r1 + 1, :] + s_scr[1, r2:r2 + 1, :]
            code_scr[j:j + 1, :] = i_scr[0, r1:r1 + 1, :] * float(N_KEYS) + i_scr[1, r2:r2 + 1, :]
        cand = cand_scr[...]
        code = code_scr[...]
        for r in range(PEER_TOPK):
            m = jnp.max(cand, axis=0, keepdims=True)
            idx = jnp.min(jnp.where(cand == m, cand_row, float(N_CAND)), axis=0, keepdims=True)
            hit = cand_row == idx
            res_scr[0, r:r + 1, :] = m
            res_scr[1, r:r + 1, :] = jnp.sum(jnp.where(hit, code, 0.0), axis=0, keepdims=True)
            cand = jnp.where(hit, neg, cand)
        top = res_scr[0]
        e = jnp.exp(top - jnp.max(top, axis=0, keepdims=True))
        rows = slice(h * PEER_TOPK, (h + 1) * PEER_TOPK)
        gate_ref[rows, :] = e / jnp.sum(e, axis=0, keepdims=True)
        first = jnp.floor(res_scr[1] * (1.0 / N_KEYS))
        i1_ref[rows, :] = first
        i2_ref[rows, :] = res_scr[1] - first * float(N_KEYS)


def _peer_topk(q, sub_keys, *, T=256):
    m = q.shape[0]
    T = min(T, m)
    body = functools.partial(_topk_body, T=T)
    out = pl.BlockSpec((128, T), lambda i: (0, i))
    return pl.pallas_call(
        body,
        grid=(m // T,),
        in_specs=[pl.BlockSpec((T, 2048), lambda i: (i, 0)), pl.BlockSpec((16, 128, 128), lambda i: (0, 0, 0))],
        out_specs=[out, out, out],
        out_shape=[jax.ShapeDtypeStruct((128, m), F32)] * 3,
        scratch_shapes=[pltpu.VMEM((2, PEER_TOPK, T), F32), pltpu.VMEM((2, PEER_TOPK, T), F32),
                        pltpu.VMEM((N_CAND, T), F32), pltpu.VMEM((N_CAND, T), F32),
                        pltpu.VMEM((2, PEER_TOPK, T), F32)],
        compiler_params=_cparams(("parallel",)),
        name="peer_topk",
    )(q, sub_keys.reshape(16, 128, 128))


I1_PER_STEP = 8
G_PITCH = 136


def _peer_body(x_ref, gate_ref, i1_ref, i2_ref, nw_ref, fw_ref, ut_ref, v_ref, o_ref, hn_scr, g_scr, coef_scr, *, T):
    j = pl.program_id(1)

    @pl.when(j == 0)
    def _():
        hn_scr[...] = _rmsnorm(x_ref[...], nw_ref[...]).astype(BF16)
        o_ref[...] = jnp.zeros_like(o_ref)
        shape = (N_KEYS, LANES)
        key_row = _iota(shape, 0).astype(F32).astype(BF16)
        one = jnp.ones(shape, BF16)
        zero = jnp.zeros(shape, BF16)

        def build(t, carry):
            row = lambda ref: jnp.broadcast_to(ref[pl.ds(t, 1), :].astype(BF16), shape)
            a_t = jnp.where(key_row == row(i1_ref), row(gate_ref), zero)
            b_t = jnp.where(key_row == row(i2_ref), one, zero)
            g_scr[pl.ds(pl.multiple_of(t * G_PITCH, 8), N_KEYS), :] = _dot_nt(a_t, b_t)
            return carry

        lax.fori_loop(0, T, build, 0, unroll=32)

    hn = hn_scr[...]
    for cidx in range(I1_PER_STEP // 2):
        act = _dot(hn, ut_ref[:, cidx * 2 * LANES:(cidx + 1) * 2 * LANES])
        ge = 0.5 * act * (1.0 + lax.erf(act * (2.0 ** -0.5)))
        for half in range(2):
            ii = 2 * cidx + half
            g_t = g_scr[pl.ds(j * I1_PER_STEP + ii, T, stride=G_PITCH), :]
            coef_scr[:, ii * LANES:(ii + 1) * LANES] = (g_t * ge[:, half * LANES:(half + 1) * LANES]).astype(BF16)
    o_ref[...] += _dot(coef_scr[...], v_ref[...])

    @pl.when(j == pl.num_programs(1) - 1)
    def _():
        o_ref[...] = _rmsnorm(x_ref[...] + o_ref[...], fw_ref[...])


def _peer_mix(x, gate, i1, i2, norm_w, final_w, ut_bf16, v_bf16, *, T=512):
    m = x.shape[0]
    T = min(T, m)
    ne = I1_PER_STEP * N_KEYS
    body = functools.partial(_peer_body, T=T)
    once = pl.Buffered(1)
    row = lambda w: pl.BlockSpec((T, w), lambda i, j: (i, 0), pipeline_mode=once)
    vec = pl.BlockSpec((1, 1024), lambda i, j: (0, 0), pipeline_mode=once)
    return pl.pallas_call(
        body,
        grid=(m // T, N_KEYS // I1_PER_STEP),
        in_specs=[row(1024), row(128), row(128), row(128), vec, vec,
                  pl.BlockSpec((1024, ne), lambda i, j: (0, j)), pl.BlockSpec((ne, 1024), lambda i, j: (j, 0))],
        out_specs=pl.BlockSpec((T, 1024), lambda i, j: (i, 0)),
        out_shape=jax.ShapeDtypeStruct((m, 1024), F32),
        scratch_shapes=[pltpu.VMEM((T, 1024), BF16), pltpu.VMEM((T * G_PITCH, LANES), F32),
                        pltpu.VMEM((T, 1024), BF16)],
        compiler_params=_cparams(("parallel", "arbitrary")),
        name="peer_mix",
    )(x, gate, i1, i2, norm_w.reshape(1, 1024), final_w.reshape(1, 1024), ut_bf16, v_bf16)


def _pad_rows(x, rows):
    return jnp.pad(x, ((0, 0), (rows - x.shape[1], 0), (0, 0)))


def _block(x, mem_k, mem_v, conv_prev, ssm_prev, shift_prev, wkv_prev, p, *, ssd_l, rwkv_c):
    b, L, _ = x.shape
    m = b * L
    x2d = x.reshape(m, D_MODEL)
    proj = _linear(x2d, p["w_in"], norm_w=p["norm_mix_w"], tm=2048, tn=1152, name="in_proj")
    proj3 = proj.reshape(b, L, D_IN_PAD)
    lp = max(L, SUBLANES)
    n_valid = L if L < SUBLANES else max(ssd_l, rwkv_c)
    proj3p = proj3 if lp == L else jnp.pad(proj3, ((0, 0), (0, lp - L), (0, 0)))

    nb = SEQS_PER_STEP if L == 1 else 1
    y_ssd, ssm_new = _ssd(proj3p, _pad_rows(conv_prev, 8), ssm_prev, p,
                          l=ssd_l, n_valid=min(n_valid, ssd_l), nb=nb)
    y_rwkv, wkv_new = _rwkv(proj3p, _pad_rows(shift_prev[:, None, :], 8), wkv_prev, p,
                            C=rwkv_c, n_valid=min(n_valid, rwkv_c), nb=nb)
    tail = proj3[:, max(L - (CONV_K - 1), 0):]
    xbc = jnp.concatenate([tail[..., 1024:2048], tail[..., 7168:7680]], axis=-1)
    rw = jnp.concatenate([tail[..., 2048:5120], tail[..., 7680:7936]], axis=-1)
    conv_new = jnp.concatenate([conv_prev, xbc], axis=1)[:, -(CONV_K - 1):]
    shift_new = rw[:, -1]

    if L == 1:
        x1 = _merge_out(x2d, proj, y_ssd[:, 0], y_rwkv[:, 0], p["w_out"])
        q = _linear(x1, p["w_mq"], norm_w=p["norm_mem_w"], name="mem_q")
        o = _attn_step(q.reshape(b, MEM_HEADS, MEM_HD), mem_k, mem_v)
        x2 = _linear(o.reshape(m, D_MODEL), p["w_mo"], res=x1, name="mem_o")
        pq = _linear(x2, p["w_pq"], norm_w=p["norm_ffn_w"], name="peer_q")
    else:
        x2, pq = _tail(x, proj3, y_ssd, y_rwkv, mem_k.reshape(b, N_MEM, D_MODEL), mem_v.reshape(b, N_MEM, D_MODEL), p)
        x2, pq = x2.reshape(m, D_MODEL), pq.reshape(m, 2 * D_MODEL)
    gate_t, i1_t, i2_t = _peer_topk(pq, p["sub_keys"])
    y = _peer_mix(x2, gate_t.T, i1_t.T, i2_t.T, p["norm_ffn_w"], p["norm_final_w"], p["expert_ut"], p["expert_v"])
    return y.reshape(b, L, D_MODEL), conv_new, ssm_new, shift_new, wkv_new


def kernel(x_prompt, x_sample, mem_prompt, state_ssm, state_conv, state_wkv, state_shift, cache_mem_k, cache_mem_v, norm_mix_w, w_in, conv_w, conv_b, dt_bias, a_log, d_skip, ssd_norm_w, rwkv_mu, rwkv_w0, rwkv_w2, rwkv_a0, rwkv_a2, rwkv_g2, rwkv_k_k, rwkv_k_a, rwkv_r_k, rwkv_ln_w, rwkv_ln_b, w_out, norm_mem_w, mem_norm_w, w_mk, w_mv, w_mq, w_mo, norm_ffn_w, w_pq, sub_keys, expert_u, expert_v, norm_final_w):
    assert w_in.shape[0] == 1, "single-layer problem"
    bp = x_prompt.shape[0]
    bs = x_sample.shape[0]
    w_in_perm = _permute_w_in(w_in[0])
    pad16 = lambda a: jnp.pad(a.reshape(1, H_SSD), ((0, 0), (0, LANES - H_SSD)))
    head_of_lane = jnp.arange(D_MODEL) // SSD_HEADDIM
    zeros64 = jnp.zeros((64, D_MODEL), F32)
    hi_lo = lambda w: jnp.stack(_split_bf16(w))
    p = {
        "norm_mix_w": norm_mix_w[0], "w_in": w_in_perm,
        "conv_w": conv_w[0], "conv_b": conv_b[0].reshape(1, CONV_DIM),
        "dt_bias": pad16(dt_bias[0]), "a_log": pad16(a_log[0]),
        "d_skip_e": jnp.repeat(d_skip[0], SSD_HEADDIM).reshape(1, D_MODEL),
        "ssd_norm_w": ssd_norm_w[0].reshape(1, D_MODEL),
        "expand": (jnp.arange(LANES)[:, None] == head_of_lane[None, :]).astype(BF16),
        "rwkv_mu": rwkv_mu[0].reshape(1, RWKV_PROJ), "rwkv_w0": rwkv_w0[0].reshape(1, D_MODEL),
        "rwkv_w2p": hi_lo(jnp.concatenate([rwkv_w2[0], zeros64], axis=0)),
        "rwkv_a0": rwkv_a0[0].reshape(1, D_MODEL),
        "rwkv_a2p": hi_lo(jnp.concatenate([zeros64, rwkv_a2[0]], axis=0)),
        "rwkv_g2": hi_lo(rwkv_g2[0]), "rwkv_k_k": rwkv_k_k[0].reshape(1, D_MODEL),
        "rwkv_k_a": rwkv_k_a[0].reshape(1, D_MODEL), "rwkv_r_k": rwkv_r_k[0].reshape(1, D_MODEL),
        "rwkv_ln_w": rwkv_ln_w[0].reshape(1, D_MODEL), "rwkv_ln_b": rwkv_ln_b[0].reshape(1, D_MODEL),
        "seg": (head_of_lane[:LANES, None] == head_of_lane[None, :LANES]).astype(BF16),
        "w_out": w_out[0], "norm_mem_w": norm_mem_w[0], "w_mq": w_mq[0], "w_mo": w_mo[0],
        "norm_ffn_w": norm_ffn_w[0], "w_pq": w_pq[0], "sub_keys": sub_keys[0],
        "expert_ut": expert_u[0].astype(BF16).T, "expert_v": expert_v[0].astype(BF16),
        "norm_final_w": norm_final_w,
    }

    mem2d = mem_prompt.reshape(bp * N_MEM, D_MODEL)
    mem_k = _linear(mem2d, w_mk[0], norm_w=mem_norm_w[0], name="mem_k").reshape(bp, N_MEM, MEM_HEADS, MEM_HD)
    mem_v = _linear(mem2d, w_mv[0], norm_w=mem_norm_w[0], name="mem_v").reshape(bp, N_MEM, MEM_HEADS, MEM_HD)
    y_p, conv_p, ssm_p, shift_p, wkv_p = _block(
        x_prompt, mem_k, mem_v,
        jnp.zeros((bp, CONV_K - 1, CONV_DIM), F32), jnp.zeros((bp, H_SSD, SSD_HEADDIM, SSD_STATE), F32),
        jnp.zeros((bp, RWKV_PROJ), F32), jnp.zeros((bp, H_RWKV, HD_RWKV, HD_RWKV), F32), p,
        ssd_l=SSD_CHUNK, rwkv_c=RWKV_CHUNK)
    y_s, conv_s, ssm_s, shift_s, wkv_s = _block(
        x_sample, cache_mem_k[0], cache_mem_v[0], state_conv[0], state_ssm[0], state_shift[0], state_wkv[0], p,
        ssd_l=SUBLANES, rwkv_c=SUBLANES)
    return (y_p, y_s, ssm_p[None], conv_p[None], wkv_p[None], shift_p[None], mem_k[None], mem_v[None],
            ssm_s[None], conv_s[None], wkv_s[None], shift_s[None])
```

```python
import functools

import jax
import jax.numpy as jnp
from jax import lax
from jax.experimental import pallas as pl
from jax.experimental.pallas import tpu as pltpu

F32 = jnp.float32
BF16 = jnp.bfloat16
HI = lax.Precision.HIGHEST

D_MODEL = 1024
H_SSD = 16
SSD_HEADDIM = 64
SSD_STATE = 128
CONV_K = 4
CONV_DIM = 1536
H_RWKV = 16
HD_RWKV = 64
RWKV_PROJ = 3328
RWKV_LN_EPS = 64e-5
N_MEM = 256
MEM_HEADS = 4
MEM_HD = 256
N_KEYS = 128
PEER_HEADS = 8
PEER_TOPK = 16
NORM_EPS = 1e-6

LANES = 128
SUBLANES = 8
VMEM_LIMIT = 56 * 1024 * 1024

D_IN_PAD = 8064
COL_Z, COL_XS, COL_R, COL_K, COL_V, COL_GS, COL_GR = 0, 1, 2, 3, 4, 5, 6
COL_BC = 14
COL_LR = 30
COL_DT = 62

SSD_CHUNK = 128
RWKV_CHUNK = 64
SEQS_PER_STEP = 8


def _cparams(sem):
    return pltpu.CompilerParams(dimension_semantics=sem, vmem_limit_bytes=VMEM_LIMIT)


def _dot(a, b, prec=None):
    return jnp.dot(a, b, preferred_element_type=F32, precision=prec)


def _dot_nt(a, b, prec=None):
    return lax.dot_general(a, b, (((1,), (1,)), ((), ())), preferred_element_type=F32, precision=prec)


def _dot_tn(a, b, prec=None):
    return lax.dot_general(a, b, (((0,), (0,)), ((), ())), preferred_element_type=F32, precision=prec)


def _split_bf16(x):
    hi = x.astype(BF16)
    lo = (x - hi.astype(F32)).astype(BF16)
    return hi, lo


def _dot_onehot(x, sel_bf16):
    hi, lo = _split_bf16(x)
    return _dot(hi, sel_bf16) + _dot(lo, sel_bf16)


NN = ((1,), (0,))
NT = ((1,), (1,))
TN = ((0,), (0,))


def _cmm(a, b, dims):
    return lax.dot_general(a.astype(BF16), b.astype(BF16), (dims, ((), ())), preferred_element_type=F32)


def _softplus(x):
    return jnp.maximum(x, 0.0) + jnp.log1p(jnp.exp(-jnp.abs(x)))


def _sigmoid(x):
    return 1.0 / (1.0 + jnp.exp(-x))


def _rmsnorm(x, w):
    return x * lax.rsqrt(jnp.mean(x * x, axis=-1, keepdims=True) + NORM_EPS) * w


def _iota(shape, dim):
    return lax.broadcasted_iota(jnp.int32, shape, dim)


def _per_sequence(nb, per_seq, init, main, fini=None):
    def wrapped(*refs):
        views = [[r.at[s] if flag else r for r, flag in zip(refs, per_seq)] for s in range(nb)]
        c = pl.program_id(1)

        @pl.when(c == 0)
        def _():
            for v in views:
                init(*v)

        live = [main(*v) for v in views]
        while live:
            live = [g for g in live if next(g, True) is None]
        if fini is not None:
            @pl.when(c == pl.num_programs(1) - 1)
            def _():
                for v in views:
                    fini(*v)
    return wrapped


def _linear_body(*refs, has_norm, has_res):
    it = iter(refs)
    x_ref = next(it)
    nw_ref = next(it) if has_norm else None
    w_ref = next(it)
    res_ref = next(it) if has_res else None
    o_ref = next(it)
    x = x_ref[...]
    if has_norm:
        x = _rmsnorm(x, nw_ref[...])
    acc = _dot(x.astype(BF16), w_ref[...])
    if has_res:
        acc = acc + res_ref[...]
    o_ref[...] = acc


def _linear(x, w, *, norm_w=None, res=None, tm=512, tn=1024, name="linear"):
    m, k = x.shape
    n = w.shape[1]
    tm = min(tm, m)
    tn = min(tn, n)
    assert m % tm == 0 and n % tn == 0
    args = [x]
    specs = [pl.BlockSpec((tm, k), lambda i, j: (i, 0))]
    if norm_w is not None:
        args.append(norm_w.reshape(1, k))
        specs.append(pl.BlockSpec((1, k), lambda i, j: (0, 0)))
    args.append(w.astype(BF16))
    specs.append(pl.BlockSpec((k, tn), lambda i, j: (0, j)))
    if res is not None:
        args.append(res)
        specs.append(pl.BlockSpec((tm, tn), lambda i, j: (i, j)))
    body = functools.partial(_linear_body, has_norm=norm_w is not None, has_res=res is not None)
    return pl.pallas_call(
        body,
        grid=(m // tm, n // tn),
        in_specs=specs,
        out_specs=pl.BlockSpec((tm, tn), lambda i, j: (i, j)),
        out_shape=jax.ShapeDtypeStruct((m, n), F32),
        compiler_params=_cparams(("parallel", "arbitrary")),
        name=name,
    )(*args)


W_IN_SEGMENTS = ((0, 2048), (2576, 5648), (5904, 7952), (2048, 2560), (5648, 5904), (2560, 2576))


def _permute_body(w_ref, o_ref):
    w = w_ref[...]
    parts = [w[:, a:b] for a, b in W_IN_SEGMENTS]
    parts.append(jnp.zeros((w.shape[0], LANES - H_SSD), F32))
    o_ref[...] = jnp.concatenate(parts, axis=1).astype(BF16)


def _permute_w_in(w):
    k, n = w.shape
    tr = 128
    return pl.pallas_call(
        _permute_body,
        grid=(k // tr,),
        in_specs=[pl.BlockSpec((tr, n), lambda i: (i, 0))],
        out_specs=pl.BlockSpec((tr, D_IN_PAD), lambda i: (i, 0)),
        out_shape=jax.ShapeDtypeStruct((k, D_IN_PAD), BF16),
        compiler_params=_cparams(("parallel",)),
        name="permute_w_in",
    )(w)


def _ssd_init(z_ref, xs_ref, bc_ref, dt_ref, convprev_ref, ssmprev_ref, convw_ref, convb_ref, dtb_ref,
              alog_ref, dskip_ref, normw_ref, expand_ref, y_ref, state_ref, ext_ref):
    ext_ref[0:8, :] = convprev_ref[...]
    state_ref[...] = ssmprev_ref[...]


def _ssd_main(z_ref, xs_ref, bc_ref, dt_ref, convprev_ref, ssmprev_ref, convw_ref, convb_ref, dtb_ref,
              alog_ref, dskip_ref, normw_ref, expand_ref, y_ref, state_ref, ext_ref, *, l, n_valid):
    ext_ref[8:8 + l, 0:D_MODEL] = xs_ref[...]
    ext_ref[8:8 + l, D_MODEL:CONV_DIM] = bc_ref[...]
    conv = convb_ref[...]
    for i in range(CONV_K):
        conv = conv + ext_ref[pl.ds(5 + i, l), :] * convw_ref[i:i + 1, :]
    ext_ref[0:8, :] = ext_ref[l:l + 8, :]

    yield
    act = conv * _sigmoid(conv)
    xs = act[:, 0:D_MODEL]
    bm = act[:, D_MODEL:D_MODEL + 256]
    cm = act[:, D_MODEL + 256:CONV_DIM]

    dt = _softplus(dt_ref[...] + dtb_ref[...])
    if n_valid < l:
        dt = jnp.where(_iota((l, LANES), 0) < n_valid, dt, 0.0)
    ad = dt * (-jnp.exp(alog_ref[...]))
    row = _iota((l, l), 0)
    col = _iota((l, l), 1)
    causal = row >= col
    cs = _dot_sel3(jnp.where(causal, 1.0, 0.0).astype(BF16), ad)
    eye = jnp.where(_iota((LANES, LANES), 0) == _iota((LANES, LANES), 1), 1.0, 0.0).astype(BF16)
    cs_parts = _split3(cs)
    cs_t = sum(_dot_nt(eye, part) for part in cs_parts)
    expand = expand_ref[...]
    dt_e = sum(_dot(part, expand) for part in _split3(dt))
    cs_e = sum(_dot(part, expand) for part in cs_parts)
    yield
    xdt = xs * dt_e
    decay_in = jnp.exp(cs_e)
    decay_out = jnp.exp(cs_e[l - 1:l, :] - cs_e)
    xdt_dec = (xdt * decay_out).astype(BF16)
    lane_lo = _iota((l, LANES), 1) < SSD_HEADDIM
    row_lo = _iota((LANES, 1), 0) < SSD_HEADDIM

    ys = []
    for g in range(2):
        bg = bm[:, g * LANES:(g + 1) * LANES].astype(BF16)
        cg = cm[:, g * LANES:(g + 1) * LANES].astype(BF16)
        cb = _dot_nt(cg, bg)
        for pr in range(4 * g, 4 * g + 4):
            sl = slice(pr * LANES, (pr + 1) * LANES)
            h0, h1 = 2 * pr, 2 * pr + 1
            m0 = jnp.where(causal, jnp.exp(cs[:, h0:h0 + 1] - cs_t[h0:h0 + 1, :]), 0.0) * cb
            m1 = jnp.where(causal, jnp.exp(cs[:, h1:h1 + 1] - cs_t[h1:h1 + 1, :]), 0.0) * cb
            xp = xdt[:, sl]
            x_lo = jnp.where(lane_lo, xp, 0.0).astype(BF16)
            x_hi = jnp.where(lane_lo, 0.0, xp).astype(BF16)
            y_diag = _dot(m0.astype(BF16), x_lo) + _dot(m1.astype(BF16), x_hi)
            s_pair = state_ref[2 * pr:2 * pr + 2].reshape(LANES, SSD_STATE)
            y_off = _dot_nt(cg, s_pair.astype(BF16)) * decay_in[:, sl]
            ys.append(y_diag + y_off + dskip_ref[:, sl] * xs[:, sl])
            upd = _dot_tn(xdt_dec[:, sl], bg)
            last = cs[l - 1:l, :]
            row_decay = jnp.exp(jnp.where(row_lo, last[:, h0:h0 + 1], last[:, h1:h1 + 1]))
            state_ref[2 * pr:2 * pr + 2] = (row_decay * s_pair + upd).reshape(2, SSD_HEADDIM, SSD_STATE)
            yield
    y = jnp.concatenate(ys, axis=1)
    z = z_ref[...]
    y_ref[...] = _rmsnorm(y * (z * _sigmoid(z)), normw_ref[...])


def _ssd(proj3, conv_prev8, ssm_prev, p, *, l, n_valid, nb):
    b, lp, _ = proj3.shape
    nc = lp // l
    full = lambda shape: pl.BlockSpec(shape, lambda i, c: (0,) * len(shape))
    state = pl.BlockSpec((nb, H_SSD, SSD_HEADDIM, SSD_STATE), lambda i, c: (i, 0, 0, 0))
    body = _per_sequence(nb, [True] * 6 + [False] * 7 + [True] * 3, _ssd_init,
                         functools.partial(_ssd_main, l=l, n_valid=n_valid))
    return pl.pallas_call(
        body,
        grid=(b // nb, nc),
        in_specs=[
            pl.BlockSpec((nb, l, 1024), lambda i, c: (i, c, COL_Z)),
            pl.BlockSpec((nb, l, 1024), lambda i, c: (i, c, COL_XS)),
            pl.BlockSpec((nb, l, 512), lambda i, c: (i, c, COL_BC)),
            pl.BlockSpec((nb, l, 128), lambda i, c: (i, c, COL_DT)),
            pl.BlockSpec((nb, 8, CONV_DIM), lambda i, c: (i, 0, 0)),
            state,
            full((CONV_K, CONV_DIM)), full((1, CONV_DIM)), full((1, 128)), full((1, 128)),
            full((1, 1024)), full((1, 1024)), full((128, 1024)),
        ],
        out_specs=[pl.BlockSpec((nb, l, 1024), lambda i, c: (i, c, 0)), state],
        out_shape=[jax.ShapeDtypeStruct((b, lp, 1024), F32),
                   jax.ShapeDtypeStruct((b, H_SSD, SSD_HEADDIM, SSD_STATE), F32)],
        scratch_shapes=[pltpu.VMEM((nb, l + 8, CONV_DIM), F32)],
        compiler_params=_cparams(("parallel", "arbitrary")),
        name="ssd",
    )(proj3, proj3, proj3, proj3, conv_prev8, ssm_prev, p["conv_w"], p["conv_b"], p["dt_bias"], p["a_log"],
      p["d_skip_e"], p["ssd_norm_w"], p["expand"])


def _dot_w(x, w_ref):
    xh, xl = _split_bf16(x)
    return _dot(xh, w_ref[0]) + _dot(xl, w_ref[0]) + _dot(xh, w_ref[1])


def _split3(x):
    x1 = x.astype(BF16)
    r1 = x - x1.astype(F32)
    x2 = r1.astype(BF16)
    x3 = (r1 - x2.astype(F32)).astype(BF16)
    return x1, x2, x3


def _dot_sel3(sel_bf16, x):
    return sum(_dot(sel_bf16, part) for part in _split3(x))


def _head_sum(x, seg_pair):
    rows = x.shape[0]
    n = x.shape[1] // LANES
    stacked = jnp.concatenate([x[:, i * LANES:(i + 1) * LANES] for i in range(n)], axis=0)
    s = _dot_onehot(stacked, seg_pair)
    return jnp.concatenate([s[i * rows:(i + 1) * rows] for i in range(n)], axis=1)


def _rwkv_init(r_ref, k_ref, v_ref, lr_ref, shprev_ref, wkvprev_ref, mu_ref, w0_ref, w2_ref, a0_ref, a2_ref,
               g2_ref, kk_ref, ka_ref, rk_ref, lnw_ref, lnb_ref, seg_ref, y_ref, wkv_ref, ext_ref, state_ref):
    same_head = (_iota((LANES, LANES), 0) < HD_RWKV) == (_iota((LANES, LANES), 1) < HD_RWKV)
    ext_ref[0:8, :] = shprev_ref[...]
    for p in range(8):
        s_c = wkvprev_ref[2 * p:2 * p + 2].reshape(LANES, HD_RWKV)
        state_ref[p] = jnp.where(same_head, jnp.concatenate([s_c, s_c], axis=1), 0.0)


def _rwkv_fini(r_ref, k_ref, v_ref, lr_ref, shprev_ref, wkvprev_ref, mu_ref, w0_ref, w2_ref, a0_ref, a2_ref,
               g2_ref, kk_ref, ka_ref, rk_ref, lnw_ref, lnb_ref, seg_ref, y_ref, wkv_ref, ext_ref, state_ref):
    for p in range(8):
        s_p = state_ref[p]
        wkv_ref[2 * p] = s_p[0:HD_RWKV, 0:HD_RWKV]
        wkv_ref[2 * p + 1] = s_p[HD_RWKV:LANES, HD_RWKV:LANES]


def _rwkv_main(r_ref, k_ref, v_ref, lr_ref, shprev_ref, wkvprev_ref, mu_ref, w0_ref, w2_ref, a0_ref, a2_ref,
               g2_ref, kk_ref, ka_ref, rk_ref, lnw_ref, lnb_ref, seg_ref, y_ref, wkv_ref, ext_ref, state_ref,
               *, C, n_valid):
    ext_ref[8:8 + C, 0:1024] = r_ref[...]
    ext_ref[8:8 + C, 1024:2048] = k_ref[...]
    ext_ref[8:8 + C, 2048:3072] = v_ref[...]
    ext_ref[8:8 + C, 3072:3328] = lr_ref[...]
    cur = ext_ref[8:8 + C, :]
    prev = ext_ref[pl.ds(7, C), :]
    mixed = cur + (prev - cur) * mu_ref[...]
    ext_ref[0:8, :] = ext_ref[C:C + 8, :]

    yield
    r = mixed[:, 0:1024]
    k = mixed[:, 1024:2048]
    v = mixed[:, 2048:3072]
    wa = mixed[:, 3072:3200]
    gl = mixed[:, 3200:3328]
    seg = seg_ref[...]

    wlog = -_softplus(-(w0_ref[...] + _dot_w(jnp.tanh(wa), w2_ref))) - 0.5
    lw = -jnp.exp(wlog)
    a_sig = _sigmoid(a0_ref[...] + _dot_w(wa, a2_ref))
    g = _dot_w(_sigmoid(gl), g2_ref)
    yield
    kk = k * kk_ref[...]
    kk = kk * lax.rsqrt(jnp.maximum(_head_sum(kk * kk, seg), 1e-24))
    kh = k * (1.0 + (a_sig - 1.0) * ka_ref[...])
    av = -kk
    bv = kk * a_sig
    bonus = _head_sum(r * kh * rk_ref[...], seg) * v
    if n_valid < C:
        valid = _iota((C, 1024), 0) < n_valid
        lw = jnp.where(valid, lw, 0.0)
        av = jnp.where(valid, av, 0.0)
        bv = jnp.where(valid, bv, 0.0)
        kh = jnp.where(valid, kh, 0.0)
        v = jnp.where(valid, v, 0.0)

    yield
    causal_c = jnp.where(_iota((C, C), 0) >= _iota((C, C), 1), 1.0, 0.0).astype(BF16)
    cl = _dot_sel3(causal_c, lw)
    e_pos = jnp.exp(cl)
    e_neg = jnp.exp(-cl)
    rt = r * e_pos
    at = av * jnp.exp(cl - lw)
    bt = bv * e_neg
    kt = kh * e_neg
    p_end = e_pos[C - 1:C, :]

    R2 = 2 * C
    lane_lo = _iota((C, LANES), 1) < HD_RWKV
    rr = _iota((R2, R2), 0)
    cc = _iota((R2, R2), 1)
    strict = rr > cc
    incl = rr >= cc

    def stack(x):
        return jnp.concatenate([jnp.where(lane_lo, x, 0.0), jnp.where(lane_lo, 0.0, x)], axis=0)

    pairs = range(8)
    sls = [slice(p * LANES, (p + 1) * LANES) for p in pairs]
    a_s = [stack(at[:, sl]) for sl in sls]
    r_s = [stack(rt[:, sl]) for sl in sls]
    b_s = [stack(bt[:, sl]) for sl in sls]
    k_s = [stack(kt[:, sl]) for sl in sls]
    v_s = [stack(v[:, sl]) for sl in sls]
    yield
    s_bd = [state_ref[p] for p in pairs]
    wide = R2 == LANES
    if wide:
        ar = [jnp.concatenate([a_s[p], r_s[p]], axis=0) for p in pairs]
        bk = [jnp.concatenate([b_s[p], k_s[p]], axis=0) for p in pairs]
        quad = [_cmm(ar[p], bk[p], NT) for p in pairs]
        ars = [_cmm(ar[p], s_bd[p], NT) for p in pairs]
        m_ab = [quad[p][:R2, :R2] for p in pairs]
        m_ak = [jnp.where(strict, quad[p][:R2, R2:], 0.0) for p in pairs]
        w = [ars[p][:R2] + _cmm(m_ak[p], v_s[p], NN) for p in pairs]
    else:
        m_ab = [_cmm(a_s[p], b_s[p], NT) for p in pairs] if n_valid > 1 else None
        m_ak = [jnp.where(strict, _cmm(a_s[p], k_s[p], NT), 0.0) for p in pairs]
        w = [_cmm(a_s[p], s_bd[p], NT) + _cmm(m_ak[p], v_s[p], NN) for p in pairs]
    yield
    if n_valid > 1:
        assert wide
        n_sq = C.bit_length() - 1
        eye_p = jnp.where((_iota((C, LANES), 1) & (C - 1)) == _iota((C, LANES), 0), 1.0, 0.0)
        x0 = [jnp.where(strict, m_ab[p], 0.0) for p in pairs]
        xp = [x0[p][0:C] + x0[p][C:R2] for p in pairs]
        tp = [eye_p + xp[p] for p in pairs]
        xp = [_cmm(xp[p], stack(xp[p]), NN) for p in pairs]
        for _ in range(1, n_sq - 1):
            prod = [_cmm(jnp.concatenate([xp[p], tp[p]], axis=0), stack(xp[p]), NN) for p in pairs]
            tp = [tp[p] + prod[p][C:R2] for p in pairs]
            xp = [prod[p][0:C] for p in pairs]
            yield
        tp = [tp[p] + _cmm(tp[p], stack(xp[p]), NN) for p in pairs]
        u = [_cmm(stack(tp[p]), w[p], NN) for p in pairs]
    else:
        u = w
    yield
    ys = []
    for p in pairs:
        if wide:
            n_r = jnp.where(jnp.concatenate([incl, incl], axis=1), quad[p][R2:, :], 0.0)
            uv = jnp.concatenate([u[p], v_s[p]], axis=0)
            y2 = ars[p][R2:] + _cmm(n_r, uv, NN)
            upd = _cmm(uv, bk[p], TN)
        else:
            n_rb = jnp.where(incl, _cmm(r_s[p], b_s[p], NT), 0.0)
            n_rk = jnp.where(incl, _cmm(r_s[p], k_s[p], NT), 0.0)
            y2 = _cmm(r_s[p], s_bd[p], NT) + _cmm(n_rb, u[p], NN) + _cmm(n_rk, v_s[p], NN)
            upd = _cmm(u[p], b_s[p], TN) + _cmm(v_s[p], k_s[p], TN)
        ys.append(y2[0:C] + y2[C:R2])
        state_ref[p] = (s_bd[p] + upd) * p_end[:, sls[p]]
    yield
    yr = jnp.concatenate(ys, axis=1)
    mean = _head_sum(yr, seg) * (1.0 / HD_RWKV)
    d = yr - mean
    var = _head_sum(d * d, seg) * (1.0 / HD_RWKV)
    yn = d * lax.rsqrt(var + RWKV_LN_EPS) * lnw_ref[...] + lnb_ref[...]
    y_ref[...] = (yn + bonus) * g


def _rwkv(proj3, shift_prev8, wkv_prev, p, *, C, n_valid, nb):
    b, lp, _ = proj3.shape
    nc = lp // C
    full = lambda shape: pl.BlockSpec(shape, lambda i, c: (0,) * len(shape))
    state = pl.BlockSpec((nb, H_RWKV, HD_RWKV, HD_RWKV), lambda i, c: (i, 0, 0, 0))
    body = _per_sequence(nb, [True] * 6 + [False] * 12 + [True] * 4, _rwkv_init,
                         functools.partial(_rwkv_main, C=C, n_valid=n_valid), _rwkv_fini)
    return pl.pallas_call(
        body,
        grid=(b // nb, nc),
        in_specs=[
            pl.BlockSpec((nb, C, 1024), lambda i, c: (i, c, COL_R)),
            pl.BlockSpec((nb, C, 1024), lambda i, c: (i, c, COL_K)),
            pl.BlockSpec((nb, C, 1024), lambda i, c: (i, c, COL_V)),
            pl.BlockSpec((nb, C, 256), lambda i, c: (i, c, COL_LR)),
            pl.BlockSpec((nb, 8, RWKV_PROJ), lambda i, c: (i, 0, 0)),
            state,
            full((1, RWKV_PROJ)), full((1, 1024)), full((2, 128, 1024)), full((1, 1024)), full((2, 128, 1024)),
            full((2, 128, 1024)), full((1, 1024)), full((1, 1024)), full((1, 1024)), full((1, 1024)), full((1, 1024)),
            full((128, 128)),
        ],
        out_specs=[pl.BlockSpec((nb, C, 1024), lambda i, c: (i, c, 0)), state],
        out_shape=[jax.ShapeDtypeStruct((b, lp, 1024), F32), jax.ShapeDtypeStruct((b, H_RWKV, HD_RWKV, HD_RWKV), F32)],
        scratch_shapes=[pltpu.VMEM((nb, C + 8, RWKV_PROJ), F32), pltpu.VMEM((nb, 8, 128, 128), F32)],
        compiler_params=_cparams(("parallel", "arbitrary")),
        name="rwkv",
    )(proj3, proj3, proj3, proj3, shift_prev8, wkv_prev, p["rwkv_mu"], p["rwkv_w0"], p["rwkv_w2p"], p["rwkv_a0"],
      p["rwkv_a2p"], p["rwkv_g2"], p["rwkv_k_k"], p["rwkv_k_a"], p["rwkv_r_k"], p["rwkv_ln_w"], p["rwkv_ln_b"],
      p["seg"])


def _merge_body(x_ref, gs_ref, gr_ref, ys_ref, yr_ref, w_ref, o_ref):
    merged = _sigmoid(gs_ref[...]) * ys_ref[...] + _sigmoid(gr_ref[...]) * yr_ref[...]
    o_ref[...] = x_ref[...] + _dot(merged.astype(BF16), w_ref[...])


def _merge_out(x, proj, y_ssd, y_rwkv, w_out, *, tm=512):
    m = x.shape[0]
    tm = min(tm, m)
    row = lambda j: pl.BlockSpec((tm, 1024), lambda i: (i, j))
    return pl.pallas_call(
        _merge_body,
        grid=(m // tm,),
        in_specs=[row(0), row(COL_GS), row(COL_GR), row(0), row(0), pl.BlockSpec((1024, 1024), lambda i: (0, 0))],
        out_specs=row(0),
        out_shape=jax.ShapeDtypeStruct((m, 1024), F32),
        compiler_params=_cparams(("parallel",)),
        name="merge_out",
    )(x, proj, proj, y_ssd, y_rwkv, w_out.astype(BF16))


def _mem_attention(q, k_ref, v_ref):
    outs = []
    for h in range(MEM_HEADS):
        sl = slice(h * MEM_HD, (h + 1) * MEM_HD)
        s = _dot_nt(q[:, sl].astype(BF16), k_ref[:, sl].astype(BF16)) * (MEM_HD ** -0.5)
        s = s - jnp.max(s, axis=-1, keepdims=True)
        e = jnp.exp(s)
        prob = e / jnp.sum(e, axis=-1, keepdims=True)
        outs.append(_dot(prob.astype(BF16), v_ref[:, sl].astype(BF16)))
    return jnp.concatenate(outs, axis=1)


def _tail_body(x_ref, gs_ref, gr_ref, ys_ref, yr_ref, k_ref, v_ref, wout_ref, nmem_ref, wmq_ref, wmo_ref,
               nffn_ref, wpq_ref, x2_ref, pq_ref):
    merged = _sigmoid(gs_ref[...]) * ys_ref[...] + _sigmoid(gr_ref[...]) * yr_ref[...]
    x1 = x_ref[...] + _dot(merged.astype(BF16), wout_ref[...])
    q = _dot(_rmsnorm(x1, nmem_ref[...]).astype(BF16), wmq_ref[...])
    x2 = x1 + _dot(_mem_attention(q, k_ref, v_ref).astype(BF16), wmo_ref[...])
    x2_ref[...] = x2
    pq_ref[...] = _dot(_rmsnorm(x2, nffn_ref[...]).astype(BF16), wpq_ref[...])


def _tail(x3, proj3, y_ssd, y_rwkv, mem_k, mem_v, p, *, tm=512):
    b, L, _ = x3.shape
    once = pl.Buffered(1)
    tile = lambda j: pl.BlockSpec((None, tm, 1024), lambda i, t: (i, t, j))
    mem = pl.BlockSpec((None, N_MEM, 1024), lambda i, t: (i, 0, 0))
    full = lambda r, c: pl.BlockSpec((r, c), lambda i, t: (0, 0), pipeline_mode=once)
    return pl.pallas_call(
        _tail_body,
        grid=(b, L // tm),
        in_specs=[tile(0), tile(COL_GS), tile(COL_GR), tile(0), tile(0), mem, mem,
                  full(1024, 1024), full(1, 1024), full(1024, 1024), full(1024, 1024), full(1, 1024),
                  full(1024, 2048)],
        out_specs=[tile(0), pl.BlockSpec((None, tm, 2048), lambda i, t: (i, t, 0))],
        out_shape=[jax.ShapeDtypeStruct((b, L, 1024), F32), jax.ShapeDtypeStruct((b, L, 2048), F32)],
        compiler_params=_cparams(("parallel", "arbitrary")),
        name="token_tail",
    )(x3, proj3, proj3, y_ssd, y_rwkv, mem_k, mem_v, p["w_out"].astype(BF16), p["norm_mem_w"].reshape(1, 1024),
      p["w_mq"].astype(BF16), p["w_mo"].astype(BF16), p["norm_ffn_w"].reshape(1, 1024), p["w_pq"].astype(BF16))


def _attn_step_body(q_ref, k_ref, v_ref, o_ref):
    s = jnp.sum(k_ref[...] * q_ref[...][:, None], axis=-1, keepdims=True) * (MEM_HD ** -0.5)
    e = jnp.exp(s - jnp.max(s, axis=1, keepdims=True))
    prob = e / jnp.sum(e, axis=1, keepdims=True)
    o_ref[...] = jnp.sum(prob * v_ref[...], axis=1)


def _attn_step(q, mem_k, mem_v, *, nb=4):
    b = q.shape[0]
    qspec = pl.BlockSpec((nb, MEM_HEADS, MEM_HD), lambda i: (i, 0, 0))
    kvspec = pl.BlockSpec((nb, N_MEM, MEM_HEADS, MEM_HD), lambda i: (i, 0, 0, 0))
    return pl.pallas_call(
        _attn_step_body,
        grid=(b // nb,),
        in_specs=[qspec, kvspec, kvspec],
        out_specs=qspec,
        out_shape=jax.ShapeDtypeStruct((b, MEM_HEADS, MEM_HD), F32),
        compiler_params=_cparams(("parallel",)),
        name="mem_attn_step",
    )(q, mem_k, mem_v)


def _stage2_pairs():
    return [(r1, r2) for r1 in range(PEER_TOPK) for r2 in range(PEER_TOPK) if (r1 + 1) * (r2 + 1) <= PEER_TOPK]


N_CAND = 56


def _topk_body(q_ref, keys_ref, gate_ref, i1_ref, i2_ref, s_scr, i_scr, cand_scr, code_scr, res_scr, *, T):
    neg = -jnp.inf
    key_row = _iota((N_KEYS, T), 0).astype(F32)
    cand_row = _iota((N_CAND, T), 0).astype(F32)
    pairs = _stage2_pairs()
    q = q_ref[...]
    for h in range(PEER_HEADS):
        for c in range(2):
            hc = 2 * h + c
            s = _dot_nt(keys_ref[hc], q[:, hc * LANES:(hc + 1) * LANES], HI)
            for r in range(PEER_TOPK):
                m = jnp.max(s, axis=0, keepdims=True)
                idx = jnp.min(jnp.where(s == m, key_row, float(N_KEYS)), axis=0, keepdims=True)
                s_scr[c, r:r + 1, :] = m
                i_scr[c, r:r + 1, :] = idx
                s = jnp.where(key_row == idx, neg, s)
        cand_scr[...] = jnp.full((N_CAND, T), neg, F32)
        code_scr[...] = jnp.zeros((N_CAND, T), F32)
        for j, (r1, r2) in enumerate(pairs):
            cand_scr[j:j + 1, :] = s_scr[0, r1:r1 + 1, :] + s_scr[1, r2:r2 + 1, :]
            code_scr[j:j + 1, :] = i_scr[0, r1:r1 + 1, :] * float(N_KEYS) + i_scr[1, r2:r2 + 1, :]
        cand = cand_scr[...]
        code = code_scr[...]
        for r in range(PEER_TOPK):
            m = jnp.max(cand, axis=0, keepdims=True)
            idx = jnp.min(jnp.where(cand == m, cand_row, float(N_CAND)), axis=0, keepdims=True)
            hit = cand_row == idx
            res_scr[0, r:r + 1, :] = m
            res_scr[1, r:r + 1, :] = jnp.sum(jnp.where(hit, code, 0.0), axis=0, keepdims=True)
            cand = jnp.where(hit, neg, cand)
        top = res_scr[0]
        e = jnp.exp(top - jnp.max(top, axis=0, keepdims=True))
        rows = slice(h * PEER_TOPK, (h + 1) * PEER_TOPK)
        gate_ref[rows, :] = e / jnp.sum(e, axis=0, keepdims=True)
        first = jnp.floor(res_scr[1] * (1.0 / N_KEYS))
        i1_ref[rows, :] = first
        i2_ref[rows, :] = res_scr[1] - first * float(N_KEYS)


def _peer_topk(q, sub_keys, *, T=256):
    m = q.shape[0]
    T = min(T, m)
    body = functools.partial(_topk_body, T=T)
    out = pl.BlockSpec((128, T), lambda i: (0, i))
    return pl.pallas_call(
        body,
        grid=(m // T,),
        in_specs=[pl.BlockSpec((T, 2048), lambda i: (i, 0)), pl.BlockSpec((16, 128, 128), lambda i: (0, 0, 0))],
        out_specs=[out, out, out],
        out_shape=[jax.ShapeDtypeStruct((128, m), F32)] * 3,
        scratch_shapes=[pltpu.VMEM((2, PEER_TOPK, T), F32), pltpu.VMEM((2, PEER_TOPK, T), F32),
                        pltpu.VMEM((N_CAND, T), F32), pltpu.VMEM((N_CAND, T), F32),
                        pltpu.VMEM((2, PEER_TOPK, T), F32)],
        compiler_params=_cparams(("parallel",)),
        name="peer_topk",
    )(q, sub_keys.reshape(16, 128, 128))


I1_PER_STEP = 8
G_PITCH = 136


def _peer_body(x_ref, gate_ref, i1_ref, i2_ref, nw_ref, fw_ref, ut_ref, v_ref, o_ref, hn_scr, g_scr, coef_scr, *, T):
    j = pl.program_id(1)

    @pl.when(j == 0)
    def _():
        hn_scr[...] = _rmsnorm(x_ref[...], nw_ref[...]).astype(BF16)
        o_ref[...] = jnp.zeros_like(o_ref)
        shape = (N_KEYS, LANES)
        key_row = _iota(shape, 0).astype(F32).astype(BF16)
        one = jnp.ones(shape, BF16)
        zero = jnp.zeros(shape, BF16)

        def build(t, carry):
            row = lambda ref: jnp.broadcast_to(ref[pl.ds(t, 1), :].astype(BF16), shape)
            a_t = jnp.where(key_row == row(i1_ref), row(gate_ref), zero)
            b_t = jnp.where(key_row == row(i2_ref), one, zero)
            g_scr[pl.ds(pl.multiple_of(t * G_PITCH, 8), N_KEYS), :] = _dot_nt(a_t, b_t)
            return carry

        lax.fori_loop(0, T, build, 0, unroll=64)

    hn = hn_scr[...]
    for cidx in range(I1_PER_STEP // 2):
        act = _dot(hn, ut_ref[:, cidx * 2 * LANES:(cidx + 1) * 2 * LANES])
        ge = 0.5 * act * (1.0 + lax.erf(act * (2.0 ** -0.5)))
        for half in range(2):
            ii = 2 * cidx + half
            g_t = g_scr[pl.ds(j * I1_PER_STEP + ii, T, stride=G_PITCH), :]
            coef_scr[:, ii * LANES:(ii + 1) * LANES] = (g_t * ge[:, half * LANES:(half + 1) * LANES]).astype(BF16)
    o_ref[...] += _dot(coef_scr[...], v_ref[...])

    @pl.when(j == pl.num_programs(1) - 1)
    def _():
        o_ref[...] = _rmsnorm(x_ref[...] + o_ref[...], fw_ref[...])


def _peer_mix(x, gate, i1, i2, norm_w, final_w, ut_bf16, v_bf16, *, T=512):
    m = x.shape[0]
    T = min(T, m)
    ne = I1_PER_STEP * N_KEYS
    body = functools.partial(_peer_body, T=T)
    once = pl.Buffered(1)
    row = lambda w: pl.BlockSpec((T, w), lambda i, j: (i, 0), pipeline_mode=once)
    vec = pl.BlockSpec((1, 1024), lambda i, j: (0, 0), pipeline_mode=once)
    return pl.pallas_call(
        body,
        grid=(m // T, N_KEYS // I1_PER_STEP),
        in_specs=[row(1024), row(128), row(128), row(128), vec, vec,
                  pl.BlockSpec((1024, ne), lambda i, j: (0, j)), pl.BlockSpec((ne, 1024), lambda i, j: (j, 0))],
        out_specs=pl.BlockSpec((T, 1024), lambda i, j: (i, 0)),
        out_shape=jax.ShapeDtypeStruct((m, 1024), F32),
        scratch_shapes=[pltpu.VMEM((T, 1024), BF16), pltpu.VMEM((T * G_PITCH, LANES), F32),
                        pltpu.VMEM((T, 1024), BF16)],
        compiler_params=_cparams(("parallel", "arbitrary")),
        name="peer_mix",
    )(x, gate, i1, i2, norm_w.reshape(1, 1024), final_w.reshape(1, 1024), ut_bf16, v_bf16)


def _pad_rows(x, rows):
    return jnp.pad(x, ((0, 0), (rows - x.shape[1], 0), (0, 0)))


def _block(x, mem_k, mem_v, conv_prev, ssm_prev, shift_prev, wkv_prev, p, *, ssd_l, rwkv_c):
    b, L, _ = x.shape
    m = b * L
    x2d = x.reshape(m, D_MODEL)
    proj = _linear(x2d, p["w_in"], norm_w=p["norm_mix_w"], tm=2048, tn=1152, name="in_proj")
    proj3 = proj.reshape(b, L, D_IN_PAD)
    lp = max(L, SUBLANES)
    n_valid = L if L < SUBLANES else max(ssd_l, rwkv_c)
    proj3p = proj3 if lp == L else jnp.pad(proj3, ((0, 0), (0, lp - L), (0, 0)))

    nb = SEQS_PER_STEP if L == 1 else 1
    y_ssd, ssm_new = _ssd(proj3p, _pad_rows(conv_prev, 8), ssm_prev, p,
                          l=ssd_l, n_valid=min(n_valid, ssd_l), nb=nb)
    y_rwkv, wkv_new = _rwkv(proj3p, _pad_rows(shift_prev[:, None, :], 8), wkv_prev, p,
                            C=rwkv_c, n_valid=min(n_valid, rwkv_c), nb=nb)
    tail = proj3[:, max(L - (CONV_K - 1), 0):]
    xbc = jnp.concatenate([tail[..., 1024:2048], tail[..., 7168:7680]], axis=-1)
    rw = jnp.concatenate([tail[..., 2048:5120], tail[..., 7680:7936]], axis=-1)
    conv_new = jnp.concatenate([conv_prev, xbc], axis=1)[:, -(CONV_K - 1):]
    shift_new = rw[:, -1]

    if L == 1:
        x1 = _merge_out(x2d, proj, y_ssd[:, 0], y_rwkv[:, 0], p["w_out"])
        q = _linear(x1, p["w_mq"], norm_w=p["norm_mem_w"], name="mem_q")
        o = _attn_step(q.reshape(b, MEM_HEADS, MEM_HD), mem_k, mem_v)
        x2 = _linear(o.reshape(m, D_MODEL), p["w_mo"], res=x1, name="mem_o")
        pq = _linear(x2, p["w_pq"], norm_w=p["norm_ffn_w"], name="peer_q")
    else:
        x2, pq = _tail(x, proj3, y_ssd, y_rwkv, mem_k.reshape(b, N_MEM, D_MODEL), mem_v.reshape(b, N_MEM, D_MODEL), p)
        x2, pq = x2.reshape(m, D_MODEL), pq.reshape(m, 2 * D_MODEL)
    gate_t, i1_t, i2_t = _peer_topk(pq, p["sub_keys"])
    y = _peer_mix(x2, gate_t.T, i1_t.T, i2_t.T, p["norm_ffn_w"], p["norm_final_w"], p["expert_ut"], p["expert_v"])
    return y.reshape(b, L, D_MODEL), conv_new, ssm_new, shift_new, wkv_new


def kernel(x_prompt, x_sample, mem_prompt, state_ssm, state_conv, state_wkv, state_shift, cache_mem_k, cache_mem_v, norm_mix_w, w_in, conv_w, conv_b, dt_bias, a_log, d_skip, ssd_norm_w, rwkv_mu, rwkv_w0, rwkv_w2, rwkv_a0, rwkv_a2, rwkv_g2, rwkv_k_k, rwkv_k_a, rwkv_r_k, rwkv_ln_w, rwkv_ln_b, w_out, norm_mem_w, mem_norm_w, w_mk, w_mv, w_mq, w_mo, norm_ffn_w, w_pq, sub_keys, expert_u, expert_v, norm_final_w):
    assert w_in.shape[0] == 1, "single-layer problem"
    bp = x_prompt.shape[0]
    bs = x_sample.shape[0]
    w_in_perm = _permute_w_in(w_in[0])
    pad16 = lambda a: jnp.pad(a.reshape(1, H_SSD), ((0, 0), (0, LANES - H_SSD)))
    head_of_lane = jnp.arange(D_MODEL) // SSD_HEADDIM
    zeros64 = jnp.zeros((64, D_MODEL), F32)
    hi_lo = lambda w: jnp.stack(_split_bf16(w))
    p = {
        "norm_mix_w": norm_mix_w[0], "w_in": w_in_perm,
        "conv_w": conv_w[0], "conv_b": conv_b[0].reshape(1, CONV_DIM),
        "dt_bias": pad16(dt_bias[0]), "a_log": pad16(a_log[0]),
        "d_skip_e": jnp.repeat(d_skip[0], SSD_HEADDIM).reshape(1, D_MODEL),
        "ssd_norm_w": ssd_norm_w[0].reshape(1, D_MODEL),
        "expand": (jnp.arange(LANES)[:, None] == head_of_lane[None, :]).astype(BF16),
        "rwkv_mu": rwkv_mu[0].reshape(1, RWKV_PROJ), "rwkv_w0": rwkv_w0[0].reshape(1, D_MODEL),
        "rwkv_w2p": hi_lo(jnp.concatenate([rwkv_w2[0], zeros64], axis=0)),
        "rwkv_a0": rwkv_a0[0].reshape(1, D_MODEL),
        "rwkv_a2p": hi_lo(jnp.concatenate([zeros64, rwkv_a2[0]], axis=0)),
        "rwkv_g2": hi_lo(rwkv_g2[0]), "rwkv_k_k": rwkv_k_k[0].reshape(1, D_MODEL),
        "rwkv_k_a": rwkv_k_a[0].reshape(1, D_MODEL), "rwkv_r_k": rwkv_r_k[0].reshape(1, D_MODEL),
        "rwkv_ln_w": rwkv_ln_w[0].reshape(1, D_MODEL), "rwkv_ln_b": rwkv_ln_b[0].reshape(1, D_MODEL),
        "seg": (head_of_lane[:LANES, None] == head_of_lane[None, :LANES]).astype(BF16),
        "w_out": w_out[0], "norm_mem_w": norm_mem_w[0], "w_mq": w_mq[0], "w_mo": w_mo[0],
        "norm_ffn_w": norm_ffn_w[0], "w_pq": w_pq[0], "sub_keys": sub_keys[0],
        "expert_ut": expert_u[0].astype(BF16).T, "expert_v": expert_v[0].astype(BF16),
        "norm_final_w": norm_final_w,
    }

    mem2d = mem_prompt.reshape(bp * N_MEM, D_MODEL)
    mem_k = _linear(mem2d, w_mk[0], norm_w=mem_norm_w[0], name="mem_k").reshape(bp, N_MEM, MEM_HEADS, MEM_HD)
    mem_v = _linear(mem2d, w_mv[0], norm_w=mem_norm_w[0], name="mem_v").reshape(bp, N_MEM, MEM_HEADS, MEM_HD)
    y_p, conv_p, ssm_p, shift_p, wkv_p = _block(
        x_prompt, mem_k, mem_v,
        jnp.zeros((bp, CONV_K - 1, CONV_DIM), F32), jnp.zeros((bp, H_SSD, SSD_HEADDIM, SSD_STATE), F32),
        jnp.zeros((bp, RWKV_PROJ), F32), jnp.zeros((bp, H_RWKV, HD_RWKV, HD_RWKV), F32), p,
        ssd_l=SSD_CHUNK, rwkv_c=RWKV_CHUNK)
    y_s, conv_s, ssm_s, shift_s, wkv_s = _block(
        x_sample, cache_mem_k[0], cache_mem_v[0], state_conv[0], state_ssm[0], state_shift[0], state_wkv[0], p,
        ssd_l=SUBLANES, rwkv_c=SUBLANES)
    return (y_p, y_s, ssm_p[None], conv_p[None], wkv_p[None], shift_p[None], mem_k[None], mem_v[None],
            ssm_s[None], conv_s[None], wkv_s[None], shift_s[None])
```

```python
import functools

import jax
import jax.numpy as jnp
from jax import lax
from jax.experimental import pallas as pl
from jax.experimental.pallas import tpu as pltpu

F32 = jnp.float32
BF16 = jnp.bfloat16
HI = lax.Precision.HIGHEST

D_MODEL = 1024
H_SSD = 16
SSD_HEADDIM = 64
SSD_STATE = 128
CONV_K = 4
CONV_DIM = 1536
H_RWKV = 16
HD_RWKV = 64
RWKV_PROJ = 3328
RWKV_LN_EPS = 64e-5
N_MEM = 256
MEM_HEADS = 4
MEM_HD = 256
N_KEYS = 128
PEER_HEADS = 8
PEER_TOPK = 16
NORM_EPS = 1e-6

LANES = 128
SUBLANES = 8
VMEM_LIMIT = 56 * 1024 * 1024

D_IN_PAD = 8064
COL_Z, COL_XS, COL_R, COL_K, COL_V, COL_GS, COL_GR = 0, 1, 2, 3, 4, 5, 6
COL_BC = 14
COL_LR = 30
COL_DT = 62

SSD_CHUNK = 128
RWKV_CHUNK = 64
SEQS_PER_STEP = 8


def _cparams(sem):
    return pltpu.CompilerParams(dimension_semantics=sem, vmem_limit_bytes=VMEM_LIMIT)


def _dot(a, b, prec=None):
    return jnp.dot(a, b, preferred_element_type=F32, precision=prec)


def _dot_nt(a, b, prec=None):
    return lax.dot_general(a, b, (((1,), (1,)), ((), ())), preferred_element_type=F32, precision=prec)


def _dot_tn(a, b, prec=None):
    return lax.dot_general(a, b, (((0,), (0,)), ((), ())), preferred_element_type=F32, precision=prec)


def _split_bf16(x):
    hi = x.astype(BF16)
    lo = (x - hi.astype(F32)).astype(BF16)
    return hi, lo


def _dot_onehot(x, sel_bf16):
    hi, lo = _split_bf16(x)
    return _dot(hi, sel_bf16) + _dot(lo, sel_bf16)


NN = ((1,), (0,))
NT = ((1,), (1,))
TN = ((0,), (0,))


def _cmm(a, b, dims):
    return lax.dot_general(a.astype(BF16), b.astype(BF16), (dims, ((), ())), preferred_element_type=F32)


def _softplus(x):
    return jnp.maximum(x, 0.0) + jnp.log1p(jnp.exp(-jnp.abs(x)))


def _sigmoid(x):
    return 1.0 / (1.0 + jnp.exp(-x))


def _rmsnorm(x, w):
    return x * lax.rsqrt(jnp.mean(x * x, axis=-1, keepdims=True) + NORM_EPS) * w


def _iota(shape, dim):
    return lax.broadcasted_iota(jnp.int32, shape, dim)


def _per_sequence(nb, per_seq, init, main, fini=None):
    def wrapped(*refs):
        views = [[r.at[s] if flag else r for r, flag in zip(refs, per_seq)] for s in range(nb)]
        c = pl.program_id(1)

        @pl.when(c == 0)
        def _():
            for v in views:
                init(*v)

        live = [main(*v) for v in views]
        while live:
            live = [g for g in live if next(g, True) is None]
        if fini is not None:
            @pl.when(c == pl.num_programs(1) - 1)
            def _():
                for v in views:
                    fini(*v)
    return wrapped


def _linear_body(*refs, has_norm, has_res):
    it = iter(refs)
    x_ref = next(it)
    nw_ref = next(it) if has_norm else None
    w_ref = next(it)
    res_ref = next(it) if has_res else None
    o_ref = next(it)
    x = x_ref[...]
    if has_norm:
        x = _rmsnorm(x, nw_ref[...])
    acc = _dot(x.astype(BF16), w_ref[...])
    if has_res:
        acc = acc + res_ref[...]
    o_ref[...] = acc


def _linear(x, w, *, norm_w=None, res=None, tm=512, tn=1024, name="linear"):
    m, k = x.shape
    n = w.shape[1]
    tm = min(tm, m)
    tn = min(tn, n)
    assert m % tm == 0 and n % tn == 0
    args = [x]
    specs = [pl.BlockSpec((tm, k), lambda i, j: (i, 0))]
    if norm_w is not None:
        args.append(norm_w.reshape(1, k))
        specs.append(pl.BlockSpec((1, k), lambda i, j: (0, 0)))
    args.append(w.astype(BF16))
    specs.append(pl.BlockSpec((k, tn), lambda i, j: (0, j)))
    if res is not None:
        args.append(res)
        specs.append(pl.BlockSpec((tm, tn), lambda i, j: (i, j)))
    body = functools.partial(_linear_body, has_norm=norm_w is not None, has_res=res is not None)
    return pl.pallas_call(
        body,
        grid=(m // tm, n // tn),
        in_specs=specs,
        out_specs=pl.BlockSpec((tm, tn), lambda i, j: (i, j)),
        out_shape=jax.ShapeDtypeStruct((m, n), F32),
        compiler_params=_cparams(("parallel", "arbitrary")),
        name=name,
    )(*args)


W_IN_SEGMENTS = ((0, 2048), (2576, 5648), (5904, 7952), (2048, 2560), (5648, 5904), (2560, 2576))


def _permute_body(w_ref, o_ref):
    w = w_ref[...]
    parts = [w[:, a:b] for a, b in W_IN_SEGMENTS]
    parts.append(jnp.zeros((w.shape[0], LANES - H_SSD), F32))
    o_ref[...] = jnp.concatenate(parts, axis=1).astype(BF16)


def _permute_w_in(w):
    k, n = w.shape
    tr = 128
    return pl.pallas_call(
        _permute_body,
        grid=(k // tr,),
        in_specs=[pl.BlockSpec((tr, n), lambda i: (i, 0))],
        out_specs=pl.BlockSpec((tr, D_IN_PAD), lambda i: (i, 0)),
        out_shape=jax.ShapeDtypeStruct((k, D_IN_PAD), BF16),
        compiler_params=_cparams(("parallel",)),
        name="permute_w_in",
    )(w)


def _ssd_init(z_ref, xs_ref, bc_ref, dt_ref, convprev_ref, ssmprev_ref, convw_ref, convb_ref, dtb_ref,
              alog_ref, dskip_ref, normw_ref, expand_ref, y_ref, state_ref, ext_ref):
    ext_ref[0:8, :] = convprev_ref[...]
    state_ref[...] = ssmprev_ref[...]


def _ssd_main(z_ref, xs_ref, bc_ref, dt_ref, convprev_ref, ssmprev_ref, convw_ref, convb_ref, dtb_ref,
              alog_ref, dskip_ref, normw_ref, expand_ref, y_ref, state_ref, ext_ref, *, l, n_valid):
    ext_ref[8:8 + l, 0:D_MODEL] = xs_ref[...]
    ext_ref[8:8 + l, D_MODEL:CONV_DIM] = bc_ref[...]
    conv = convb_ref[...]
    for i in range(CONV_K):
        conv = conv + ext_ref[pl.ds(5 + i, l), :] * convw_ref[i:i + 1, :]
    ext_ref[0:8, :] = ext_ref[l:l + 8, :]

    yield
    act = conv * _sigmoid(conv)
    xs = act[:, 0:D_MODEL]
    bm = act[:, D_MODEL:D_MODEL + 256]
    cm = act[:, D_MODEL + 256:CONV_DIM]

    dt = _softplus(dt_ref[...] + dtb_ref[...])
    if n_valid < l:
        dt = jnp.where(_iota((l, LANES), 0) < n_valid, dt, 0.0)
    ad = dt * (-jnp.exp(alog_ref[...]))
    row = _iota((l, l), 0)
    col = _iota((l, l), 1)
    causal = row >= col
    cs = _dot_sel3(jnp.where(causal, 1.0, 0.0).astype(BF16), ad)
    eye = jnp.where(_iota((LANES, LANES), 0) == _iota((LANES, LANES), 1), 1.0, 0.0).astype(BF16)
    cs_parts = _split3(cs)
    cs_t = sum(_dot_nt(eye, part) for part in cs_parts)
    expand = expand_ref[...]
    dt_e = sum(_dot(part, expand) for part in _split3(dt))
    cs_e = sum(_dot(part, expand) for part in cs_parts)
    yield
    xdt = xs * dt_e
    decay_in = jnp.exp(cs_e)
    decay_out = jnp.exp(cs_e[l - 1:l, :] - cs_e)
    xdt_dec = (xdt * decay_out).astype(BF16)
    lane_lo = _iota((l, LANES), 1) < SSD_HEADDIM
    row_lo = _iota((LANES, 1), 0) < SSD_HEADDIM

    ys = []
    for g in range(2):
        bg = bm[:, g * LANES:(g + 1) * LANES].astype(BF16)
        cg = cm[:, g * LANES:(g + 1) * LANES].astype(BF16)
        cb = _dot_nt(cg, bg)
        for pr in range(4 * g, 4 * g + 4):
            sl = slice(pr * LANES, (pr + 1) * LANES)
            h0, h1 = 2 * pr, 2 * pr + 1
            m0 = jnp.where(causal, jnp.exp(cs[:, h0:h0 + 1] - cs_t[h0:h0 + 1, :]), 0.0) * cb
            m1 = jnp.where(causal, jnp.exp(cs[:, h1:h1 + 1] - cs_t[h1:h1 + 1, :]), 0.0) * cb
            xp = xdt[:, sl]
            x_lo = jnp.where(lane_lo, xp, 0.0).astype(BF16)
            x_hi = jnp.where(lane_lo, 0.0, xp).astype(BF16)
            y_diag = _dot(m0.astype(BF16), x_lo) + _dot(m1.astype(BF16), x_hi)
            s_pair = state_ref[2 * pr:2 * pr + 2].reshape(LANES, SSD_STATE)
            y_off = _dot_nt(cg, s_pair.astype(BF16)) * decay_in[:, sl]
            ys.append(y_diag + y_off + dskip_ref[:, sl] * xs[:, sl])
            upd = _dot_tn(xdt_dec[:, sl], bg)
            last = cs[l - 1:l, :]
            row_decay = jnp.exp(jnp.where(row_lo, last[:, h0:h0 + 1], last[:, h1:h1 + 1]))
            state_ref[2 * pr:2 * pr + 2] = (row_decay * s_pair + upd).reshape(2, SSD_HEADDIM, SSD_STATE)
            yield
    y = jnp.concatenate(ys, axis=1)
    z = z_ref[...]
    y_ref[...] = _rmsnorm(y * (z * _sigmoid(z)), normw_ref[...])


def _ssd(proj3, conv_prev8, ssm_prev, p, *, l, n_valid, nb):
    b, lp, _ = proj3.shape
    nc = lp // l
    full = lambda shape: pl.BlockSpec(shape, lambda i, c: (0,) * len(shape))
    state = pl.BlockSpec((nb, H_SSD, SSD_HEADDIM, SSD_STATE), lambda i, c: (i, 0, 0, 0))
    body = _per_sequence(nb, [True] * 6 + [False] * 7 + [True] * 3, _ssd_init,
                         functools.partial(_ssd_main, l=l, n_valid=n_valid))
    return pl.pallas_call(
        body,
        grid=(b // nb, nc),
        in_specs=[
            pl.BlockSpec((nb, l, 1024), lambda i, c: (i, c, COL_Z)),
            pl.BlockSpec((nb, l, 1024), lambda i, c: (i, c, COL_XS)),
            pl.BlockSpec((nb, l, 512), lambda i, c: (i, c, COL_BC)),
            pl.BlockSpec((nb, l, 128), lambda i, c: (i, c, COL_DT)),
            pl.BlockSpec((nb, 8, CONV_DIM), lambda i, c: (i, 0, 0)),
            state,
            full((CONV_K, CONV_DIM)), full((1, CONV_DIM)), full((1, 128)), full((1, 128)),
            full((1, 1024)), full((1, 1024)), full((128, 1024)),
        ],
        out_specs=[pl.BlockSpec((nb, l, 1024), lambda i, c: (i, c, 0)), state],
        out_shape=[jax.ShapeDtypeStruct((b, lp, 1024), F32),
                   jax.ShapeDtypeStruct((b, H_SSD, SSD_HEADDIM, SSD_STATE), F32)],
        scratch_shapes=[pltpu.VMEM((nb, l + 8, CONV_DIM), F32)],
        compiler_params=_cparams(("parallel", "arbitrary")),
        name="ssd",
    )(proj3, proj3, proj3, proj3, conv_prev8, ssm_prev, p["conv_w"], p["conv_b"], p["dt_bias"], p["a_log"],
      p["d_skip_e"], p["ssd_norm_w"], p["expand"])


def _dot_w(x, w_ref):
    xh, xl = _split_bf16(x)
    return _dot(xh, w_ref[0]) + _dot(xl, w_ref[0]) + _dot(xh, w_ref[1])


def _split3(x):
    x1 = x.astype(BF16)
    r1 = x - x1.astype(F32)
    x2 = r1.astype(BF16)
    x3 = (r1 - x2.astype(F32)).astype(BF16)
    return x1, x2, x3


def _dot_sel3(sel_bf16, x):
    return sum(_dot(sel_bf16, part) for part in _split3(x))


def _head_sum(x, seg_pair):
    rows = x.shape[0]
    n = x.shape[1] // LANES
    stacked = jnp.concatenate([x[:, i * LANES:(i + 1) * LANES] for i in range(n)], axis=0)
    s = _dot_onehot(stacked, seg_pair)
    return jnp.concatenate([s[i * rows:(i + 1) * rows] for i in range(n)], axis=1)


def _rwkv_init(r_ref, k_ref, v_ref, lr_ref, shprev_ref, wkvprev_ref, mu_ref, w0_ref, w2_ref, a0_ref, a2_ref,
               g2_ref, kk_ref, ka_ref, rk_ref, lnw_ref, lnb_ref, seg_ref, y_ref, wkv_ref, ext_ref, state_ref):
    same_head = (_iota((LANES, LANES), 0) < HD_RWKV) == (_iota((LANES, LANES), 1) < HD_RWKV)
    ext_ref[0:8, :] = shprev_ref[...]
    for p in range(8):
        s_c = wkvprev_ref[2 * p:2 * p + 2].reshape(LANES, HD_RWKV)
        state_ref[p] = jnp.where(same_head, jnp.concatenate([s_c, s_c], axis=1), 0.0)


def _rwkv_fini(r_ref, k_ref, v_ref, lr_ref, shprev_ref, wkvprev_ref, mu_ref, w0_ref, w2_ref, a0_ref, a2_ref,
               g2_ref, kk_ref, ka_ref, rk_ref, lnw_ref, lnb_ref, seg_ref, y_ref, wkv_ref, ext_ref, state_ref):
    for p in range(8):
        s_p = state_ref[p]
        wkv_ref[2 * p] = s_p[0:HD_RWKV, 0:HD_RWKV]
        wkv_ref[2 * p + 1] = s_p[HD_RWKV:LANES, HD_RWKV:LANES]


def _rwkv_main(r_ref, k_ref, v_ref, lr_ref, shprev_ref, wkvprev_ref, mu_ref, w0_ref, w2_ref, a0_ref, a2_ref,
               g2_ref, kk_ref, ka_ref, rk_ref, lnw_ref, lnb_ref, seg_ref, y_ref, wkv_ref, ext_ref, state_ref,
               *, C, n_valid):
    ext_ref[8:8 + C, 0:1024] = r_ref[...]
    ext_ref[8:8 + C, 1024:2048] = k_ref[...]
    ext_ref[8:8 + C, 2048:3072] = v_ref[...]
    ext_ref[8:8 + C, 3072:3328] = lr_ref[...]
    cur = ext_ref[8:8 + C, :]
    prev = ext_ref[pl.ds(7, C), :]
    mixed = cur + (prev - cur) * mu_ref[...]
    ext_ref[0:8, :] = ext_ref[C:C + 8, :]

    yield
    r = mixed[:, 0:1024]
    k = mixed[:, 1024:2048]
    v = mixed[:, 2048:3072]
    wa = mixed[:, 3072:3200]
    gl = mixed[:, 3200:3328]
    seg = seg_ref[...]

    wlog = -_softplus(-(w0_ref[...] + _dot_w(jnp.tanh(wa), w2_ref))) - 0.5
    lw = -jnp.exp(wlog)
    a_sig = _sigmoid(a0_ref[...] + _dot_w(wa, a2_ref))
    g = _dot_w(_sigmoid(gl), g2_ref)
    yield
    kk = k * kk_ref[...]
    kk = kk * lax.rsqrt(jnp.maximum(_head_sum(kk * kk, seg), 1e-24))
    kh = k * (1.0 + (a_sig - 1.0) * ka_ref[...])
    av = -kk
    bv = kk * a_sig
    bonus = _head_sum(r * kh * rk_ref[...], seg) * v
    if n_valid < C:
        valid = _iota((C, 1024), 0) < n_valid
        lw = jnp.where(valid, lw, 0.0)
        av = jnp.where(valid, av, 0.0)
        bv = jnp.where(valid, bv, 0.0)
        kh = jnp.where(valid, kh, 0.0)
        v = jnp.where(valid, v, 0.0)

    yield
    causal_c = jnp.where(_iota((C, C), 0) >= _iota((C, C), 1), 1.0, 0.0).astype(BF16)
    cl = _dot_sel3(causal_c, lw)
    e_pos = jnp.exp(cl)
    e_neg = jnp.exp(-cl)
    rt = r * e_pos
    at = av * jnp.exp(cl - lw)
    bt = bv * e_neg
    kt = kh * e_neg
    p_end = e_pos[C - 1:C, :]

    R2 = 2 * C
    lane_lo = _iota((C, LANES), 1) < HD_RWKV

    def stack(x):
        return jnp.concatenate([jnp.where(lane_lo, x, 0.0), jnp.where(lane_lo, 0.0, x)], axis=0)

    pairs = range(8)
    sls = [slice(p * LANES, (p + 1) * LANES) for p in pairs]
    b_s = [stack(bt[:, sl]) for sl in sls]
    k_s = [stack(kt[:, sl]) for sl in sls]
    v_s = [stack(v[:, sl]) for sl in sls]
    yield
    s_bd = [state_ref[p] for p in pairs]
    wide = R2 == LANES
    if wide:
        tok = _iota((C, LANES), 1) & (C - 1)
        row = _iota((C, LANES), 0)
        strict = tok < row
        incl = tok <= row
        ar = [jnp.concatenate([at[:, sl], rt[:, sl]], axis=0) for sl in sls]
        bk = [jnp.concatenate([b_s[p], k_s[p]], axis=0) for p in pairs]
        quad = [_cmm(ar[p], bk[p], NT) for p in pairs]
        ars = [_cmm(ar[p], s_bd[p], NT) for p in pairs]
        m_ak = [jnp.where(strict, quad[p][0:C, R2:], 0.0) for p in pairs]
        w = [ars[p][0:C] + _cmm(m_ak[p], v_s[p], NN) for p in pairs]
    else:
        rr = _iota((R2, R2), 0)
        cc = _iota((R2, R2), 1)
        strict = rr > cc
        incl = rr >= cc
        a_s = [stack(at[:, sl]) for sl in sls]
        r_s = [stack(rt[:, sl]) for sl in sls]
        m_ak = [jnp.where(strict, _cmm(a_s[p], k_s[p], NT), 0.0) for p in pairs]
        w = [_cmm(a_s[p], s_bd[p], NT) + _cmm(m_ak[p], v_s[p], NN) for p in pairs]
    yield
    if n_valid > 1:
        assert wide
        n_sq = C.bit_length() - 1
        xp = [jnp.where(strict, quad[p][0:C, 0:R2], 0.0) for p in pairs]
        tp = [jnp.where(tok == row, 1.0, 0.0) + xp[p] for p in pairs]
        xp = [_cmm(xp[p], stack(xp[p]), NN) for p in pairs]
        for _ in range(1, n_sq - 1):
            prod = [_cmm(jnp.concatenate([xp[p], tp[p]], axis=0), stack(xp[p]), NN) for p in pairs]
            tp = [tp[p] + prod[p][C:R2] for p in pairs]
            xp = [prod[p][0:C] for p in pairs]
            yield
        tp = [tp[p] + _cmm(tp[p], stack(xp[p]), NN) for p in pairs]
        u = [_cmm(tp[p], stack(w[p]), NN) for p in pairs]
    else:
        u = w
    yield
    ys = []
    for p in pairs:
        if wide:
            n_r = jnp.where(jnp.concatenate([incl, incl], axis=1), quad[p][C:R2, :], 0.0)
            uv = jnp.concatenate([stack(u[p]), v_s[p]], axis=0)
            ys.append(ars[p][C:R2] + _cmm(n_r, uv, NN))
            upd = _cmm(uv, bk[p], TN)
        else:
            n_rb = jnp.where(incl, _cmm(r_s[p], b_s[p], NT), 0.0)
            n_rk = jnp.where(incl, _cmm(r_s[p], k_s[p], NT), 0.0)
            y2 = _cmm(r_s[p], s_bd[p], NT) + _cmm(n_rb, u[p], NN) + _cmm(n_rk, v_s[p], NN)
            ys.append(y2[0:C] + y2[C:R2])
            upd = _cmm(u[p], b_s[p], TN) + _cmm(v_s[p], k_s[p], TN)
        state_ref[p] = (s_bd[p] + upd) * p_end[:, sls[p]]
    yield
    yr = jnp.concatenate(ys, axis=1)
    mean = _head_sum(yr, seg) * (1.0 / HD_RWKV)
    d = yr - mean
    var = _head_sum(d * d, seg) * (1.0 / HD_RWKV)
    yn = d * lax.rsqrt(var + RWKV_LN_EPS) * lnw_ref[...] + lnb_ref[...]
    y_ref[...] = (yn + bonus) * g


def _rwkv(proj3, shift_prev8, wkv_prev, p, *, C, n_valid, nb):
    b, lp, _ = proj3.shape
    nc = lp // C
    full = lambda shape: pl.BlockSpec(shape, lambda i, c: (0,) * len(shape))
    state = pl.BlockSpec((nb, H_RWKV, HD_RWKV, HD_RWKV), lambda i, c: (i, 0, 0, 0))
    body = _per_sequence(nb, [True] * 6 + [False] * 12 + [True] * 4, _rwkv_init,
                         functools.partial(_rwkv_main, C=C, n_valid=n_valid), _rwkv_fini)
    return pl.pallas_call(
        body,
        grid=(b // nb, nc),
        in_specs=[
            pl.BlockSpec((nb, C, 1024), lambda i, c: (i, c, COL_R)),
            pl.BlockSpec((nb, C, 1024), lambda i, c: (i, c, COL_K)),
            pl.BlockSpec((nb, C, 1024), lambda i, c: (i, c, COL_V)),
            pl.BlockSpec((nb, C, 256), lambda i, c: (i, c, COL_LR)),
            pl.BlockSpec((nb, 8, RWKV_PROJ), lambda i, c: (i, 0, 0)),
            state,
            full((1, RWKV_PROJ)), full((1, 1024)), full((2, 128, 1024)), full((1, 1024)), full((2, 128, 1024)),
            full((2, 128, 1024)), full((1, 1024)), full((1, 1024)), full((1, 1024)), full((1, 1024)), full((1, 1024)),
            full((128, 128)),
        ],
        out_specs=[pl.BlockSpec((nb, C, 1024), lambda i, c: (i, c, 0)), state],
        out_shape=[jax.ShapeDtypeStruct((b, lp, 1024), F32), jax.ShapeDtypeStruct((b, H_RWKV, HD_RWKV, HD_RWKV), F32)],
        scratch_shapes=[pltpu.VMEM((nb, C + 8, RWKV_PROJ), F32), pltpu.VMEM((nb, 8, 128, 128), F32)],
        compiler_params=_cparams(("parallel", "arbitrary")),
        name="rwkv",
    )(proj3, proj3, proj3, proj3, shift_prev8, wkv_prev, p["rwkv_mu"], p["rwkv_w0"], p["rwkv_w2p"], p["rwkv_a0"],
      p["rwkv_a2p"], p["rwkv_g2"], p["rwkv_k_k"], p["rwkv_k_a"], p["rwkv_r_k"], p["rwkv_ln_w"], p["rwkv_ln_b"],
      p["seg"])


def _merge_body(x_ref, gs_ref, gr_ref, ys_ref, yr_ref, w_ref, o_ref):
    merged = _sigmoid(gs_ref[...]) * ys_ref[...] + _sigmoid(gr_ref[...]) * yr_ref[...]
    o_ref[...] = x_ref[...] + _dot(merged.astype(BF16), w_ref[...])


def _merge_out(x, proj, y_ssd, y_rwkv, w_out, *, tm=512):
    m = x.shape[0]
    tm = min(tm, m)
    row = lambda j: pl.BlockSpec((tm, 1024), lambda i: (i, j))
    return pl.pallas_call(
        _merge_body,
        grid=(m // tm,),
        in_specs=[row(0), row(COL_GS), row(COL_GR), row(0), row(0), pl.BlockSpec((1024, 1024), lambda i: (0, 0))],
        out_specs=row(0),
        out_shape=jax.ShapeDtypeStruct((m, 1024), F32),
        compiler_params=_cparams(("parallel",)),
        name="merge_out",
    )(x, proj, proj, y_ssd, y_rwkv, w_out.astype(BF16))


def _mem_attention(q, k_ref, v_ref):
    outs = []
    for h in range(MEM_HEADS):
        sl = slice(h * MEM_HD, (h + 1) * MEM_HD)
        s = _dot_nt(q[:, sl].astype(BF16), k_ref[:, sl].astype(BF16)) * (MEM_HD ** -0.5)
        s = s - jnp.max(s, axis=-1, keepdims=True)
        e = jnp.exp(s)
        prob = e / jnp.sum(e, axis=-1, keepdims=True)
        outs.append(_dot(prob.astype(BF16), v_ref[:, sl].astype(BF16)))
    return jnp.concatenate(outs, axis=1)


def _tail_body(x_ref, gs_ref, gr_ref, ys_ref, yr_ref, k_ref, v_ref, wout_ref, nmem_ref, wmq_ref, wmo_ref,
               nffn_ref, wpq_ref, x2_ref, pq_ref):
    merged = _sigmoid(gs_ref[...]) * ys_ref[...] + _sigmoid(gr_ref[...]) * yr_ref[...]
    x1 = x_ref[...] + _dot(merged.astype(BF16), wout_ref[...])
    q = _dot(_rmsnorm(x1, nmem_ref[...]).astype(BF16), wmq_ref[...])
    x2 = x1 + _dot(_mem_attention(q, k_ref, v_ref).astype(BF16), wmo_ref[...])
    x2_ref[...] = x2
    pq_ref[...] = _dot(_rmsnorm(x2, nffn_ref[...]).astype(BF16), wpq_ref[...])


def _tail(x3, proj3, y_ssd, y_rwkv, mem_k, mem_v, p, *, tm=512):
    b, L, _ = x3.shape
    once = pl.Buffered(1)
    tile = lambda j: pl.BlockSpec((None, tm, 1024), lambda i, t: (i, t, j))
    mem = pl.BlockSpec((None, N_MEM, 1024), lambda i, t: (i, 0, 0))
    full = lambda r, c: pl.BlockSpec((r, c), lambda i, t: (0, 0), pipeline_mode=once)
    return pl.pallas_call(
        _tail_body,
        grid=(b, L // tm),
        in_specs=[tile(0), tile(COL_GS), tile(COL_GR), tile(0), tile(0), mem, mem,
                  full(1024, 1024), full(1, 1024), full(1024, 1024), full(1024, 1024), full(1, 1024),
                  full(1024, 2048)],
        out_specs=[tile(0), pl.BlockSpec((None, tm, 2048), lambda i, t: (i, t, 0))],
        out_shape=[jax.ShapeDtypeStruct((b, L, 1024), F32), jax.ShapeDtypeStruct((b, L, 2048), F32)],
        compiler_params=_cparams(("parallel", "arbitrary")),
        name="token_tail",
    )(x3, proj3, proj3, y_ssd, y_rwkv, mem_k, mem_v, p["w_out"].astype(BF16), p["norm_mem_w"].reshape(1, 1024),
      p["w_mq"].astype(BF16), p["w_mo"].astype(BF16), p["norm_ffn_w"].reshape(1, 1024), p["w_pq"].astype(BF16))


def _attn_step_body(q_ref, k_ref, v_ref, o_ref):
    s = jnp.sum(k_ref[...] * q_ref[...][:, None], axis=-1, keepdims=True) * (MEM_HD ** -0.5)
    e = jnp.exp(s - jnp.max(s, axis=1, keepdims=True))
    prob = e / jnp.sum(e, axis=1, keepdims=True)
    o_ref[...] = jnp.sum(prob * v_ref[...], axis=1)


def _attn_step(q, mem_k, mem_v, *, nb=4):
    b = q.shape[0]
    qspec = pl.BlockSpec((nb, MEM_HEADS, MEM_HD), lambda i: (i, 0, 0))
    kvspec = pl.BlockSpec((nb, N_MEM, MEM_HEADS, MEM_HD), lambda i: (i, 0, 0, 0))
    return pl.pallas_call(
        _attn_step_body,
        grid=(b // nb,),
        in_specs=[qspec, kvspec, kvspec],
        out_specs=qspec,
        out_shape=jax.ShapeDtypeStruct((b, MEM_HEADS, MEM_HD), F32),
        compiler_params=_cparams(("parallel",)),
        name="mem_attn_step",
    )(q, mem_k, mem_v)


def _stage2_pairs():
    return [(r1, r2) for r1 in range(PEER_TOPK) for r2 in range(PEER_TOPK) if (r1 + 1) * (r2 + 1) <= PEER_TOPK]


N_CAND = 56


def _topk_body(q_ref, keys_ref, gate_ref, i1_ref, i2_ref, s_scr, i_scr, cand_scr, code_scr, res_scr, *, T):
    neg = -jnp.inf
    key_row = _iota((N_KEYS, T), 0).astype(F32)
    cand_row = _iota((N_CAND, T), 0).astype(F32)
    pairs = _stage2_pairs()
    q = q_ref[...]
    for h in range(PEER_HEADS):
        for c in range(2):
            hc = 2 * h + c
            s = _dot_nt(keys_ref[hc], q[:, hc * LANES:(hc + 1) * LANES], HI)
            for r in range(PEER_TOPK):
                m = jnp.max(s, axis=0, keepdims=True)
                idx = jnp.min(jnp.where(s == m, key_row, float(N_KEYS)), axis=0, keepdims=True)
                s_scr[c, r:r + 1, :] = m
                i_scr[c, r:r + 1, :] = idx
                s = jnp.where(key_row == idx, neg, s)
        cand_scr[...] = jnp.full((N_CAND, T), neg, F32)
        code_scr[...] = jnp.zeros((N_CAND, T), F32)
        for j, (r1, r2) in enumerate(pairs):
            cand_scr[j:j + 1, :] = s_scr[0, r1:r1 + 1, :] + s_scr[1, r2:r2 + 1, :]
            code_scr[j:j + 1, :] = i_scr[0, r1:r1 + 1, :] * float(N_KEYS) + i_scr[1, r2:r2 + 1, :]
        cand = cand_scr[...]
        code = code_scr[...]
        for r in range(PEER_TOPK):
            m = jnp.max(cand, axis=0, keepdims=True)
            idx = jnp.min(jnp.where(cand == m, cand_row, float(N_CAND)), axis=0, keepdims=True)
            hit = cand_row == idx
            res_scr[0, r:r + 1, :] = m
            res_scr[1, r:r + 1, :] = jnp.sum(jnp.where(hit, code, 0.0), axis=0, keepdims=True)
            cand = jnp.where(hit, neg, cand)
        top = res_scr[0]
        e = jnp.exp(top - jnp.max(top, axis=0, keepdims=True))
        rows = slice(h * PEER_TOPK, (h + 1) * PEER_TOPK)
        gate_ref[rows, :] = e / jnp.sum(e, axis=0, keepdims=True)
        first = jnp.floor(res_scr[1] * (1.0 / N_KEYS))
        i1_ref[rows, :] = first
        i2_ref[rows, :] = res_scr[1] - first * float(N_KEYS)


def _peer_topk(q, sub_keys, *, T=256):
    m = q.shape[0]
    T = min(T, m)
    body = functools.partial(_topk_body, T=T)
    out = pl.BlockSpec((128, T), lambda i: (0, i))
    return pl.pallas_call(
        body,
        grid=(m // T,),
        in_specs=[pl.BlockSpec((T, 2048), lambda i: (i, 0)), pl.BlockSpec((16, 128, 128), lambda i: (0, 0, 0))],
        out_specs=[out, out, out],
        out_shape=[jax.ShapeDtypeStruct((128, m), F32)] * 3,
        scratch_shapes=[pltpu.VMEM((2, PEER_TOPK, T), F32), pltpu.VMEM((2, PEER_TOPK, T), F32),
                        pltpu.VMEM((N_CAND, T), F32), pltpu.VMEM((N_CAND, T), F32),
                        pltpu.VMEM((2, PEER_TOPK, T), F32)],
        compiler_params=_cparams(("parallel",)),
        name="peer_topk",
    )(q, sub_keys.reshape(16, 128, 128))


I1_PER_STEP = 8
G_PITCH = 136


def _peer_body(x_ref, gate_ref, i1_ref, i2_ref, nw_ref, fw_ref, ut_ref, v_ref, o_ref, hn_scr, g_scr, coef_scr, *, T):
    j = pl.program_id(1)

    @pl.when(j == 0)
    def _():
        hn_scr[...] = _rmsnorm(x_ref[...], nw_ref[...]).astype(BF16)
        o_ref[...] = jnp.zeros_like(o_ref)
        shape = (N_KEYS, LANES)
        key_row = _iota(shape, 0).astype(F32).astype(BF16)
        one = jnp.ones(shape, BF16)
        zero = jnp.zeros(shape, BF16)

        def build(t, carry):
            row = lambda ref: jnp.broadcast_to(ref[pl.ds(t, 1), :].astype(BF16), shape)
            a_t = jnp.where(key_row == row(i1_ref), row(gate_ref), zero)
            b_t = jnp.where(key_row == row(i2_ref), one, zero)
            g_scr[pl.ds(pl.multiple_of(t * G_PITCH, 8), N_KEYS), :] = _dot_nt(a_t, b_t)
            return carry

        lax.fori_loop(0, T, build, 0, unroll=64)

    hn = hn_scr[...]
    for cidx in range(I1_PER_STEP // 2):
        act = _dot(hn, ut_ref[:, cidx * 2 * LANES:(cidx + 1) * 2 * LANES])
        ge = 0.5 * act * (1.0 + lax.erf(act * (2.0 ** -0.5)))
        for half in range(2):
            ii = 2 * cidx + half
            g_t = g_scr[pl.ds(j * I1_PER_STEP + ii, T, stride=G_PITCH), :]
            coef_scr[:, ii * LANES:(ii + 1) * LANES] = (g_t * ge[:, half * LANES:(half + 1) * LANES]).astype(BF16)
    o_ref[...] += _dot(coef_scr[...], v_ref[...])

    @pl.when(j == pl.num_programs(1) - 1)
    def _():
        o_ref[...] = _rmsnorm(x_ref[...] + o_ref[...], fw_ref[...])


def _peer_mix(x, gate, i1, i2, norm_w, final_w, ut_bf16, v_bf16, *, T=512):
    m = x.shape[0]
    T = min(T, m)
    ne = I1_PER_STEP * N_KEYS
    body = functools.partial(_peer_body, T=T)
    once = pl.Buffered(1)
    row = lambda w: pl.BlockSpec((T, w), lambda i, j: (i, 0), pipeline_mode=once)
    vec = pl.BlockSpec((1, 1024), lambda i, j: (0, 0), pipeline_mode=once)
    return pl.pallas_call(
        body,
        grid=(m // T, N_KEYS // I1_PER_STEP),
        in_specs=[row(1024), row(128), row(128), row(128), vec, vec,
                  pl.BlockSpec((1024, ne), lambda i, j: (0, j)), pl.BlockSpec((ne, 1024), lambda i, j: (j, 0))],
        out_specs=pl.BlockSpec((T, 1024), lambda i, j: (i, 0)),
        out_shape=jax.ShapeDtypeStruct((m, 1024), F32),
        scratch_shapes=[pltpu.VMEM((T, 1024), BF16), pltpu.VMEM((T * G_PITCH, LANES), F32),
                        pltpu.VMEM((T, 1024), BF16)],
        compiler_params=_cparams(("parallel", "arbitrary")),
        name="peer_mix",
    )(x, gate, i1, i2, norm_w.reshape(1, 1024), final_w.reshape(1, 1024), ut_bf16, v_bf16)


def _pad_rows(x, rows):
    return jnp.pad(x, ((0, 0), (rows - x.shape[1], 0), (0, 0)))


def _block(x, mem_k, mem_v, conv_prev, ssm_prev, shift_prev, wkv_prev, p, *, ssd_l, rwkv_c):
    b, L, _ = x.shape
    m = b * L
    x2d = x.reshape(m, D_MODEL)
    proj = _linear(x2d, p["w_in"], norm_w=p["norm_mix_w"], tm=2048, tn=1152, name="in_proj")
    proj3 = proj.reshape(b, L, D_IN_PAD)
    lp = max(L, SUBLANES)
    n_valid = L if L < SUBLANES else max(ssd_l, rwkv_c)
    proj3p = proj3 if lp == L else jnp.pad(proj3, ((0, 0), (0, lp - L), (0, 0)))

    nb = SEQS_PER_STEP if L == 1 else 1
    y_ssd, ssm_new = _ssd(proj3p, _pad_rows(conv_prev, 8), ssm_prev, p,
                          l=ssd_l, n_valid=min(n_valid, ssd_l), nb=nb)
    y_rwkv, wkv_new = _rwkv(proj3p, _pad_rows(shift_prev[:, None, :], 8), wkv_prev, p,
                            C=rwkv_c, n_valid=min(n_valid, rwkv_c), nb=nb)
    tail = proj3[:, max(L - (CONV_K - 1), 0):]
    xbc = jnp.concatenate([tail[..., 1024:2048], tail[..., 7168:7680]], axis=-1)
    rw = jnp.concatenate([tail[..., 2048:5120], tail[..., 7680:7936]], axis=-1)
    conv_new = jnp.concatenate([conv_prev, xbc], axis=1)[:, -(CONV_K - 1):]
    shift_new = rw[:, -1]

    if L == 1:
        x1 = _merge_out(x2d, proj, y_ssd[:, 0], y_rwkv[:, 0], p["w_out"])
        q = _linear(x1, p["w_mq"], norm_w=p["norm_mem_w"], name="mem_q")
        o = _attn_step(q.reshape(b, MEM_HEADS, MEM_HD), mem_k, mem_v)
        x2 = _linear(o.reshape(m, D_MODEL), p["w_mo"], res=x1, name="mem_o")
        pq = _linear(x2, p["w_pq"], norm_w=p["norm_ffn_w"], name="peer_q")
    else:
        x2, pq = _tail(x, proj3, y_ssd, y_rwkv, mem_k.reshape(b, N_MEM, D_MODEL), mem_v.reshape(b, N_MEM, D_MODEL), p)
        x2, pq = x2.reshape(m, D_MODEL), pq.reshape(m, 2 * D_MODEL)
    gate_t, i1_t, i2_t = _peer_topk(pq, p["sub_keys"])
    y = _peer_mix(x2, gate_t.T, i1_t.T, i2_t.T, p["norm_ffn_w"], p["norm_final_w"], p["expert_ut"], p["expert_v"])
    return y.reshape(b, L, D_MODEL), conv_new, ssm_new, shift_new, wkv_new


def kernel(x_prompt, x_sample, mem_prompt, state_ssm, state_conv, state_wkv, state_shift, cache_mem_k, cache_mem_v, norm_mix_w, w_in, conv_w, conv_b, dt_bias, a_log, d_skip, ssd_norm_w, rwkv_mu, rwkv_w0, rwkv_w2, rwkv_a0, rwkv_a2, rwkv_g2, rwkv_k_k, rwkv_k_a, rwkv_r_k, rwkv_ln_w, rwkv_ln_b, w_out, norm_mem_w, mem_norm_w, w_mk, w_mv, w_mq, w_mo, norm_ffn_w, w_pq, sub_keys, expert_u, expert_v, norm_final_w):
    assert w_in.shape[0] == 1, "single-layer problem"
    bp = x_prompt.shape[0]
    bs = x_sample.shape[0]
    w_in_perm = _permute_w_in(w_in[0])
    pad16 = lambda a: jnp.pad(a.reshape(1, H_SSD), ((0, 0), (0, LANES - H_SSD)))
    head_of_lane = jnp.arange(D_MODEL) // SSD_HEADDIM
    zeros64 = jnp.zeros((64, D_MODEL), F32)
    hi_lo = lambda w: jnp.stack(_split_bf16(w))
    p = {
        "norm_mix_w": norm_mix_w[0], "w_in": w_in_perm,
        "conv_w": conv_w[0], "conv_b": conv_b[0].reshape(1, CONV_DIM),
        "dt_bias": pad16(dt_bias[0]), "a_log": pad16(a_log[0]),
        "d_skip_e": jnp.repeat(d_skip[0], SSD_HEADDIM).reshape(1, D_MODEL),
        "ssd_norm_w": ssd_norm_w[0].reshape(1, D_MODEL),
        "expand": (jnp.arange(LANES)[:, None] == head_of_lane[None, :]).astype(BF16),
        "rwkv_mu": rwkv_mu[0].reshape(1, RWKV_PROJ), "rwkv_w0": rwkv_w0[0].reshape(1, D_MODEL),
        "rwkv_w2p": hi_lo(jnp.concatenate([rwkv_w2[0], zeros64], axis=0)),
        "rwkv_a0": rwkv_a0[0].reshape(1, D_MODEL),
        "rwkv_a2p": hi_lo(jnp.concatenate([zeros64, rwkv_a2[0]], axis=0)),
        "rwkv_g2": hi_lo(rwkv_g2[0]), "rwkv_k_k": rwkv_k_k[0].reshape(1, D_MODEL),
        "rwkv_k_a": rwkv_k_a[0].reshape(1, D_MODEL), "rwkv_r_k": rwkv_r_k[0].reshape(1, D_MODEL),
        "rwkv_ln_w": rwkv_ln_w[0].reshape(1, D_MODEL), "rwkv_ln_b": rwkv_ln_b[0].reshape(1, D_MODEL),
        "seg": (head_of_lane[:LANES, None] == head_of_lane[None, :LANES]).astype(BF16),
        "w_out": w_out[0], "norm_mem_w": norm_mem_w[0], "w_mq": w_mq[0], "w_mo": w_mo[0],
        "norm_ffn_w": norm_ffn_w[0], "w_pq": w_pq[0], "sub_keys": sub_keys[0],
        "expert_ut": expert_u[0].astype(BF16).T, "expert_v": expert_v[0].astype(BF16),
        "norm_final_w": norm_final_w,
    }

    mem2d = mem_prompt.reshape(bp * N_MEM, D_MODEL)
    mem_k = _linear(mem2d, w_mk[0], norm_w=mem_norm_w[0], name="mem_k").reshape(bp, N_MEM, MEM_HEADS, MEM_HD)
    mem_v = _linear(mem2d, w_mv[0], norm_w=mem_norm_w[0], name="mem_v").reshape(bp, N_MEM, MEM_HEADS, MEM_HD)
    y_p, conv_p, ssm_p, shift_p, wkv_p = _block(
        x_prompt, mem_k, mem_v,
        jnp.zeros((bp, CONV_K - 1, CONV_DIM), F32), jnp.zeros((bp, H_SSD, SSD_HEADDIM, SSD_STATE), F32),
        jnp.zeros((bp, RWKV_PROJ), F32), jnp.zeros((bp, H_RWKV, HD_RWKV, HD_RWKV), F32), p,
        ssd_l=SSD_CHUNK, rwkv_c=RWKV_CHUNK)
    y_s, conv_s, ssm_s, shift_s, wkv_s = _block(
        x_sample, cache_mem_k[0], cache_mem_v[0], state_conv[0], state_ssm[0], state_shift[0], state_wkv[0], p,
        ssd_l=SUBLANES, rwkv_c=SUBLANES)
    return (y_p, y_s, ssm_p[None], conv_p[None], wkv_p[None], shift_p[None], mem_k[None], mem_v[None],
            ssm_s[None], conv_s[None], wkv_s[None], shift_s[None])
```

```python
import functools

import jax
import jax.numpy as jnp
from jax import lax
from jax.experimental import pallas as pl
from jax.experimental.pallas import tpu as pltpu

F32 = jnp.float32
BF16 = jnp.bfloat16
HI = lax.Precision.HIGHEST

D_MODEL = 1024
H_SSD = 16
SSD_HEADDIM = 64
SSD_STATE = 128
CONV_K = 4
CONV_DIM = 1536
H_RWKV = 16
HD_RWKV = 64
RWKV_PROJ = 3328
RWKV_LN_EPS = 64e-5
N_MEM = 256
MEM_HEADS = 4
MEM_HD = 256
N_KEYS = 128
PEER_HEADS = 8
PEER_TOPK = 16
NORM_EPS = 1e-6

LANES = 128
SUBLANES = 8
VMEM_LIMIT = 56 * 1024 * 1024

D_IN_PAD = 8064
COL_Z, COL_XS, COL_R, COL_K, COL_V, COL_GS, COL_GR = 0, 1, 2, 3, 4, 5, 6
COL_BC = 14
COL_LR = 30
COL_DT = 62

SSD_CHUNK = 128
RWKV_CHUNK = 64
SEQS_PER_STEP = 8
RWKV_SEQS_PER_STEP = 4


def _cparams(sem):
    return pltpu.CompilerParams(dimension_semantics=sem, vmem_limit_bytes=VMEM_LIMIT)


def _dot(a, b, prec=None):
    return jnp.dot(a, b, preferred_element_type=F32, precision=prec)


def _dot_nt(a, b, prec=None):
    return lax.dot_general(a, b, (((1,), (1,)), ((), ())), preferred_element_type=F32, precision=prec)


def _dot_tn(a, b, prec=None):
    return lax.dot_general(a, b, (((0,), (0,)), ((), ())), preferred_element_type=F32, precision=prec)


def _split_bf16(x):
    hi = x.astype(BF16)
    lo = (x - hi.astype(F32)).astype(BF16)
    return hi, lo


def _dot_onehot(x, sel_bf16):
    hi, lo = _split_bf16(x)
    return _dot(hi, sel_bf16) + _dot(lo, sel_bf16)


NN = ((1,), (0,))
NT = ((1,), (1,))
TN = ((0,), (0,))


def _cmm(a, b, dims):
    return lax.dot_general(a.astype(BF16), b.astype(BF16), (dims, ((), ())), preferred_element_type=F32)


def _softplus(x):
    return jnp.maximum(x, 0.0) + jnp.log1p(jnp.exp(-jnp.abs(x)))


def _sigmoid(x):
    return 1.0 / (1.0 + jnp.exp(-x))


def _rmsnorm(x, w):
    return x * lax.rsqrt(jnp.mean(x * x, axis=-1, keepdims=True) + NORM_EPS) * w


def _iota(shape, dim):
    return lax.broadcasted_iota(jnp.int32, shape, dim)


def _per_sequence(nb, per_seq, init, main, fini=None):
    def wrapped(*refs):
        views = [[r.at[s] if flag else r for r, flag in zip(refs, per_seq)] for s in range(nb)]
        c = pl.program_id(1)

        @pl.when(c == 0)
        def _():
            for v in views:
                init(*v)

        live = [main(*v) for v in views]
        while live:
            live = [g for g in live if next(g, True) is None]
        if fini is not None:
            @pl.when(c == pl.num_programs(1) - 1)
            def _():
                for v in views:
                    fini(*v)
    return wrapped


def _linear_body(*refs, has_norm, has_res):
    it = iter(refs)
    x_ref = next(it)
    nw_ref = next(it) if has_norm else None
    w_ref = next(it)
    res_ref = next(it) if has_res else None
    o_ref = next(it)
    x = x_ref[...]
    if has_norm:
        x = _rmsnorm(x, nw_ref[...])
    acc = _dot(x.astype(BF16), w_ref[...])
    if has_res:
        acc = acc + res_ref[...]
    o_ref[...] = acc


def _linear(x, w, *, norm_w=None, res=None, tm=512, tn=1024, name="linear"):
    m, k = x.shape
    n = w.shape[1]
    tm = min(tm, m)
    tn = min(tn, n)
    assert m % tm == 0 and n % tn == 0
    args = [x]
    specs = [pl.BlockSpec((tm, k), lambda i, j: (i, 0))]
    if norm_w is not None:
        args.append(norm_w.reshape(1, k))
        specs.append(pl.BlockSpec((1, k), lambda i, j: (0, 0)))
    args.append(w.astype(BF16))
    specs.append(pl.BlockSpec((k, tn), lambda i, j: (0, j)))
    if res is not None:
        args.append(res)
        specs.append(pl.BlockSpec((tm, tn), lambda i, j: (i, j)))
    body = functools.partial(_linear_body, has_norm=norm_w is not None, has_res=res is not None)
    return pl.pallas_call(
        body,
        grid=(m // tm, n // tn),
        in_specs=specs,
        out_specs=pl.BlockSpec((tm, tn), lambda i, j: (i, j)),
        out_shape=jax.ShapeDtypeStruct((m, n), F32),
        compiler_params=_cparams(("parallel", "arbitrary")),
        name=name,
    )(*args)


W_IN_SEGMENTS = ((0, 2048), (2576, 5648), (5904, 7952), (2048, 2560), (5648, 5904), (2560, 2576))


def _permute_body(w_ref, o_ref):
    w = w_ref[...]
    parts = [w[:, a:b] for a, b in W_IN_SEGMENTS]
    parts.append(jnp.zeros((w.shape[0], LANES - H_SSD), F32))
    o_ref[...] = jnp.concatenate(parts, axis=1).astype(BF16)


def _permute_w_in(w):
    k, n = w.shape
    tr = 128
    return pl.pallas_call(
        _permute_body,
        grid=(k // tr,),
        in_specs=[pl.BlockSpec((tr, n), lambda i: (i, 0))],
        out_specs=pl.BlockSpec((tr, D_IN_PAD), lambda i: (i, 0)),
        out_shape=jax.ShapeDtypeStruct((k, D_IN_PAD), BF16),
        compiler_params=_cparams(("parallel",)),
        name="permute_w_in",
    )(w)


def _ssd_init(z_ref, xs_ref, bc_ref, dt_ref, convprev_ref, ssmprev_ref, convw_ref, convb_ref, dtb_ref,
              alog_ref, dskip_ref, normw_ref, expand_ref, y_ref, state_ref, ext_ref):
    ext_ref[0:8, :] = convprev_ref[...]
    state_ref[...] = ssmprev_ref[...]


def _ssd_main(z_ref, xs_ref, bc_ref, dt_ref, convprev_ref, ssmprev_ref, convw_ref, convb_ref, dtb_ref,
              alog_ref, dskip_ref, normw_ref, expand_ref, y_ref, state_ref, ext_ref, *, l, n_valid):
    ext_ref[8:8 + l, 0:D_MODEL] = xs_ref[...]
    ext_ref[8:8 + l, D_MODEL:CONV_DIM] = bc_ref[...]
    conv = convb_ref[...]
    for i in range(CONV_K):
        conv = conv + ext_ref[pl.ds(5 + i, l), :] * convw_ref[i:i + 1, :]
    ext_ref[0:8, :] = ext_ref[l:l + 8, :]

    yield
    act = conv * _sigmoid(conv)
    xs = act[:, 0:D_MODEL]
    bm = act[:, D_MODEL:D_MODEL + 256]
    cm = act[:, D_MODEL + 256:CONV_DIM]

    dt = _softplus(dt_ref[...] + dtb_ref[...])
    if n_valid < l:
        dt = jnp.where(_iota((l, LANES), 0) < n_valid, dt, 0.0)
    ad = dt * (-jnp.exp(alog_ref[...]))
    row = _iota((l, l), 0)
    col = _iota((l, l), 1)
    causal = row >= col
    cs = _dot_sel3(jnp.where(causal, 1.0, 0.0).astype(BF16), ad)
    eye = jnp.where(_iota((LANES, LANES), 0) == _iota((LANES, LANES), 1), 1.0, 0.0).astype(BF16)
    cs_parts = _split3(cs)
    cs_t = sum(_dot_nt(eye, part) for part in cs_parts)
    expand = expand_ref[...]
    dt_e = sum(_dot(part, expand) for part in _split3(dt))
    cs_e = sum(_dot(part, expand) for part in cs_parts)
    yield
    xdt = xs * dt_e
    decay_in = jnp.exp(cs_e)
    decay_out = jnp.exp(cs_e[l - 1:l, :] - cs_e)
    xdt_dec = (xdt * decay_out).astype(BF16)
    lane_lo = _iota((l, LANES), 1) < SSD_HEADDIM
    row_lo = _iota((LANES, 1), 0) < SSD_HEADDIM

    ys = []
    for g in range(2):
        bg = bm[:, g * LANES:(g + 1) * LANES].astype(BF16)
        cg = cm[:, g * LANES:(g + 1) * LANES].astype(BF16)
        cb = _dot_nt(cg, bg)
        for pr in range(4 * g, 4 * g + 4):
            sl = slice(pr * LANES, (pr + 1) * LANES)
            h0, h1 = 2 * pr, 2 * pr + 1
            m0 = jnp.where(causal, jnp.exp(cs[:, h0:h0 + 1] - cs_t[h0:h0 + 1, :]), 0.0) * cb
            m1 = jnp.where(causal, jnp.exp(cs[:, h1:h1 + 1] - cs_t[h1:h1 + 1, :]), 0.0) * cb
            xp = xdt[:, sl]
            x_lo = jnp.where(lane_lo, xp, 0.0).astype(BF16)
            x_hi = jnp.where(lane_lo, 0.0, xp).astype(BF16)
            y_diag = _dot(m0.astype(BF16), x_lo) + _dot(m1.astype(BF16), x_hi)
            s_pair = state_ref[2 * pr:2 * pr + 2].reshape(LANES, SSD_STATE)
            y_off = _dot_nt(cg, s_pair.astype(BF16)) * decay_in[:, sl]
            ys.append(y_diag + y_off + dskip_ref[:, sl] * xs[:, sl])
            upd = _dot_tn(xdt_dec[:, sl], bg)
            last = cs[l - 1:l, :]
            row_decay = jnp.exp(jnp.where(row_lo, last[:, h0:h0 + 1], last[:, h1:h1 + 1]))
            state_ref[2 * pr:2 * pr + 2] = (row_decay * s_pair + upd).reshape(2, SSD_HEADDIM, SSD_STATE)
            yield
    y = jnp.concatenate(ys, axis=1)
    z = z_ref[...]
    y_ref[...] = _rmsnorm(y * (z * _sigmoid(z)), normw_ref[...])


def _ssd(proj3, conv_prev8, ssm_prev, p, *, l, n_valid, nb):
    b, lp, _ = proj3.shape
    nc = lp // l
    full = lambda shape: pl.BlockSpec(shape, lambda i, c: (0,) * len(shape))
    state = pl.BlockSpec((nb, H_SSD, SSD_HEADDIM, SSD_STATE), lambda i, c: (i, 0, 0, 0))
    body = _per_sequence(nb, [True] * 6 + [False] * 7 + [True] * 3, _ssd_init,
                         functools.partial(_ssd_main, l=l, n_valid=n_valid))
    return pl.pallas_call(
        body,
        grid=(b // nb, nc),
        in_specs=[
            pl.BlockSpec((nb, l, 1024), lambda i, c: (i, c, COL_Z)),
            pl.BlockSpec((nb, l, 1024), lambda i, c: (i, c, COL_XS)),
            pl.BlockSpec((nb, l, 512), lambda i, c: (i, c, COL_BC)),
            pl.BlockSpec((nb, l, 128), lambda i, c: (i, c, COL_DT)),
            pl.BlockSpec((nb, 8, CONV_DIM), lambda i, c: (i, 0, 0)),
            state,
            full((CONV_K, CONV_DIM)), full((1, CONV_DIM)), full((1, 128)), full((1, 128)),
            full((1, 1024)), full((1, 1024)), full((128, 1024)),
        ],
        out_specs=[pl.BlockSpec((nb, l, 1024), lambda i, c: (i, c, 0)), state],
        out_shape=[jax.ShapeDtypeStruct((b, lp, 1024), F32),
                   jax.ShapeDtypeStruct((b, H_SSD, SSD_HEADDIM, SSD_STATE), F32)],
        scratch_shapes=[pltpu.VMEM((nb, l + 8, CONV_DIM), F32)],
        compiler_params=_cparams(("parallel", "arbitrary")),
        name="ssd",
    )(proj3, proj3, proj3, proj3, conv_prev8, ssm_prev, p["conv_w"], p["conv_b"], p["dt_bias"], p["a_log"],
      p["d_skip_e"], p["ssd_norm_w"], p["expand"])


def _dot_w(x, w_ref):
    xh, xl = _split_bf16(x)
    return _dot(xh, w_ref[0]) + _dot(xl, w_ref[0]) + _dot(xh, w_ref[1])


def _split3(x):
    x1 = x.astype(BF16)
    r1 = x - x1.astype(F32)
    x2 = r1.astype(BF16)
    x3 = (r1 - x2.astype(F32)).astype(BF16)
    return x1, x2, x3


def _dot_sel3(sel_bf16, x):
    return sum(_dot(sel_bf16, part) for part in _split3(x))


def _head_sum(x, seg_pair):
    rows = x.shape[0]
    n = x.shape[1] // LANES
    stacked = jnp.concatenate([x[:, i * LANES:(i + 1) * LANES] for i in range(n)], axis=0)
    s = _dot_onehot(stacked, seg_pair)
    return jnp.concatenate([s[i * rows:(i + 1) * rows] for i in range(n)], axis=1)


def _rwkv_init(r_ref, k_ref, v_ref, lr_ref, shprev_ref, wkvprev_ref, mu_ref, w0_ref, w2_ref, a0_ref, a2_ref,
               g2_ref, kk_ref, ka_ref, rk_ref, lnw_ref, lnb_ref, seg_ref, y_ref, wkv_ref, ext_ref, state_ref):
    same_head = (_iota((LANES, LANES), 0) < HD_RWKV) == (_iota((LANES, LANES), 1) < HD_RWKV)
    ext_ref[0:8, :] = shprev_ref[...]
    for p in range(8):
        s_c = wkvprev_ref[2 * p:2 * p + 2].reshape(LANES, HD_RWKV)
        state_ref[p] = jnp.where(same_head, jnp.concatenate([s_c, s_c], axis=1), 0.0)


def _rwkv_fini(r_ref, k_ref, v_ref, lr_ref, shprev_ref, wkvprev_ref, mu_ref, w0_ref, w2_ref, a0_ref, a2_ref,
               g2_ref, kk_ref, ka_ref, rk_ref, lnw_ref, lnb_ref, seg_ref, y_ref, wkv_ref, ext_ref, state_ref):
    for p in range(8):
        s_p = state_ref[p]
        wkv_ref[2 * p] = s_p[0:HD_RWKV, 0:HD_RWKV]
        wkv_ref[2 * p + 1] = s_p[HD_RWKV:LANES, HD_RWKV:LANES]


def _rwkv_main(r_ref, k_ref, v_ref, lr_ref, shprev_ref, wkvprev_ref, mu_ref, w0_ref, w2_ref, a0_ref, a2_ref,
               g2_ref, kk_ref, ka_ref, rk_ref, lnw_ref, lnb_ref, seg_ref, y_ref, wkv_ref, ext_ref, state_ref,
               *, C, n_valid):
    ext_ref[8:8 + C, 0:1024] = r_ref[...]
    ext_ref[8:8 + C, 1024:2048] = k_ref[...]
    ext_ref[8:8 + C, 2048:3072] = v_ref[...]
    ext_ref[8:8 + C, 3072:3328] = lr_ref[...]
    cur = ext_ref[8:8 + C, :]
    prev = ext_ref[pl.ds(7, C), :]
    mixed = cur + (prev - cur) * mu_ref[...]
    ext_ref[0:8, :] = ext_ref[C:C + 8, :]

    yield
    r = mixed[:, 0:1024]
    k = mixed[:, 1024:2048]
    v = mixed[:, 2048:3072]
    wa = mixed[:, 3072:3200]
    gl = mixed[:, 3200:3328]
    seg = seg_ref[...]

    wlog = -_softplus(-(w0_ref[...] + _dot_w(jnp.tanh(wa), w2_ref))) - 0.5
    lw = -jnp.exp(wlog)
    a_sig = _sigmoid(a0_ref[...] + _dot_w(wa, a2_ref))
    g = _dot_w(_sigmoid(gl), g2_ref)
    yield
    kk = k * kk_ref[...]
    kk = kk * lax.rsqrt(jnp.maximum(_head_sum(kk * kk, seg), 1e-24))
    kh = k * (1.0 + (a_sig - 1.0) * ka_ref[...])
    av = -kk
    bv = kk * a_sig
    bonus = _head_sum(r * kh * rk_ref[...], seg) * v
    if n_valid < C:
        valid = _iota((C, 1024), 0) < n_valid
        lw = jnp.where(valid, lw, 0.0)
        av = jnp.where(valid, av, 0.0)
        bv = jnp.where(valid, bv, 0.0)
        kh = jnp.where(valid, kh, 0.0)
        v = jnp.where(valid, v, 0.0)

    yield
    causal_c = jnp.where(_iota((C, C), 0) >= _iota((C, C), 1), 1.0, 0.0).astype(BF16)
    cl = _dot_sel3(causal_c, lw)
    e_pos = jnp.exp(cl)
    e_neg = jnp.exp(-cl)
    rt = r * e_pos
    at = av * jnp.exp(cl - lw)
    bt = bv * e_neg
    kt = kh * e_neg
    p_end = e_pos[C - 1:C, :]

    R2 = 2 * C
    lane_lo = _iota((C, LANES), 1) < HD_RWKV

    def stack(x):
        return jnp.concatenate([jnp.where(lane_lo, x, 0.0), jnp.where(lane_lo, 0.0, x)], axis=0)

    pairs = range(8)
    sls = [slice(p * LANES, (p + 1) * LANES) for p in pairs]
    b_s = [stack(bt[:, sl]) for sl in sls]
    k_s = [stack(kt[:, sl]) for sl in sls]
    v_s = [stack(v[:, sl]) for sl in sls]
    yield
    s_bd = [state_ref[p] for p in pairs]
    wide = R2 == LANES
    if wide:
        tok = _iota((C, LANES), 1) & (C - 1)
        row = _iota((C, LANES), 0)
        strict = tok < row
        incl = tok <= row
        ar = [jnp.concatenate([at[:, sl], rt[:, sl]], axis=0) for sl in sls]
        bk = [jnp.concatenate([b_s[p], k_s[p]], axis=0) for p in pairs]
        quad = [_cmm(ar[p], bk[p], NT) for p in pairs]
        ars = [_cmm(ar[p], s_bd[p], NT) for p in pairs]
        m_ak = [jnp.where(strict, quad[p][0:C, R2:], 0.0) for p in pairs]
        w = [ars[p][0:C] + _cmm(m_ak[p], v_s[p], NN) for p in pairs]
    else:
        rr = _iota((R2, R2), 0)
        cc = _iota((R2, R2), 1)
        strict = rr > cc
        incl = rr >= cc
        a_s = [stack(at[:, sl]) for sl in sls]
        r_s = [stack(rt[:, sl]) for sl in sls]
        m_ak = [jnp.where(strict, _cmm(a_s[p], k_s[p], NT), 0.0) for p in pairs]
        w = [_cmm(a_s[p], s_bd[p], NT) + _cmm(m_ak[p], v_s[p], NN) for p in pairs]
    yield
    if n_valid > 1:
        assert wide
        n_sq = C.bit_length() - 1
        xp = [jnp.where(strict, quad[p][0:C, 0:R2], 0.0) for p in pairs]
        tp = [jnp.where(tok == row, 1.0, 0.0) + xp[p] for p in pairs]
        xp = [_cmm(xp[p], stack(xp[p]), NN) for p in pairs]
        for _ in range(1, n_sq - 1):
            prod = [_cmm(jnp.concatenate([xp[p], tp[p]], axis=0), stack(xp[p]), NN) for p in pairs]
            tp = [tp[p] + prod[p][C:R2] for p in pairs]
            xp = [prod[p][0:C] for p in pairs]
            yield
        tp = [tp[p] + _cmm(tp[p], stack(xp[p]), NN) for p in pairs]
        u = [_cmm(tp[p], stack(w[p]), NN) for p in pairs]
    else:
        u = w
    yield
    ys = []
    for p in pairs:
        if wide:
            n_r = jnp.where(jnp.concatenate([incl, incl], axis=1), quad[p][C:R2, :], 0.0)
            uv = jnp.concatenate([stack(u[p]), v_s[p]], axis=0)
            ys.append(ars[p][C:R2] + _cmm(n_r, uv, NN))
            upd = _cmm(uv, bk[p], TN)
        else:
            n_rb = jnp.where(incl, _cmm(r_s[p], b_s[p], NT), 0.0)
            n_rk = jnp.where(incl, _cmm(r_s[p], k_s[p], NT), 0.0)
            y2 = _cmm(r_s[p], s_bd[p], NT) + _cmm(n_rb, u[p], NN) + _cmm(n_rk, v_s[p], NN)
            ys.append(y2[0:C] + y2[C:R2])
            upd = _cmm(u[p], b_s[p], TN) + _cmm(v_s[p], k_s[p], TN)
        state_ref[p] = (s_bd[p] + upd) * p_end[:, sls[p]]
    yield
    yr = jnp.concatenate(ys, axis=1)
    mean = _head_sum(yr, seg) * (1.0 / HD_RWKV)
    d = yr - mean
    var = _head_sum(d * d, seg) * (1.0 / HD_RWKV)
    yn = d * lax.rsqrt(var + RWKV_LN_EPS) * lnw_ref[...] + lnb_ref[...]
    y_ref[...] = (yn + bonus) * g


def _rwkv(proj3, shift_prev8, wkv_prev, p, *, C, n_valid, nb):
    b, lp, _ = proj3.shape
    nc = lp // C
    full = lambda shape: pl.BlockSpec(shape, lambda i, c: (0,) * len(shape))
    state = pl.BlockSpec((nb, H_RWKV, HD_RWKV, HD_RWKV), lambda i, c: (i, 0, 0, 0))
    body = _per_sequence(nb, [True] * 6 + [False] * 12 + [True] * 4, _rwkv_init,
                         functools.partial(_rwkv_main, C=C, n_valid=n_valid), _rwkv_fini)
    return pl.pallas_call(
        body,
        grid=(b // nb, nc),
        in_specs=[
            pl.BlockSpec((nb, C, 1024), lambda i, c: (i, c, COL_R)),
            pl.BlockSpec((nb, C, 1024), lambda i, c: (i, c, COL_K)),
            pl.BlockSpec((nb, C, 1024), lambda i, c: (i, c, COL_V)),
            pl.BlockSpec((nb, C, 256), lambda i, c: (i, c, COL_LR)),
            pl.BlockSpec((nb, 8, RWKV_PROJ), lambda i, c: (i, 0, 0)),
            state,
            full((1, RWKV_PROJ)), full((1, 1024)), full((2, 128, 1024)), full((1, 1024)), full((2, 128, 1024)),
            full((2, 128, 1024)), full((1, 1024)), full((1, 1024)), full((1, 1024)), full((1, 1024)), full((1, 1024)),
            full((128, 128)),
        ],
        out_specs=[pl.BlockSpec((nb, C, 1024), lambda i, c: (i, c, 0)), state],
        out_shape=[jax.ShapeDtypeStruct((b, lp, 1024), F32), jax.ShapeDtypeStruct((b, H_RWKV, HD_RWKV, HD_RWKV), F32)],
        scratch_shapes=[pltpu.VMEM((nb, C + 8, RWKV_PROJ), F32), pltpu.VMEM((nb, 8, 128, 128), F32)],
        compiler_params=_cparams(("parallel", "arbitrary")),
        name="rwkv",
    )(proj3, proj3, proj3, proj3, shift_prev8, wkv_prev, p["rwkv_mu"], p["rwkv_w0"], p["rwkv_w2p"], p["rwkv_a0"],
      p["rwkv_a2p"], p["rwkv_g2"], p["rwkv_k_k"], p["rwkv_k_a"], p["rwkv_r_k"], p["rwkv_ln_w"], p["rwkv_ln_b"],
      p["seg"])


def _merge_body(x_ref, gs_ref, gr_ref, ys_ref, yr_ref, w_ref, o_ref):
    merged = _sigmoid(gs_ref[...]) * ys_ref[...] + _sigmoid(gr_ref[...]) * yr_ref[...]
    o_ref[...] = x_ref[...] + _dot(merged.astype(BF16), w_ref[...])


def _merge_out(x, proj, y_ssd, y_rwkv, w_out, *, tm=512):
    m = x.shape[0]
    tm = min(tm, m)
    row = lambda j: pl.BlockSpec((tm, 1024), lambda i: (i, j))
    return pl.pallas_call(
        _merge_body,
        grid=(m // tm,),
        in_specs=[row(0), row(COL_GS), row(COL_GR), row(0), row(0), pl.BlockSpec((1024, 1024), lambda i: (0, 0))],
        out_specs=row(0),
        out_shape=jax.ShapeDtypeStruct((m, 1024), F32),
        compiler_params=_cparams(("parallel",)),
        name="merge_out",
    )(x, proj, proj, y_ssd, y_rwkv, w_out.astype(BF16))


def _mem_attention(q, k_ref, v_ref):
    outs = []
    for h in range(MEM_HEADS):
        sl = slice(h * MEM_HD, (h + 1) * MEM_HD)
        s = _dot_nt(q[:, sl].astype(BF16), k_ref[:, sl].astype(BF16)) * (MEM_HD ** -0.5)
        s = s - jnp.max(s, axis=-1, keepdims=True)
        e = jnp.exp(s)
        prob = e / jnp.sum(e, axis=-1, keepdims=True)
        outs.append(_dot(prob.astype(BF16), v_ref[:, sl].astype(BF16)))
    return jnp.concatenate(outs, axis=1)


def _tail_body(x_ref, gs_ref, gr_ref, ys_ref, yr_ref, k_ref, v_ref, wout_ref, nmem_ref, wmq_ref, wmo_ref,
               nffn_ref, wpq_ref, x2_ref, pq_ref):
    merged = _sigmoid(gs_ref[...]) * ys_ref[...] + _sigmoid(gr_ref[...]) * yr_ref[...]
    x1 = x_ref[...] + _dot(merged.astype(BF16), wout_ref[...])
    q = _dot(_rmsnorm(x1, nmem_ref[...]).astype(BF16), wmq_ref[...])
    x2 = x1 + _dot(_mem_attention(q, k_ref, v_ref).astype(BF16), wmo_ref[...])
    x2_ref[...] = x2
    pq_ref[...] = _dot(_rmsnorm(x2, nffn_ref[...]).astype(BF16), wpq_ref[...])


def _tail(x3, proj3, y_ssd, y_rwkv, mem_k, mem_v, p, *, tm=512):
    b, L, _ = x3.shape
    once = pl.Buffered(1)
    tile = lambda j: pl.BlockSpec((None, tm, 1024), lambda i, t: (i, t, j))
    mem = pl.BlockSpec((None, N_MEM, 1024), lambda i, t: (i, 0, 0))
    full = lambda r, c: pl.BlockSpec((r, c), lambda i, t: (0, 0), pipeline_mode=once)
    return pl.pallas_call(
        _tail_body,
        grid=(b, L // tm),
        in_specs=[tile(0), tile(COL_GS), tile(COL_GR), tile(0), tile(0), mem, mem,
                  full(1024, 1024), full(1, 1024), full(1024, 1024), full(1024, 1024), full(1, 1024),
                  full(1024, 2048)],
        out_specs=[tile(0), pl.BlockSpec((None, tm, 2048), lambda i, t: (i, t, 0))],
        out_shape=[jax.ShapeDtypeStruct((b, L, 1024), F32), jax.ShapeDtypeStruct((b, L, 2048), F32)],
        compiler_params=_cparams(("parallel", "arbitrary")),
        name="token_tail",
    )(x3, proj3, proj3, y_ssd, y_rwkv, mem_k, mem_v, p["w_out"].astype(BF16), p["norm_mem_w"].reshape(1, 1024),
      p["w_mq"].astype(BF16), p["w_mo"].astype(BF16), p["norm_ffn_w"].reshape(1, 1024), p["w_pq"].astype(BF16))


def _attn_step_body(q_ref, k_ref, v_ref, o_ref):
    s = jnp.sum(k_ref[...] * q_ref[...][:, None], axis=-1, keepdims=True) * (MEM_HD ** -0.5)
    e = jnp.exp(s - jnp.max(s, axis=1, keepdims=True))
    prob = e / jnp.sum(e, axis=1, keepdims=True)
    o_ref[...] = jnp.sum(prob * v_ref[...], axis=1)


def _attn_step(q, mem_k, mem_v, *, nb=4):
    b = q.shape[0]
    qspec = pl.BlockSpec((nb, MEM_HEADS, MEM_HD), lambda i: (i, 0, 0))
    kvspec = pl.BlockSpec((nb, N_MEM, MEM_HEADS, MEM_HD), lambda i: (i, 0, 0, 0))
    return pl.pallas_call(
        _attn_step_body,
        grid=(b // nb,),
        in_specs=[qspec, kvspec, kvspec],
        out_specs=qspec,
        out_shape=jax.ShapeDtypeStruct((b, MEM_HEADS, MEM_HD), F32),
        compiler_params=_cparams(("parallel",)),
        name="mem_attn_step",
    )(q, mem_k, mem_v)


def _stage2_pairs():
    return [(r1, r2) for r1 in range(PEER_TOPK) for r2 in range(PEER_TOPK) if (r1 + 1) * (r2 + 1) <= PEER_TOPK]


N_CAND = 56


def _topk_body(q_ref, keys_ref, gate_ref, i1_ref, i2_ref, s_scr, i_scr, cand_scr, code_scr, res_scr, *, T):
    neg = -jnp.inf
    key_row = _iota((N_KEYS, T), 0).astype(F32)
    cand_row = _iota((N_CAND, T), 0).astype(F32)
    pairs = _stage2_pairs()
    q = q_ref[...]
    for h in range(PEER_HEADS):
        for c in range(2):
            hc = 2 * h + c
            s = _dot_nt(keys_ref[hc], q[:, hc * LANES:(hc + 1) * LANES], HI)
            for r in range(PEER_TOPK):
                m = jnp.max(s, axis=0, keepdims=True)
                idx = jnp.min(jnp.where(s == m, key_row, float(N_KEYS)), axis=0, keepdims=True)
                s_scr[c, r:r + 1, :] = m
                i_scr[c, r:r + 1, :] = idx
                s = jnp.where(key_row == idx, neg, s)
        cand_scr[...] = jnp.full((N_CAND, T), neg, F32)
        code_scr[...] = jnp.zeros((N_CAND, T), F32)
        for j, (r1, r2) in enumerate(pairs):
            cand_scr[j:j + 1, :] = s_scr[0, r1:r1 + 1, :] + s_scr[1, r2:r2 + 1, :]
            code_scr[j:j + 1, :] = i_scr[0, r1:r1 + 1, :] * float(N_KEYS) + i_scr[1, r2:r2 + 1, :]
        cand = cand_scr[...]
        code = code_scr[...]
        for r in range(PEER_TOPK):
            m = jnp.max(cand, axis=0, keepdims=True)
            idx = jnp.min(jnp.where(cand == m, cand_row, float(N_CAND)), axis=0, keepdims=True)
            hit = cand_row == idx
            res_scr[0, r:r + 1, :] = m
            res_scr[1, r:r + 1, :] = jnp.sum(jnp.where(hit, code, 0.0), axis=0, keepdims=True)
            cand = jnp.where(hit, neg, cand)
        top = res_scr[0]
        e = jnp.exp(top - jnp.max(top, axis=0, keepdims=True))
        rows = slice(h * PEER_TOPK, (h + 1) * PEER_TOPK)
        gate_ref[rows, :] = e / jnp.sum(e, axis=0, keepdims=True)
        first = jnp.floor(res_scr[1] * (1.0 / N_KEYS))
        i1_ref[rows, :] = first
        i2_ref[rows, :] = res_scr[1] - first * float(N_KEYS)


def _peer_topk(q, sub_keys, *, T=256):
    m = q.shape[0]
    T = min(T, m)
    body = functools.partial(_topk_body, T=T)
    out = pl.BlockSpec((128, T), lambda i: (0, i))
    return pl.pallas_call(
        body,
        grid=(m // T,),
        in_specs=[pl.BlockSpec((T, 2048), lambda i: (i, 0)), pl.BlockSpec((16, 128, 128), lambda i: (0, 0, 0))],
        out_specs=[out, out, out],
        out_shape=[jax.ShapeDtypeStruct((128, m), F32)] * 3,
        scratch_shapes=[pltpu.VMEM((2, PEER_TOPK, T), F32), pltpu.VMEM((2, PEER_TOPK, T), F32),
                        pltpu.VMEM((N_CAND, T), F32), pltpu.VMEM((N_CAND, T), F32),
                        pltpu.VMEM((2, PEER_TOPK, T), F32)],
        compiler_params=_cparams(("parallel",)),
        name="peer_topk",
    )(q, sub_keys.reshape(16, 128, 128))


I1_PER_STEP = 8
G_PITCH = 136


def _peer_body(x_ref, gate_ref, i1_ref, i2_ref, nw_ref, fw_ref, ut_ref, v_ref, o_ref, hn_scr, g_scr, coef_scr, *, T):
    j = pl.program_id(1)

    @pl.when(j == 0)
    def _():
        hn_scr[...] = _rmsnorm(x_ref[...], nw_ref[...]).astype(BF16)
        o_ref[...] = jnp.zeros_like(o_ref)
        shape = (N_KEYS, LANES)
        key_row = _iota(shape, 0).astype(F32).astype(BF16)
        one = jnp.ones(shape, BF16)
        zero = jnp.zeros(shape, BF16)

        def build(t, carry):
            row = lambda ref: jnp.broadcast_to(ref[pl.ds(t, 1), :].astype(BF16), shape)
            a_t = jnp.where(key_row == row(i1_ref), row(gate_ref), zero)
            b_t = jnp.where(key_row == row(i2_ref), one, zero)
            g_scr[pl.ds(pl.multiple_of(t * G_PITCH, 8), N_KEYS), :] = _dot_nt(a_t, b_t)
            return carry

        lax.fori_loop(0, T, build, 0, unroll=64)

    hn = hn_scr[...]
    for cidx in range(I1_PER_STEP // 2):
        act = _dot(hn, ut_ref[:, cidx * 2 * LANES:(cidx + 1) * 2 * LANES])
        ge = 0.5 * act * (1.0 + lax.erf(act * (2.0 ** -0.5)))
        for half in range(2):
            ii = 2 * cidx + half
            g_t = g_scr[pl.ds(j * I1_PER_STEP + ii, T, stride=G_PITCH), :]
            coef_scr[:, ii * LANES:(ii + 1) * LANES] = (g_t * ge[:, half * LANES:(half + 1) * LANES]).astype(BF16)
    o_ref[...] += _dot(coef_scr[...], v_ref[...])

    @pl.when(j == pl.num_programs(1) - 1)
    def _():
        o_ref[...] = _rmsnorm(x_ref[...] + o_ref[...], fw_ref[...])


def _peer_mix(x, gate, i1, i2, norm_w, final_w, ut_bf16, v_bf16, *, T=512):
    m = x.shape[0]
    T = min(T, m)
    ne = I1_PER_STEP * N_KEYS
    body = functools.partial(_peer_body, T=T)
    once = pl.Buffered(1)
    row = lambda w: pl.BlockSpec((T, w), lambda i, j: (i, 0), pipeline_mode=once)
    vec = pl.BlockSpec((1, 1024), lambda i, j: (0, 0), pipeline_mode=once)
    return pl.pallas_call(
        body,
        grid=(m // T, N_KEYS // I1_PER_STEP),
        in_specs=[row(1024), row(128), row(128), row(128), vec, vec,
                  pl.BlockSpec((1024, ne), lambda i, j: (0, j)), pl.BlockSpec((ne, 1024), lambda i, j: (j, 0))],
        out_specs=pl.BlockSpec((T, 1024), lambda i, j: (i, 0)),
        out_shape=jax.ShapeDtypeStruct((m, 1024), F32),
        scratch_shapes=[pltpu.VMEM((T, 1024), BF16), pltpu.VMEM((T * G_PITCH, LANES), F32),
                        pltpu.VMEM((T, 1024), BF16)],
        compiler_params=_cparams(("parallel", "arbitrary")),
        name="peer_mix",
    )(x, gate, i1, i2, norm_w.reshape(1, 1024), final_w.reshape(1, 1024), ut_bf16, v_bf16)


def _pad_rows(x, rows):
    return jnp.pad(x, ((0, 0), (rows - x.shape[1], 0), (0, 0)))


def _block(x, mem_k, mem_v, conv_prev, ssm_prev, shift_prev, wkv_prev, p, *, ssd_l, rwkv_c):
    b, L, _ = x.shape
    m = b * L
    x2d = x.reshape(m, D_MODEL)
    proj = _linear(x2d, p["w_in"], norm_w=p["norm_mix_w"], tm=2048, tn=1152, name="in_proj")
    proj3 = proj.reshape(b, L, D_IN_PAD)
    lp = max(L, SUBLANES)
    n_valid = L if L < SUBLANES else max(ssd_l, rwkv_c)
    proj3p = proj3 if lp == L else jnp.pad(proj3, ((0, 0), (0, lp - L), (0, 0)))

    nb = SEQS_PER_STEP if L == 1 else 1
    y_ssd, ssm_new = _ssd(proj3p, _pad_rows(conv_prev, 8), ssm_prev, p,
                          l=ssd_l, n_valid=min(n_valid, ssd_l), nb=nb)
    y_rwkv, wkv_new = _rwkv(proj3p, _pad_rows(shift_prev[:, None, :], 8), wkv_prev, p,
                            C=rwkv_c, n_valid=min(n_valid, rwkv_c), nb=nb if L == 1 else RWKV_SEQS_PER_STEP)
    tail = proj3[:, max(L - (CONV_K - 1), 0):]
    xbc = jnp.concatenate([tail[..., 1024:2048], tail[..., 7168:7680]], axis=-1)
    rw = jnp.concatenate([tail[..., 2048:5120], tail[..., 7680:7936]], axis=-1)
    conv_new = jnp.concatenate([conv_prev, xbc], axis=1)[:, -(CONV_K - 1):]
    shift_new = rw[:, -1]

    if L == 1:
        x1 = _merge_out(x2d, proj, y_ssd[:, 0], y_rwkv[:, 0], p["w_out"])
        q = _linear(x1, p["w_mq"], norm_w=p["norm_mem_w"], name="mem_q")
        o = _attn_step(q.reshape(b, MEM_HEADS, MEM_HD), mem_k, mem_v)
        x2 = _linear(o.reshape(m, D_MODEL), p["w_mo"], res=x1, name="mem_o")
        pq = _linear(x2, p["w_pq"], norm_w=p["norm_ffn_w"], name="peer_q")
    else:
        x2, pq = _tail(x, proj3, y_ssd, y_rwkv, mem_k.reshape(b, N_MEM, D_MODEL), mem_v.reshape(b, N_MEM, D_MODEL), p)
        x2, pq = x2.reshape(m, D_MODEL), pq.reshape(m, 2 * D_MODEL)
    gate_t, i1_t, i2_t = _peer_topk(pq, p["sub_keys"])
    y = _peer_mix(x2, gate_t.T, i1_t.T, i2_t.T, p["norm_ffn_w"], p["norm_final_w"], p["expert_ut"], p["expert_v"])
    return y.reshape(b, L, D_MODEL), conv_new, ssm_new, shift_new, wkv_new


def kernel(x_prompt, x_sample, mem_prompt, state_ssm, state_conv, state_wkv, state_shift, cache_mem_k, cache_mem_v, norm_mix_w, w_in, conv_w, conv_b, dt_bias, a_log, d_skip, ssd_norm_w, rwkv_mu, rwkv_w0, rwkv_w2, rwkv_a0, rwkv_a2, rwkv_g2, rwkv_k_k, rwkv_k_a, rwkv_r_k, rwkv_ln_w, rwkv_ln_b, w_out, norm_mem_w, mem_norm_w, w_mk, w_mv, w_mq, w_mo, norm_ffn_w, w_pq, sub_keys, expert_u, expert_v, norm_final_w):
    assert w_in.shape[0] == 1, "single-layer problem"
    bp = x_prompt.shape[0]
    bs = x_sample.shape[0]
    w_in_perm = _permute_w_in(w_in[0])
    pad16 = lambda a: jnp.pad(a.reshape(1, H_SSD), ((0, 0), (0, LANES - H_SSD)))
    head_of_lane = jnp.arange(D_MODEL) // SSD_HEADDIM
    zeros64 = jnp.zeros((64, D_MODEL), F32)
    hi_lo = lambda w: jnp.stack(_split_bf16(w))
    p = {
        "norm_mix_w": norm_mix_w[0], "w_in": w_in_perm,
        "conv_w": conv_w[0], "conv_b": conv_b[0].reshape(1, CONV_DIM),
        "dt_bias": pad16(dt_bias[0]), "a_log": pad16(a_log[0]),
        "d_skip_e": jnp.repeat(d_skip[0], SSD_HEADDIM).reshape(1, D_MODEL),
        "ssd_norm_w": ssd_norm_w[0].reshape(1, D_MODEL),
        "expand": (jnp.arange(LANES)[:, None] == head_of_lane[None, :]).astype(BF16),
        "rwkv_mu": rwkv_mu[0].reshape(1, RWKV_PROJ), "rwkv_w0": rwkv_w0[0].reshape(1, D_MODEL),
        "rwkv_w2p": hi_lo(jnp.concatenate([rwkv_w2[0], zeros64], axis=0)),
        "rwkv_a0": rwkv_a0[0].reshape(1, D_MODEL),
        "rwkv_a2p": hi_lo(jnp.concatenate([zeros64, rwkv_a2[0]], axis=0)),
        "rwkv_g2": hi_lo(rwkv_g2[0]), "rwkv_k_k": rwkv_k_k[0].reshape(1, D_MODEL),
        "rwkv_k_a": rwkv_k_a[0].reshape(1, D_MODEL), "rwkv_r_k": rwkv_r_k[0].reshape(1, D_MODEL),
        "rwkv_ln_w": rwkv_ln_w[0].reshape(1, D_MODEL), "rwkv_ln_b": rwkv_ln_b[0].reshape(1, D_MODEL),
        "seg": (head_of_lane[:LANES, None] == head_of_lane[None, :LANES]).astype(BF16),
        "w_out": w_out[0], "norm_mem_w": norm_mem_w[0], "w_mq": w_mq[0], "w_mo": w_mo[0],
        "norm_ffn_w": norm_ffn_w[0], "w_pq": w_pq[0], "sub_keys": sub_keys[0],
        "expert_ut": expert_u[0].astype(BF16).T, "expert_v": expert_v[0].astype(BF16),
        "norm_final_w": norm_final_w,
    }

    mem2d = mem_prompt.reshape(bp * N_MEM, D_MODEL)
    mem_k = _linear(mem2d, w_mk[0], norm_w=mem_norm_w[0], name="mem_k").reshape(bp, N_MEM, MEM_HEADS, MEM_HD)
    mem_v = _linear(mem2d, w_mv[0], norm_w=mem_norm_w[0], name="mem_v").reshape(bp, N_MEM, MEM_HEADS, MEM_HD)
    y_p, conv_p, ssm_p, shift_p, wkv_p = _block(
        x_prompt, mem_k, mem_v,
        jnp.zeros((bp, CONV_K - 1, CONV_DIM), F32), jnp.zeros((bp, H_SSD, SSD_HEADDIM, SSD_STATE), F32),
        jnp.zeros((bp, RWKV_PROJ), F32), jnp.zeros((bp, H_RWKV, HD_RWKV, HD_RWKV), F32), p,
        ssd_l=SSD_CHUNK, rwkv_c=RWKV_CHUNK)
    y_s, conv_s, ssm_s, shift_s, wkv_s = _block(
        x_sample, cache_mem_k[0], cache_mem_v[0], state_conv[0], state_ssm[0], state_shift[0], state_wkv[0], p,
        ssd_l=SUBLANES, rwkv_c=SUBLANES)
    return (y_p, y_s, ssm_p[None], conv_p[None], wkv_p[None], shift_p[None], mem_k[None], mem_v[None],
            ssm_s[None], conv_s[None], wkv_s[None], shift_s[None])
```

```python
import functools

import jax
import jax.numpy as jnp
from jax import lax
from jax.experimental import pallas as pl
from jax.experimental.pallas import tpu as pltpu

F32 = jnp.float32
BF16 = jnp.bfloat16
HI = lax.Precision.HIGHEST

D_MODEL = 1024
H_SSD = 16
SSD_HEADDIM = 64
SSD_STATE = 128
CONV_K = 4
CONV_DIM = 1536
H_RWKV = 16
HD_RWKV = 64
RWKV_PROJ = 3328
RWKV_LN_EPS = 64e-5
N_MEM = 256
MEM_HEADS = 4
MEM_HD = 256
N_KEYS = 128
PEER_HEADS = 8
PEER_TOPK = 16
NORM_EPS = 1e-6

LANES = 128
SUBLANES = 8
VMEM_LIMIT = 56 * 1024 * 1024

D_IN_PAD = 8064
COL_Z, COL_XS, COL_R, COL_K, COL_V, COL_GS, COL_GR = 0, 1, 2, 3, 4, 5, 6
COL_BC = 14
COL_LR = 30
COL_DT = 62

SSD_CHUNK = 128
RWKV_CHUNK = 64
SEQS_PER_STEP = 8
RWKV_SEQS_PER_STEP = 4
SSD_SEQS_PER_STEP = 2


def _cparams(sem):
    return pltpu.CompilerParams(dimension_semantics=sem, vmem_limit_bytes=VMEM_LIMIT)


def _dot(a, b, prec=None):
    return jnp.dot(a, b, preferred_element_type=F32, precision=prec)


def _dot_nt(a, b, prec=None):
    return lax.dot_general(a, b, (((1,), (1,)), ((), ())), preferred_element_type=F32, precision=prec)


def _dot_tn(a, b, prec=None):
    return lax.dot_general(a, b, (((0,), (0,)), ((), ())), preferred_element_type=F32, precision=prec)


def _split_bf16(x):
    hi = x.astype(BF16)
    lo = (x - hi.astype(F32)).astype(BF16)
    return hi, lo


def _dot_onehot(x, sel_bf16):
    hi, lo = _split_bf16(x)
    return _dot(hi, sel_bf16) + _dot(lo, sel_bf16)


NN = ((1,), (0,))
NT = ((1,), (1,))
TN = ((0,), (0,))


def _cmm(a, b, dims):
    return lax.dot_general(a.astype(BF16), b.astype(BF16), (dims, ((), ())), preferred_element_type=F32)


def _softplus(x):
    return jnp.maximum(x, 0.0) + jnp.log1p(jnp.exp(-jnp.abs(x)))


def _sigmoid(x):
    return 1.0 / (1.0 + jnp.exp(-x))


def _rmsnorm(x, w):
    return x * lax.rsqrt(jnp.mean(x * x, axis=-1, keepdims=True) + NORM_EPS) * w


def _iota(shape, dim):
    return lax.broadcasted_iota(jnp.int32, shape, dim)


def _per_sequence(nb, per_seq, init, main, fini=None):
    def wrapped(*refs):
        views = [[r.at[s] if flag else r for r, flag in zip(refs, per_seq)] for s in range(nb)]
        c = pl.program_id(1)

        @pl.when(c == 0)
        def _():
            for v in views:
                init(*v)

        live = [main(*v) for v in views]
        while live:
            live = [g for g in live if next(g, True) is None]
        if fini is not None:
            @pl.when(c == pl.num_programs(1) - 1)
            def _():
                for v in views:
                    fini(*v)
    return wrapped


def _linear_body(*refs, has_norm, has_res):
    it = iter(refs)
    x_ref = next(it)
    nw_ref = next(it) if has_norm else None
    w_ref = next(it)
    res_ref = next(it) if has_res else None
    o_ref = next(it)
    x = x_ref[...]
    if has_norm:
        x = _rmsnorm(x, nw_ref[...])
    acc = _dot(x.astype(BF16), w_ref[...])
    if has_res:
        acc = acc + res_ref[...]
    o_ref[...] = acc


def _linear(x, w, *, norm_w=None, res=None, tm=512, tn=1024, name="linear"):
    m, k = x.shape
    n = w.shape[1]
    tm = min(tm, m)
    tn = min(tn, n)
    assert m % tm == 0 and n % tn == 0
    args = [x]
    specs = [pl.BlockSpec((tm, k), lambda i, j: (i, 0))]
    if norm_w is not None:
        args.append(norm_w.reshape(1, k))
        specs.append(pl.BlockSpec((1, k), lambda i, j: (0, 0)))
    args.append(w.astype(BF16))
    specs.append(pl.BlockSpec((k, tn), lambda i, j: (0, j)))
    if res is not None:
        args.append(res)
        specs.append(pl.BlockSpec((tm, tn), lambda i, j: (i, j)))
    body = functools.partial(_linear_body, has_norm=norm_w is not None, has_res=res is not None)
    return pl.pallas_call(
        body,
        grid=(m // tm, n // tn),
        in_specs=specs,
        out_specs=pl.BlockSpec((tm, tn), lambda i, j: (i, j)),
        out_shape=jax.ShapeDtypeStruct((m, n), F32),
        compiler_params=_cparams(("parallel", "arbitrary")),
        name=name,
    )(*args)


W_IN_SEGMENTS = ((0, 2048), (2576, 5648), (5904, 7952), (2048, 2560), (5648, 5904), (2560, 2576))


def _permute_body(w_ref, o_ref):
    w = w_ref[...]
    parts = [w[:, a:b] for a, b in W_IN_SEGMENTS]
    parts.append(jnp.zeros((w.shape[0], LANES - H_SSD), F32))
    o_ref[...] = jnp.concatenate(parts, axis=1).astype(BF16)


def _permute_w_in(w):
    k, n = w.shape
    tr = 128
    return pl.pallas_call(
        _permute_body,
        grid=(k // tr,),
        in_specs=[pl.BlockSpec((tr, n), lambda i: (i, 0))],
        out_specs=pl.BlockSpec((tr, D_IN_PAD), lambda i: (i, 0)),
        out_shape=jax.ShapeDtypeStruct((k, D_IN_PAD), BF16),
        compiler_params=_cparams(("parallel",)),
        name="permute_w_in",
    )(w)


def _ssd_init(z_ref, xs_ref, bc_ref, dt_ref, convprev_ref, ssmprev_ref, convw_ref, convb_ref, dtb_ref,
              alog_ref, dskip_ref, normw_ref, expand_ref, y_ref, state_ref, ext_ref):
    ext_ref[0:8, :] = convprev_ref[...]
    state_ref[...] = ssmprev_ref[...]


def _ssd_main(z_ref, xs_ref, bc_ref, dt_ref, convprev_ref, ssmprev_ref, convw_ref, convb_ref, dtb_ref,
              alog_ref, dskip_ref, normw_ref, expand_ref, y_ref, state_ref, ext_ref, *, l, n_valid):
    ext_ref[8:8 + l, 0:D_MODEL] = xs_ref[...]
    ext_ref[8:8 + l, D_MODEL:CONV_DIM] = bc_ref[...]
    conv = convb_ref[...]
    for i in range(CONV_K):
        conv = conv + ext_ref[pl.ds(5 + i, l), :] * convw_ref[i:i + 1, :]
    ext_ref[0:8, :] = ext_ref[l:l + 8, :]

    yield
    act = conv * _sigmoid(conv)
    xs = act[:, 0:D_MODEL]
    bm = act[:, D_MODEL:D_MODEL + 256]
    cm = act[:, D_MODEL + 256:CONV_DIM]

    dt = _softplus(dt_ref[...] + dtb_ref[...])
    if n_valid < l:
        dt = jnp.where(_iota((l, LANES), 0) < n_valid, dt, 0.0)
    ad = dt * (-jnp.exp(alog_ref[...]))
    row = _iota((l, l), 0)
    col = _iota((l, l), 1)
    causal = row >= col
    cs = _dot_sel3(jnp.where(causal, 1.0, 0.0).astype(BF16), ad)
    eye = jnp.where(_iota((LANES, LANES), 0) == _iota((LANES, LANES), 1), 1.0, 0.0).astype(BF16)
    cs_parts = _split3(cs)
    cs_t = sum(_dot_nt(eye, part) for part in cs_parts)
    expand = expand_ref[...]
    dt_e = sum(_dot(part, expand) for part in _split3(dt))
    cs_e = sum(_dot(part, expand) for part in cs_parts)
    yield
    xdt = xs * dt_e
    decay_in = jnp.exp(cs_e)
    decay_out = jnp.exp(cs_e[l - 1:l, :] - cs_e)
    xdt_dec = (xdt * decay_out).astype(BF16)
    lane_lo = _iota((l, LANES), 1) < SSD_HEADDIM
    row_lo = _iota((LANES, 1), 0) < SSD_HEADDIM

    ys = []
    for g in range(2):
        bg = bm[:, g * LANES:(g + 1) * LANES].astype(BF16)
        cg = cm[:, g * LANES:(g + 1) * LANES].astype(BF16)
        cb = _dot_nt(cg, bg)
        for pr in range(4 * g, 4 * g + 4):
            sl = slice(pr * LANES, (pr + 1) * LANES)
            h0, h1 = 2 * pr, 2 * pr + 1
            m0 = jnp.where(causal, jnp.exp(cs[:, h0:h0 + 1] - cs_t[h0:h0 + 1, :]), 0.0) * cb
            m1 = jnp.where(causal, jnp.exp(cs[:, h1:h1 + 1] - cs_t[h1:h1 + 1, :]), 0.0) * cb
            xp = xdt[:, sl]
            x_lo = jnp.where(lane_lo, xp, 0.0).astype(BF16)
            x_hi = jnp.where(lane_lo, 0.0, xp).astype(BF16)
            y_diag = _dot(m0.astype(BF16), x_lo) + _dot(m1.astype(BF16), x_hi)
            s_pair = state_ref[2 * pr:2 * pr + 2].reshape(LANES, SSD_STATE)
            y_off = _dot_nt(cg, s_pair.astype(BF16)) * decay_in[:, sl]
            ys.append(y_diag + y_off + dskip_ref[:, sl] * xs[:, sl])
            upd = _dot_tn(xdt_dec[:, sl], bg)
            last = cs[l - 1:l, :]
            row_decay = jnp.exp(jnp.where(row_lo, last[:, h0:h0 + 1], last[:, h1:h1 + 1]))
            state_ref[2 * pr:2 * pr + 2] = (row_decay * s_pair + upd).reshape(2, SSD_HEADDIM, SSD_STATE)
            yield
    y = jnp.concatenate(ys, axis=1)
    z = z_ref[...]
    y_ref[...] = _rmsnorm(y * (z * _sigmoid(z)), normw_ref[...])


def _ssd(proj3, conv_prev8, ssm_prev, p, *, l, n_valid, nb):
    b, lp, _ = proj3.shape
    nc = lp // l
    full = lambda shape: pl.BlockSpec(shape, lambda i, c: (0,) * len(shape))
    state = pl.BlockSpec((nb, H_SSD, SSD_HEADDIM, SSD_STATE), lambda i, c: (i, 0, 0, 0))
    body = _per_sequence(nb, [True] * 6 + [False] * 7 + [True] * 3, _ssd_init,
                         functools.partial(_ssd_main, l=l, n_valid=n_valid))
    return pl.pallas_call(
        body,
        grid=(b // nb, nc),
        in_specs=[
            pl.BlockSpec((nb, l, 1024), lambda i, c: (i, c, COL_Z)),
            pl.BlockSpec((nb, l, 1024), lambda i, c: (i, c, COL_XS)),
            pl.BlockSpec((nb, l, 512), lambda i, c: (i, c, COL_BC)),
            pl.BlockSpec((nb, l, 128), lambda i, c: (i, c, COL_DT)),
            pl.BlockSpec((nb, 8, CONV_DIM), lambda i, c: (i, 0, 0)),
            state,
            full((CONV_K, CONV_DIM)), full((1, CONV_DIM)), full((1, 128)), full((1, 128)),
            full((1, 1024)), full((1, 1024)), full((128, 1024)),
        ],
        out_specs=[pl.BlockSpec((nb, l, 1024), lambda i, c: (i, c, 0)), state],
        out_shape=[jax.ShapeDtypeStruct((b, lp, 1024), F32),
                   jax.ShapeDtypeStruct((b, H_SSD, SSD_HEADDIM, SSD_STATE), F32)],
        scratch_shapes=[pltpu.VMEM((nb, l + 8, CONV_DIM), F32)],
        compiler_params=_cparams(("parallel", "arbitrary")),
        name="ssd",
    )(proj3, proj3, proj3, proj3, conv_prev8, ssm_prev, p["conv_w"], p["conv_b"], p["dt_bias"], p["a_log"],
      p["d_skip_e"], p["ssd_norm_w"], p["expand"])


def _dot_w(x, w_ref):
    xh, xl = _split_bf16(x)
    return _dot(xh, w_ref[0]) + _dot(xl, w_ref[0]) + _dot(xh, w_ref[1])


def _split3(x):
    x1 = x.astype(BF16)
    r1 = x - x1.astype(F32)
    x2 = r1.astype(BF16)
    x3 = (r1 - x2.astype(F32)).astype(BF16)
    return x1, x2, x3


def _dot_sel3(sel_bf16, x):
    return sum(_dot(sel_bf16, part) for part in _split3(x))


def _head_sum(x, seg_pair):
    rows = x.shape[0]
    n = x.shape[1] // LANES
    stacked = jnp.concatenate([x[:, i * LANES:(i + 1) * LANES] for i in range(n)], axis=0)
    s = _dot_onehot(stacked, seg_pair)
    return jnp.concatenate([s[i * rows:(i + 1) * rows] for i in range(n)], axis=1)


def _rwkv_init(r_ref, k_ref, v_ref, lr_ref, shprev_ref, wkvprev_ref, mu_ref, w0_ref, w2_ref, a0_ref, a2_ref,
               g2_ref, kk_ref, ka_ref, rk_ref, lnw_ref, lnb_ref, seg_ref, y_ref, wkv_ref, ext_ref, state_ref):
    same_head = (_iota((LANES, LANES), 0) < HD_RWKV) == (_iota((LANES, LANES), 1) < HD_RWKV)
    ext_ref[0:8, :] = shprev_ref[...]
    for p in range(8):
        s_c = wkvprev_ref[2 * p:2 * p + 2].reshape(LANES, HD_RWKV)
        state_ref[p] = jnp.where(same_head, jnp.concatenate([s_c, s_c], axis=1), 0.0)


def _rwkv_fini(r_ref, k_ref, v_ref, lr_ref, shprev_ref, wkvprev_ref, mu_ref, w0_ref, w2_ref, a0_ref, a2_ref,
               g2_ref, kk_ref, ka_ref, rk_ref, lnw_ref, lnb_ref, seg_ref, y_ref, wkv_ref, ext_ref, state_ref):
    for p in range(8):
        s_p = state_ref[p]
        wkv_ref[2 * p] = s_p[0:HD_RWKV, 0:HD_RWKV]
        wkv_ref[2 * p + 1] = s_p[HD_RWKV:LANES, HD_RWKV:LANES]


def _rwkv_main(r_ref, k_ref, v_ref, lr_ref, shprev_ref, wkvprev_ref, mu_ref, w0_ref, w2_ref, a0_ref, a2_ref,
               g2_ref, kk_ref, ka_ref, rk_ref, lnw_ref, lnb_ref, seg_ref, y_ref, wkv_ref, ext_ref, state_ref,
               *, C, n_valid):
    ext_ref[8:8 + C, 0:1024] = r_ref[...]
    ext_ref[8:8 + C, 1024:2048] = k_ref[...]
    ext_ref[8:8 + C, 2048:3072] = v_ref[...]
    ext_ref[8:8 + C, 3072:3328] = lr_ref[...]
    cur = ext_ref[8:8 + C, :]
    prev = ext_ref[pl.ds(7, C), :]
    mixed = cur + (prev - cur) * mu_ref[...]
    ext_ref[0:8, :] = ext_ref[C:C + 8, :]

    yield
    r = mixed[:, 0:1024]
    k = mixed[:, 1024:2048]
    v = mixed[:, 2048:3072]
    wa = mixed[:, 3072:3200]
    gl = mixed[:, 3200:3328]
    seg = seg_ref[...]

    wlog = -_softplus(-(w0_ref[...] + _dot_w(jnp.tanh(wa), w2_ref))) - 0.5
    lw = -jnp.exp(wlog)
    a_sig = _sigmoid(a0_ref[...] + _dot_w(wa, a2_ref))
    g = _dot_w(_sigmoid(gl), g2_ref)
    yield
    kk = k * kk_ref[...]
    kk = kk * lax.rsqrt(jnp.maximum(_head_sum(kk * kk, seg), 1e-24))
    kh = k * (1.0 + (a_sig - 1.0) * ka_ref[...])
    av = -kk
    bv = kk * a_sig
    bonus = _head_sum(r * kh * rk_ref[...], seg) * v
    if n_valid < C:
        valid = _iota((C, 1024), 0) < n_valid
        lw = jnp.where(valid, lw, 0.0)
        av = jnp.where(valid, av, 0.0)
        bv = jnp.where(valid, bv, 0.0)
        kh = jnp.where(valid, kh, 0.0)
        v = jnp.where(valid, v, 0.0)

    yield
    causal_c = jnp.where(_iota((C, C), 0) >= _iota((C, C), 1), 1.0, 0.0).astype(BF16)
    cl = _dot_sel3(causal_c, lw)
    e_pos = jnp.exp(cl)
    e_neg = jnp.exp(-cl)
    rt = r * e_pos
    at = av * jnp.exp(cl - lw)
    bt = bv * e_neg
    kt = kh * e_neg
    p_end = e_pos[C - 1:C, :]

    R2 = 2 * C
    lane_lo = _iota((C, LANES), 1) < HD_RWKV

    def stack(x):
        return jnp.concatenate([jnp.where(lane_lo, x, 0.0), jnp.where(lane_lo, 0.0, x)], axis=0)

    pairs = range(8)
    sls = [slice(p * LANES, (p + 1) * LANES) for p in pairs]
    b_s = [stack(bt[:, sl]) for sl in sls]
    k_s = [stack(kt[:, sl]) for sl in sls]
    v_s = [stack(v[:, sl]) for sl in sls]
    yield
    s_bd = [state_ref[p] for p in pairs]
    wide = R2 == LANES
    if wide:
        tok = _iota((C, LANES), 1) & (C - 1)
        row = _iota((C, LANES), 0)
        strict = tok < row
        incl = tok <= row
        ar = [jnp.concatenate([at[:, sl], rt[:, sl]], axis=0) for sl in sls]
        bk = [jnp.concatenate([b_s[p], k_s[p]], axis=0) for p in pairs]
        quad = [_cmm(ar[p], bk[p], NT) for p in pairs]
        ars = [_cmm(ar[p], s_bd[p], NT) for p in pairs]
        m_ak = [jnp.where(strict, quad[p][0:C, R2:], 0.0) for p in pairs]
        w = [ars[p][0:C] + _cmm(m_ak[p], v_s[p], NN) for p in pairs]
    else:
        rr = _iota((R2, R2), 0)
        cc = _iota((R2, R2), 1)
        strict = rr > cc
        incl = rr >= cc
        a_s = [stack(at[:, sl]) for sl in sls]
        r_s = [stack(rt[:, sl]) for sl in sls]
        m_ak = [jnp.where(strict, _cmm(a_s[p], k_s[p], NT), 0.0) for p in pairs]
        w = [_cmm(a_s[p], s_bd[p], NT) + _cmm(m_ak[p], v_s[p], NN) for p in pairs]
    yield
    if n_valid > 1:
        assert wide
        n_sq = C.bit_length() - 1
        xp = [jnp.where(strict, quad[p][0:C, 0:R2], 0.0) for p in pairs]
        tp = [jnp.where(tok == row, 1.0, 0.0) + xp[p] for p in pairs]
        xp = [_cmm(xp[p], stack(xp[p]), NN) for p in pairs]
        for _ in range(1, n_sq - 1):
            prod = [_cmm(jnp.concatenate([xp[p], tp[p]], axis=0), stack(xp[p]), NN) for p in pairs]
            tp = [tp[p] + prod[p][C:R2] for p in pairs]
            xp = [prod[p][0:C] for p in pairs]
            yield
        tp = [tp[p] + _cmm(tp[p], stack(xp[p]), NN) for p in pairs]
        u = [_cmm(tp[p], stack(w[p]), NN) for p in pairs]
    else:
        u = w
    yield
    ys = []
    for p in pairs:
        if wide:
            n_r = jnp.where(jnp.concatenate([incl, incl], axis=1), quad[p][C:R2, :], 0.0)
            uv = jnp.concatenate([stack(u[p]), v_s[p]], axis=0)
            ys.append(ars[p][C:R2] + _cmm(n_r, uv, NN))
            upd = _cmm(uv, bk[p], TN)
        else:
            n_rb = jnp.where(incl, _cmm(r_s[p], b_s[p], NT), 0.0)
            n_rk = jnp.where(incl, _cmm(r_s[p], k_s[p], NT), 0.0)
            y2 = _cmm(r_s[p], s_bd[p], NT) + _cmm(n_rb, u[p], NN) + _cmm(n_rk, v_s[p], NN)
            ys.append(y2[0:C] + y2[C:R2])
            upd = _cmm(u[p], b_s[p], TN) + _cmm(v_s[p], k_s[p], TN)
        state_ref[p] = (s_bd[p] + upd) * p_end[:, sls[p]]
    yield
    yr = jnp.concatenate(ys, axis=1)
    mean = _head_sum(yr, seg) * (1.0 / HD_RWKV)
    d = yr - mean
    var = _head_sum(d * d, seg) * (1.0 / HD_RWKV)
    yn = d * lax.rsqrt(var + RWKV_LN_EPS) * lnw_ref[...] + lnb_ref[...]
    y_ref[...] = (yn + bonus) * g


def _rwkv(proj3, shift_prev8, wkv_prev, p, *, C, n_valid, nb):
    b, lp, _ = proj3.shape
    nc = lp // C
    full = lambda shape: pl.BlockSpec(shape, lambda i, c: (0,) * len(shape))
    state = pl.BlockSpec((nb, H_RWKV, HD_RWKV, HD_RWKV), lambda i, c: (i, 0, 0, 0))
    body = _per_sequence(nb, [True] * 6 + [False] * 12 + [True] * 4, _rwkv_init,
                         functools.partial(_rwkv_main, C=C, n_valid=n_valid), _rwkv_fini)
    return pl.pallas_call(
        body,
        grid=(b // nb, nc),
        in_specs=[
            pl.BlockSpec((nb, C, 1024), lambda i, c: (i, c, COL_R)),
            pl.BlockSpec((nb, C, 1024), lambda i, c: (i, c, COL_K)),
            pl.BlockSpec((nb, C, 1024), lambda i, c: (i, c, COL_V)),
            pl.BlockSpec((nb, C, 256), lambda i, c: (i, c, COL_LR)),
            pl.BlockSpec((nb, 8, RWKV_PROJ), lambda i, c: (i, 0, 0)),
            state,
            full((1, RWKV_PROJ)), full((1, 1024)), full((2, 128, 1024)), full((1, 1024)), full((2, 128, 1024)),
            full((2, 128, 1024)), full((1, 1024)), full((1, 1024)), full((1, 1024)), full((1, 1024)), full((1, 1024)),
            full((128, 128)),
        ],
        out_specs=[pl.BlockSpec((nb, C, 1024), lambda i, c: (i, c, 0)), state],
        out_shape=[jax.ShapeDtypeStruct((b, lp, 1024), F32), jax.ShapeDtypeStruct((b, H_RWKV, HD_RWKV, HD_RWKV), F32)],
        scratch_shapes=[pltpu.VMEM((nb, C + 8, RWKV_PROJ), F32), pltpu.VMEM((nb, 8, 128, 128), F32)],
        compiler_params=_cparams(("parallel", "arbitrary")),
        name="rwkv",
    )(proj3, proj3, proj3, proj3, shift_prev8, wkv_prev, p["rwkv_mu"], p["rwkv_w0"], p["rwkv_w2p"], p["rwkv_a0"],
      p["rwkv_a2p"], p["rwkv_g2"], p["rwkv_k_k"], p["rwkv_k_a"], p["rwkv_r_k"], p["rwkv_ln_w"], p["rwkv_ln_b"],
      p["seg"])


def _merge_body(x_ref, gs_ref, gr_ref, ys_ref, yr_ref, w_ref, o_ref):
    merged = _sigmoid(gs_ref[...]) * ys_ref[...] + _sigmoid(gr_ref[...]) * yr_ref[...]
    o_ref[...] = x_ref[...] + _dot(merged.astype(BF16), w_ref[...])


def _merge_out(x, proj, y_ssd, y_rwkv, w_out, *, tm=512):
    m = x.shape[0]
    tm = min(tm, m)
    row = lambda j: pl.BlockSpec((tm, 1024), lambda i: (i, j))
    return pl.pallas_call(
        _merge_body,
        grid=(m // tm,),
        in_specs=[row(0), row(COL_GS), row(COL_GR), row(0), row(0), pl.BlockSpec((1024, 1024), lambda i: (0, 0))],
        out_specs=row(0),
        out_shape=jax.ShapeDtypeStruct((m, 1024), F32),
        compiler_params=_cparams(("parallel",)),
        name="merge_out",
    )(x, proj, proj, y_ssd, y_rwkv, w_out.astype(BF16))


def _mem_attention(q, k_ref, v_ref):
    outs = []
    for h in range(MEM_HEADS):
        sl = slice(h * MEM_HD, (h + 1) * MEM_HD)
        s = _dot_nt(q[:, sl].astype(BF16), k_ref[:, sl].astype(BF16)) * (MEM_HD ** -0.5)
        s = s - jnp.max(s, axis=-1, keepdims=True)
        e = jnp.exp(s)
        prob = e / jnp.sum(e, axis=-1, keepdims=True)
        outs.append(_dot(prob.astype(BF16), v_ref[:, sl].astype(BF16)))
    return jnp.concatenate(outs, axis=1)


def _tail_body(x_ref, gs_ref, gr_ref, ys_ref, yr_ref, k_ref, v_ref, wout_ref, nmem_ref, wmq_ref, wmo_ref,
               nffn_ref, wpq_ref, x2_ref, pq_ref):
    merged = _sigmoid(gs_ref[...]) * ys_ref[...] + _sigmoid(gr_ref[...]) * yr_ref[...]
    x1 = x_ref[...] + _dot(merged.astype(BF16), wout_ref[...])
    q = _dot(_rmsnorm(x1, nmem_ref[...]).astype(BF16), wmq_ref[...])
    x2 = x1 + _dot(_mem_attention(q, k_ref, v_ref).astype(BF16), wmo_ref[...])
    x2_ref[...] = x2
    pq_ref[...] = _dot(_rmsnorm(x2, nffn_ref[...]).astype(BF16), wpq_ref[...])


def _tail(x3, proj3, y_ssd, y_rwkv, mem_k, mem_v, p, *, tm=512):
    b, L, _ = x3.shape
    once = pl.Buffered(1)
    tile = lambda j: pl.BlockSpec((None, tm, 1024), lambda i, t: (i, t, j))
    mem = pl.BlockSpec((None, N_MEM, 1024), lambda i, t: (i, 0, 0))
    full = lambda r, c: pl.BlockSpec((r, c), lambda i, t: (0, 0), pipeline_mode=once)
    return pl.pallas_call(
        _tail_body,
        grid=(b, L // tm),
        in_specs=[tile(0), tile(COL_GS), tile(COL_GR), tile(0), tile(0), mem, mem,
                  full(1024, 1024), full(1, 1024), full(1024, 1024), full(1024, 1024), full(1, 1024),
                  full(1024, 2048)],
        out_specs=[tile(0), pl.BlockSpec((None, tm, 2048), lambda i, t: (i, t, 0))],
        out_shape=[jax.ShapeDtypeStruct((b, L, 1024), F32), jax.ShapeDtypeStruct((b, L, 2048), F32)],
        compiler_params=_cparams(("parallel", "arbitrary")),
        name="token_tail",
    )(x3, proj3, proj3, y_ssd, y_rwkv, mem_k, mem_v, p["w_out"].astype(BF16), p["norm_mem_w"].reshape(1, 1024),
      p["w_mq"].astype(BF16), p["w_mo"].astype(BF16), p["norm_ffn_w"].reshape(1, 1024), p["w_pq"].astype(BF16))


def _attn_step_body(q_ref, k_ref, v_ref, o_ref):
    s = jnp.sum(k_ref[...] * q_ref[...][:, None], axis=-1, keepdims=True) * (MEM_HD ** -0.5)
    e = jnp.exp(s - jnp.max(s, axis=1, keepdims=True))
    prob = e / jnp.sum(e, axis=1, keepdims=True)
    o_ref[...] = jnp.sum(prob * v_ref[...], axis=1)


def _attn_step(q, mem_k, mem_v, *, nb=4):
    b = q.shape[0]
    qspec = pl.BlockSpec((nb, MEM_HEADS, MEM_HD), lambda i: (i, 0, 0))
    kvspec = pl.BlockSpec((nb, N_MEM, MEM_HEADS, MEM_HD), lambda i: (i, 0, 0, 0))
    return pl.pallas_call(
        _attn_step_body,
        grid=(b // nb,),
        in_specs=[qspec, kvspec, kvspec],
        out_specs=qspec,
        out_shape=jax.ShapeDtypeStruct((b, MEM_HEADS, MEM_HD), F32),
        compiler_params=_cparams(("parallel",)),
        name="mem_attn_step",
    )(q, mem_k, mem_v)


def _stage2_pairs():
    return [(r1, r2) for r1 in range(PEER_TOPK) for r2 in range(PEER_TOPK) if (r1 + 1) * (r2 + 1) <= PEER_TOPK]


N_CAND = 56


def _topk_body(q_ref, keys_ref, gate_ref, i1_ref, i2_ref, s_scr, i_scr, cand_scr, code_scr, res_scr, *, T):
    neg = -jnp.inf
    key_row = _iota((N_KEYS, T), 0).astype(F32)
    cand_row = _iota((N_CAND, T), 0).astype(F32)
    pairs = _stage2_pairs()
    q = q_ref[...]
    for h in range(PEER_HEADS):
        for c in range(2):
            hc = 2 * h + c
            s = _dot_nt(keys_ref[hc], q[:, hc * LANES:(hc + 1) * LANES], HI)
            for r in range(PEER_TOPK):
                m = jnp.max(s, axis=0, keepdims=True)
                idx = jnp.min(jnp.where(s == m, key_row, float(N_KEYS)), axis=0, keepdims=True)
                s_scr[c, r:r + 1, :] = m
                i_scr[c, r:r + 1, :] = idx
                s = jnp.where(key_row == idx, neg, s)
        cand_scr[...] = jnp.full((N_CAND, T), neg, F32)
        code_scr[...] = jnp.zeros((N_CAND, T), F32)
        for j, (r1, r2) in enumerate(pairs):
            cand_scr[j:j + 1, :] = s_scr[0, r1:r1 + 1, :] + s_scr[1, r2:r2 + 1, :]
            code_scr[j:j + 1, :] = i_scr[0, r1:r1 + 1, :] * float(N_KEYS) + i_scr[1, r2:r2 + 1, :]
        cand = cand_scr[...]
        code = code_scr[...]
        for r in range(PEER_TOPK):
            m = jnp.max(cand, axis=0, keepdims=True)
            idx = jnp.min(jnp.where(cand == m, cand_row, float(N_CAND)), axis=0, keepdims=True)
            hit = cand_row == idx
            res_scr[0, r:r + 1, :] = m
            res_scr[1, r:r + 1, :] = jnp.sum(jnp.where(hit, code, 0.0), axis=0, keepdims=True)
            cand = jnp.where(hit, neg, cand)
        top = res_scr[0]
        e = jnp.exp(top - jnp.max(top, axis=0, keepdims=True))
        rows = slice(h * PEER_TOPK, (h + 1) * PEER_TOPK)
        gate_ref[rows, :] = e / jnp.sum(e, axis=0, keepdims=True)
        first = jnp.floor(res_scr[1] * (1.0 / N_KEYS))
        i1_ref[rows, :] = first
        i2_ref[rows, :] = res_scr[1] - first * float(N_KEYS)


def _peer_topk(q, sub_keys, *, T=256):
    m = q.shape[0]
    T = min(T, m)
    body = functools.partial(_topk_body, T=T)
    out = pl.BlockSpec((128, T), lambda i: (0, i))
    return pl.pallas_call(
        body,
        grid=(m // T,),
        in_specs=[pl.BlockSpec((T, 2048), lambda i: (i, 0)), pl.BlockSpec((16, 128, 128), lambda i: (0, 0, 0))],
        out_specs=[out, out, out],
        out_shape=[jax.ShapeDtypeStruct((128, m), F32)] * 3,
        scratch_shapes=[pltpu.VMEM((2, PEER_TOPK, T), F32), pltpu.VMEM((2, PEER_TOPK, T), F32),
                        pltpu.VMEM((N_CAND, T), F32), pltpu.VMEM((N_CAND, T), F32),
                        pltpu.VMEM((2, PEER_TOPK, T), F32)],
        compiler_params=_cparams(("parallel",)),
        name="peer_topk",
    )(q, sub_keys.reshape(16, 128, 128))


I1_PER_STEP = 8
G_PITCH = 136


def _peer_body(x_ref, gate_ref, i1_ref, i2_ref, nw_ref, fw_ref, ut_ref, v_ref, o_ref, hn_scr, g_scr, coef_scr, *, T):
    j = pl.program_id(1)

    @pl.when(j == 0)
    def _():
        hn_scr[...] = _rmsnorm(x_ref[...], nw_ref[...]).astype(BF16)
        o_ref[...] = jnp.zeros_like(o_ref)
        shape = (N_KEYS, LANES)
        key_row = _iota(shape, 0).astype(F32).astype(BF16)
        one = jnp.ones(shape, BF16)
        zero = jnp.zeros(shape, BF16)

        def build(t, carry):
            row = lambda ref: jnp.broadcast_to(ref[pl.ds(t, 1), :].astype(BF16), shape)
            a_t = jnp.where(key_row == row(i1_ref), row(gate_ref), zero)
            b_t = jnp.where(key_row == row(i2_ref), one, zero)
            g_scr[pl.ds(pl.multiple_of(t * G_PITCH, 8), N_KEYS), :] = _dot_nt(a_t, b_t)
            return carry

        lax.fori_loop(0, T, build, 0, unroll=64)

    hn = hn_scr[...]
    for cidx in range(I1_PER_STEP // 2):
        act = _dot(hn, ut_ref[:, cidx * 2 * LANES:(cidx + 1) * 2 * LANES])
        ge = 0.5 * act * (1.0 + lax.erf(act * (2.0 ** -0.5)))
        for half in range(2):
            ii = 2 * cidx + half
            g_t = g_scr[pl.ds(j * I1_PER_STEP + ii, T, stride=G_PITCH), :]
            coef_scr[:, ii * LANES:(ii + 1) * LANES] = (g_t * ge[:, half * LANES:(half + 1) * LANES]).astype(BF16)
    o_ref[...] += _dot(coef_scr[...], v_ref[...])

    @pl.when(j == pl.num_programs(1) - 1)
    def _():
        o_ref[...] = _rmsnorm(x_ref[...] + o_ref[...], fw_ref[...])


def _peer_mix(x, gate, i1, i2, norm_w, final_w, ut_bf16, v_bf16, *, T=512):
    m = x.shape[0]
    T = min(T, m)
    ne = I1_PER_STEP * N_KEYS
    body = functools.partial(_peer_body, T=T)
    once = pl.Buffered(1)
    row = lambda w: pl.BlockSpec((T, w), lambda i, j: (i, 0), pipeline_mode=once)
    vec = pl.BlockSpec((1, 1024), lambda i, j: (0, 0), pipeline_mode=once)
    return pl.pallas_call(
        body,
        grid=(m // T, N_KEYS // I1_PER_STEP),
        in_specs=[row(1024), row(128), row(128), row(128), vec, vec,
                  pl.BlockSpec((1024, ne), lambda i, j: (0, j)), pl.BlockSpec((ne, 1024), lambda i, j: (j, 0))],
        out_specs=pl.BlockSpec((T, 1024), lambda i, j: (i, 0)),
        out_shape=jax.ShapeDtypeStruct((m, 1024), F32),
        scratch_shapes=[pltpu.VMEM((T, 1024), BF16), pltpu.VMEM((T * G_PITCH, LANES), F32),
                        pltpu.VMEM((T, 1024), BF16)],
        compiler_params=_cparams(("parallel", "arbitrary")),
        name="peer_mix",
    )(x, gate, i1, i2, norm_w.reshape(1, 1024), final_w.reshape(1, 1024), ut_bf16, v_bf16)


def _pad_rows(x, rows):
    return jnp.pad(x, ((0, 0), (rows - x.shape[1], 0), (0, 0)))


def _block(x, mem_k, mem_v, conv_prev, ssm_prev, shift_prev, wkv_prev, p, *, ssd_l, rwkv_c):
    b, L, _ = x.shape
    m = b * L
    x2d = x.reshape(m, D_MODEL)
    proj = _linear(x2d, p["w_in"], norm_w=p["norm_mix_w"], tm=2048, tn=1152, name="in_proj")
    proj3 = proj.reshape(b, L, D_IN_PAD)
    lp = max(L, SUBLANES)
    n_valid = L if L < SUBLANES else max(ssd_l, rwkv_c)
    proj3p = proj3 if lp == L else jnp.pad(proj3, ((0, 0), (0, lp - L), (0, 0)))

    nb = SEQS_PER_STEP if L == 1 else SSD_SEQS_PER_STEP
    y_ssd, ssm_new = _ssd(proj3p, _pad_rows(conv_prev, 8), ssm_prev, p,
                          l=ssd_l, n_valid=min(n_valid, ssd_l), nb=nb)
    y_rwkv, wkv_new = _rwkv(proj3p, _pad_rows(shift_prev[:, None, :], 8), wkv_prev, p,
                            C=rwkv_c, n_valid=min(n_valid, rwkv_c), nb=nb if L == 1 else RWKV_SEQS_PER_STEP)
    tail = proj3[:, max(L - (CONV_K - 1), 0):]
    xbc = jnp.concatenate([tail[..., 1024:2048], tail[..., 7168:7680]], axis=-1)
    rw = jnp.concatenate([tail[..., 2048:5120], tail[..., 7680:7936]], axis=-1)
    conv_new = jnp.concatenate([conv_prev, xbc], axis=1)[:, -(CONV_K - 1):]
    shift_new = rw[:, -1]

    if L == 1:
        x1 = _merge_out(x2d, proj, y_ssd[:, 0], y_rwkv[:, 0], p["w_out"])
        q = _linear(x1, p["w_mq"], norm_w=p["norm_mem_w"], name="mem_q")
        o = _attn_step(q.reshape(b, MEM_HEADS, MEM_HD), mem_k, mem_v)
        x2 = _linear(o.reshape(m, D_MODEL), p["w_mo"], res=x1, name="mem_o")
        pq = _linear(x2, p["w_pq"], norm_w=p["norm_ffn_w"], name="peer_q")
    else:
        x2, pq = _tail(x, proj3, y_ssd, y_rwkv, mem_k.reshape(b, N_MEM, D_MODEL), mem_v.reshape(b, N_MEM, D_MODEL), p)
        x2, pq = x2.reshape(m, D_MODEL), pq.reshape(m, 2 * D_MODEL)
    gate_t, i1_t, i2_t = _peer_topk(pq, p["sub_keys"])
    y = _peer_mix(x2, gate_t.T, i1_t.T, i2_t.T, p["norm_ffn_w"], p["norm_final_w"], p["expert_ut"], p["expert_v"])
    return y.reshape(b, L, D_MODEL), conv_new, ssm_new, shift_new, wkv_new


def kernel(x_prompt, x_sample, mem_prompt, state_ssm, state_conv, state_wkv, state_shift, cache_mem_k, cache_mem_v, norm_mix_w, w_in, conv_w, conv_b, dt_bias, a_log, d_skip, ssd_norm_w, rwkv_mu, rwkv_w0, rwkv_w2, rwkv_a0, rwkv_a2, rwkv_g2, rwkv_k_k, rwkv_k_a, rwkv_r_k, rwkv_ln_w, rwkv_ln_b, w_out, norm_mem_w, mem_norm_w, w_mk, w_mv, w_mq, w_mo, norm_ffn_w, w_pq, sub_keys, expert_u, expert_v, norm_final_w):
    assert w_in.shape[0] == 1, "single-layer problem"
    bp = x_prompt.shape[0]
    bs = x_sample.shape[0]
    w_in_perm = _permute_w_in(w_in[0])
    pad16 = lambda a: jnp.pad(a.reshape(1, H_SSD), ((0, 0), (0, LANES - H_SSD)))
    head_of_lane = jnp.arange(D_MODEL) // SSD_HEADDIM
    zeros64 = jnp.zeros((64, D_MODEL), F32)
    hi_lo = lambda w: jnp.stack(_split_bf16(w))
    p = {
        "norm_mix_w": norm_mix_w[0], "w_in": w_in_perm,
        "conv_w": conv_w[0], "conv_b": conv_b[0].reshape(1, CONV_DIM),
        "dt_bias": pad16(dt_bias[0]), "a_log": pad16(a_log[0]),
        "d_skip_e": jnp.repeat(d_skip[0], SSD_HEADDIM).reshape(1, D_MODEL),
        "ssd_norm_w": ssd_norm_w[0].reshape(1, D_MODEL),
        "expand": (jnp.arange(LANES)[:, None] == head_of_lane[None, :]).astype(BF16),
        "rwkv_mu": rwkv_mu[0].reshape(1, RWKV_PROJ), "rwkv_w0": rwkv_w0[0].reshape(1, D_MODEL),
        "rwkv_w2p": hi_lo(jnp.concatenate([rwkv_w2[0], zeros64], axis=0)),
        "rwkv_a0": rwkv_a0[0].reshape(1, D_MODEL),
        "rwkv_a2p": hi_lo(jnp.concatenate([zeros64, rwkv_a2[0]], axis=0)),
        "rwkv_g2": hi_lo(rwkv_g2[0]), "rwkv_k_k": rwkv_k_k[0].reshape(1, D_MODEL),
        "rwkv_k_a": rwkv_k_a[0].reshape(1, D_MODEL), "rwkv_r_k": rwkv_r_k[0].reshape(1, D_MODEL),
        "rwkv_ln_w": rwkv_ln_w[0].reshape(1, D_MODEL), "rwkv_ln_b": rwkv_ln_b[0].reshape(1, D_MODEL),
        "seg": (head_of_lane[:LANES, None] == head_of_lane[None, :LANES]).astype(BF16),
        "w_out": w_out[0], "norm_mem_w": norm_mem_w[0], "w_mq": w_mq[0], "w_mo": w_mo[0],
        "norm_ffn_w": norm_ffn_w[0], "w_pq": w_pq[0], "sub_keys": sub_keys[0],
        "expert_ut": expert_u[0].astype(BF16).T, "expert_v": expert_v[0].astype(BF16),
        "norm_final_w": norm_final_w,
    }

    mem2d = mem_prompt.reshape(bp * N_MEM, D_MODEL)
    mem_k = _linear(mem2d, w_mk[0], norm_w=mem_norm_w[0], name="mem_k").reshape(bp, N_MEM, MEM_HEADS, MEM_HD)
    mem_v = _linear(mem2d, w_mv[0], norm_w=mem_norm_w[0], name="mem_v").reshape(bp, N_MEM, MEM_HEADS, MEM_HD)
    y_p, conv_p, ssm_p, shift_p, wkv_p = _block(
        x_prompt, mem_k, mem_v,
        jnp.zeros((bp, CONV_K - 1, CONV_DIM), F32), jnp.zeros((bp, H_SSD, SSD_HEADDIM, SSD_STATE), F32),
        jnp.zeros((bp, RWKV_PROJ), F32), jnp.zeros((bp, H_RWKV, HD_RWKV, HD_RWKV), F32), p,
        ssd_l=SSD_CHUNK, rwkv_c=RWKV_CHUNK)
    y_s, conv_s, ssm_s, shift_s, wkv_s = _block(
        x_sample, cache_mem_k[0], cache_mem_v[0], state_conv[0], state_ssm[0], state_shift[0], state_wkv[0], p,
        ssd_l=SUBLANES, rwkv_c=SUBLANES)
    return (y_p, y_s, ssm_p[None], conv_p[None], wkv_p[None], shift_p[None], mem_k[None], mem_v[None],
            ssm_s[None], conv_s[None], wkv_s[None], shift_s[None])
```
